```python
import math
import jax, jax.numpy as jnp
from jax import lax
import numpy as np

D_MODEL = 1024
BATCH = 4
SEQ = 4096
DEPTH = 4
DEC_BATCH = 128
DEC_SEQ = 4
PAST_LEN = 8192
PAGE_SIZE = 128

MIX_WIDTH = D_MODEL
MEM_HEADS = 4
MEM_HEAD_DIM = 64
MEM_WIDTH = MEM_HEADS * MEM_HEAD_DIM
TOK_WIDTH = MIX_WIDTH - MEM_WIDTH
N_MEM = 256
HG_HEAD_DIM = 128
HG_HEADS = TOK_WIDTH // HG_HEAD_DIM
HG_CHUNK = 64
HG_IN = 4 * TOK_WIDTH + MEM_WIDTH
LB_MAX = 0.999
SW_HEAD_DIM = 64
SW_Q_HEADS = TOK_WIDTH // SW_HEAD_DIM
SW_KV_HEADS = 4
SW_GROUP = SW_Q_HEADS // SW_KV_HEADS
WINDOW = 128
SW_IN = (SW_Q_HEADS + 2 * SW_KV_HEADS) * SW_HEAD_DIM + MEM_WIDTH
D_FF = -(-8 * D_MODEL // (3 * 256)) * 256
N_HGRN = (DEPTH + 1) // 2
N_SWA = DEPTH // 2
EPS = 1e-6
NEG = -1e30

kernel_name = 'hgrn2_swa_sink_memory_hybrid_step'

F32 = jnp.float32


def rmsnorm(x, g):
    xf = x.astype(F32)
    y = xf * lax.rsqrt(jnp.mean(xf * xf, axis=-1, keepdims=True) + EPS) * g.astype(F32)
    return y.astype(x.dtype)


def head_rms(x, g):
    xf = x.astype(F32)
    return xf * lax.rsqrt(jnp.mean(xf * xf, axis=-1, keepdims=True) + EPS) * g.astype(F32)


def _chunk_len(L):
    return L if L <= HG_CHUNK else math.gcd(L, HG_CHUNK)


def hgrn_recurrence(q, k, v, logf, s0):
    b_, L, H, DK = q.shape
    DV = v.shape[-1]
    C = _chunk_len(L)
    n = L // C

    def to_chunks(t):
        return t.reshape(b_, n, C, H, t.shape[-1]).transpose(1, 0, 3, 2, 4)

    causal = jnp.tril(jnp.ones((C, C), bool))[:, :, None]

    def step(s, inp):
        qc, kc, vc, gc = inp
        cum = jnp.cumsum(gc, axis=2)
        diff = cum[:, :, :, None, :] - cum[:, :, None, :, :]
        decay = jnp.exp(jnp.where(causal, diff, NEG))
        att = jnp.einsum('bhtc,bhsc,bhtsc->bhts', qc, kc, decay)
        o = (jnp.einsum('bhts,bhsv->bhtv', att, vc)
             + jnp.einsum('bhtc,bhcv->bhtv', qc * jnp.exp(cum), s))
        last = cum[:, :, -1:, :]
        s_new = (jnp.exp(last[:, :, 0, :, None]) * s
                 + jnp.einsum('bhsc,bhsv->bhcv', kc * jnp.exp(last - cum), vc))
        return s_new, o

    s_fin, o = lax.scan(step, s0, (to_chunks(q), to_chunks(k), to_chunks(v), to_chunks(logf)))
    o = o.transpose(1, 0, 3, 2, 4).reshape(b_, L, H, DV)
    return o, s_fin


def hgrn_branch(p, lb, g_norm, s0):
    q, fz, i_in, g = jnp.split(p, 4, axis=-1)
    B, L, _ = q.shape
    shp = (B, L, HG_HEADS, HG_HEAD_DIM)
    qf = (jax.nn.silu(q.astype(F32)) * HG_HEAD_DIM ** -0.5).reshape(shp)
    z = fz.astype(F32)
    f = lb + (1.0 - lb) * jax.nn.sigmoid(z)
    logf = jnp.log(f).reshape(shp)
    kf = ((1.0 - lb) * jax.nn.sigmoid(-z)).reshape(shp)
    vf = i_in.astype(F32).reshape(shp)
    o, s = hgrn_recurrence(qf, kf, vf, logf, s0.astype(F32))
    o = head_rms(o, jnp.ones((HG_HEAD_DIM,), F32)).reshape(B, L, TOK_WIDTH)
    o = o * g_norm.astype(F32) * jax.nn.silu(g.astype(F32))
    return o, s


def sink_softmax(s, sinks):
    sk = sinks.astype(F32).reshape(SW_KV_HEADS, SW_GROUP, 1, 1)
    m = jnp.maximum(jnp.max(s, axis=-1, keepdims=True), sk)
    p = jnp.exp(s - m)
    return p / (jnp.sum(p, axis=-1, keepdims=True) + jnp.exp(sk - m))


def swa_prompt(q, k, v, sinks):
    B, L = q.shape[:2]
    n = L // WINDOW
    qb = q.reshape(B, n, WINDOW, SW_KV_HEADS, SW_GROUP, SW_HEAD_DIM)
    kb = k.reshape(B, n, WINDOW, SW_KV_HEADS, SW_HEAD_DIM)
    vb = v.reshape(B, n, WINDOW, SW_KV_HEADS, SW_HEAD_DIM)

    def with_prev(t):
        prev = jnp.pad(t[:, :-1], ((0, 0), (1, 0), (0, 0), (0, 0), (0, 0)))
        return jnp.concatenate([prev, t], axis=2)

    kk, vv = with_prev(kb), with_prev(vb)
    s = jnp.einsum('bnqkgd,bnskd->bnkgqs', qb, kk) * SW_HEAD_DIM ** -0.5
    qi = jnp.arange(WINDOW)[:, None]
    r = jnp.arange(2 * WINDOW)[None, :]
    band = (r > qi) & (r <= qi + WINDOW)
    has_prev = (jnp.arange(n)[:, None, None] > 0) | (r >= WINDOW)[None]
    mask = (band[None] & has_prev)[None, :, None, None]
    p = sink_softmax(jnp.where(mask, s, NEG), sinks)
    o = jnp.einsum('bnkgqs,bnskd->bnqkgd', p, vv)
    return o.reshape(B, L, SW_Q_HEADS * SW_HEAD_DIM)


def swa_sample(q, k, v, buf_k, buf_v, sinks):
    B, L = q.shape[:2]
    kk = jnp.concatenate([buf_k.astype(F32), k], axis=1)
    vv = jnp.concatenate([buf_v.astype(F32), v], axis=1)
    qg = q.reshape(B, L, SW_KV_HEADS, SW_GROUP, SW_HEAD_DIM)
    s = jnp.einsum('bqkgd,bskd->bkgqs', qg, kk) * SW_HEAD_DIM ** -0.5
    j = jnp.arange(L)[:, None]
    r = jnp.arange(WINDOW + L)[None, :]
    mask = (r > j) & (r <= j + WINDOW)
    p = sink_softmax(jnp.where(mask, s, NEG), sinks)
    o = jnp.einsum('bkgqs,bskd->bqkgd', p, vv).reshape(B, L, SW_Q_HEADS * SW_HEAD_DIM)
    return o, kk[:, -WINDOW:], vv[:, -WINDOW:]


def mem_kv(mem, norm_g, w_kv, k_norm):
    B = mem.shape[0]
    kv = rmsnorm(mem, norm_g) @ w_kv
    k, v = jnp.split(kv, 2, axis=-1)
    k = head_rms(k.reshape(B, N_MEM, MEM_HEADS, MEM_HEAD_DIM), k_norm)
    v = v.reshape(B, N_MEM, MEM_HEADS, MEM_HEAD_DIM).astype(F32)
    return k, v


def mem_attend(cq, mk, mv, q_norm):
    B, L, _ = cq.shape
    q = head_rms(cq.reshape(B, L, MEM_HEADS, MEM_HEAD_DIM), q_norm)
    s = jnp.einsum('blhd,bmhd->bhlm', q, mk.astype(F32)) * MEM_HEAD_DIM ** -0.5
    p = jax.nn.softmax(s, axis=-1)
    return jnp.einsum('bhlm,bmhd->blhd', p, mv.astype(F32)).reshape(B, L, MEM_WIDTH)


def setup_inputs(seed: int = 0) -> dict:
    key = jax.random.key(seed)
    ks = jax.random.split(key, 32)

    def nrm(k, shape, scale=1.0):
        return jax.random.normal(k, shape, F32) * scale

    def gain(k, shape):
        return 1.0 + 0.02 * jax.random.normal(k, shape, F32)

    out_scale = (2.0 * DEPTH) ** -0.5
    return {
        'x_prompt': nrm(ks[0], (BATCH, SEQ, D_MODEL)),
        'x_sample': nrm(ks[1], (DEC_BATCH, DEC_SEQ, D_MODEL)),
        'cache_mem_k': nrm(ks[2], (DEPTH, DEC_BATCH, N_MEM, MEM_HEADS, MEM_HEAD_DIM)),
        'cache_mem_v': nrm(ks[3], (DEPTH, DEC_BATCH, N_MEM, MEM_HEADS, MEM_HEAD_DIM)),
        'state_hgrn': nrm(ks[4], (N_HGRN, DEC_BATCH, HG_HEADS, HG_HEAD_DIM, HG_HEAD_DIM), 0.5),
        'cache_swa_k': nrm(ks[5], (N_SWA, DEC_BATCH, WINDOW, SW_KV_HEADS, SW_HEAD_DIM)),
        'cache_swa_v': nrm(ks[6], (N_SWA, DEC_BATCH, WINDOW, SW_KV_HEADS, SW_HEAD_DIM)),
        'mem_prompt': nrm(ks[7], (BATCH, N_MEM, D_MODEL)),
        'norm_mix': gain(ks[8], (DEPTH, D_MODEL)),
        'norm_ffn': gain(ks[9], (DEPTH, D_MODEL)),
        'norm_mem': gain(ks[10], (DEPTH, D_MODEL)),
        'w_in_hgrn': nrm(ks[11], (N_HGRN, D_MODEL, HG_IN), D_MODEL ** -0.5),
        'hgrn_lb_raw': 1.0 + 0.1 * nrm(ks[12], (N_HGRN, TOK_WIDTH)),
        'hgrn_out_norm': gain(ks[13], (N_HGRN, TOK_WIDTH)),
        'w_in_swa': nrm(ks[14], (N_SWA, D_MODEL, SW_IN), D_MODEL ** -0.5),
        'swa_q_norm': gain(ks[15], (N_SWA, SW_HEAD_DIM)),
        'swa_k_norm': gain(ks[16], (N_SWA, SW_HEAD_DIM)),
        'swa_sinks': nrm(ks[17], (N_SWA, SW_Q_HEADS), 0.5),
        'w_mem_kv': nrm(ks[18], (DEPTH, D_MODEL, 2 * MEM_WIDTH), D_MODEL ** -0.5),
        'mem_q_norm': gain(ks[19], (DEPTH, MEM_HEAD_DIM)),
        'mem_k_norm': gain(ks[20], (DEPTH, MEM_HEAD_DIM)),
        'w_out': nrm(ks[21], (DEPTH, MIX_WIDTH, D_MODEL), MIX_WIDTH ** -0.5 * out_scale),
        'w_ffn_in': nrm(ks[22], (DEPTH, D_MODEL, 2 * D_FF), D_MODEL ** -0.5),
        'w_ffn_out': nrm(ks[23], (DEPTH, D_FF, D_MODEL), D_FF ** -0.5 * out_scale),
    }


def reference(x_prompt, x_sample, cache_mem_k, cache_mem_v, state_hgrn, cache_swa_k, cache_swa_v,
              mem_prompt, norm_mix, norm_ffn, norm_mem, w_in_hgrn, hgrn_lb_raw, hgrn_out_norm,
              w_in_swa, swa_q_norm, swa_k_norm, swa_sinks, w_mem_kv, mem_q_norm, mem_k_norm,
              w_out, w_ffn_in, w_ffn_out):
    sm = jax.nn.softmax(hgrn_lb_raw.astype(F32), axis=0)
    lbs = jnp.clip(jnp.cumsum(sm, axis=0) - sm[0], 0.0, LB_MAX)

    def layer(x, i, mk, mv, hg_s0, buf_k, buf_v):
        dt = x.dtype
        B, L, _ = x.shape
        j = i // 2
        h = rmsnorm(x, norm_mix[i])
        if i % 2 == 0:
            p = h @ w_in_hgrn[j]
            tok, st = hgrn_branch(p[..., :4 * TOK_WIDTH], lbs[j], hgrn_out_norm[j], hg_s0)
            new_a, new_b = st.astype(dt), None
            cq = p[..., 4 * TOK_WIDTH:]
        else:
            p = h @ w_in_swa[j]
            nq = SW_Q_HEADS * SW_HEAD_DIM
            nk = SW_KV_HEADS * SW_HEAD_DIM
            q = head_rms(p[..., :nq].reshape(B, L, SW_Q_HEADS, SW_HEAD_DIM), swa_q_norm[j])
            k = head_rms(p[..., nq:nq + nk].reshape(B, L, SW_KV_HEADS, SW_HEAD_DIM), swa_k_norm[j])
            v = p[..., nq + nk:nq + 2 * nk].reshape(B, L, SW_KV_HEADS, SW_HEAD_DIM).astype(F32)
            cq = p[..., nq + 2 * nk:]
            if buf_k is None:
                tok = swa_prompt(q, k, v, swa_sinks[j])
                kb, vb = k[:, -WINDOW:], v[:, -WINDOW:]
            else:
                tok, kb, vb = swa_sample(q, k, v, buf_k, buf_v, swa_sinks[j])
            new_a, new_b = kb.astype(dt), vb.astype(dt)
        mo = mem_attend(cq, mk, mv, mem_q_norm[i])
        x = x + jnp.concatenate([tok.astype(dt), mo.astype(dt)], axis=-1) @ w_out[i]
        gu = rmsnorm(x, norm_ffn[i]) @ w_ffn_in[i]
        g, u = jnp.split(gu, 2, axis=-1)
        x = x + (jax.nn.silu(g) * u) @ w_ffn_out[i]
        return x, new_a, new_b

    dt = x_prompt.dtype
    xp, xs = x_prompt, x_sample
    mk_p, mv_p, hg_p, hg_s, swk_p, swv_p, swk_s, swv_s = [], [], [], [], [], [], [], []
    for i in range(DEPTH):
        j = i // 2
        mkp, mvp = mem_kv(mem_prompt, norm_mem[i], w_mem_kv[i], mem_k_norm[i])
        mk_p.append(mkp.astype(dt))
        mv_p.append(mvp.astype(dt))
        if i % 2 == 0:
            s0 = jnp.zeros((xp.shape[0], HG_HEADS, HG_HEAD_DIM, HG_HEAD_DIM), F32)
            xp, a, _ = layer(xp, i, mkp, mvp, s0, None, None)
            xs, b, _ = layer(xs, i, cache_mem_k[i], cache_mem_v[i], state_hgrn[j], None, None)
            hg_p.append(a)
            hg_s.append(b)
        else:
            xp, kp, vp = layer(xp, i, mkp, mvp, None, None, None)
            xs, ksm, vsm = layer(xs, i, cache_mem_k[i], cache_mem_v[i], None, cache_swa_k[j], cache_swa_v[j])
            swk_p.append(kp)
            swv_p.append(vp)
            swk_s.append(ksm)
            swv_s.append(vsm)

    new_mem_k_prompt = jnp.stack(mk_p)
    new_mem_v_prompt = jnp.stack(mv_p)
    new_hgrn_prompt = jnp.stack(hg_p)
    new_hgrn_sample = jnp.stack(hg_s)
    new_swa_k_prompt = jnp.stack(swk_p)
    new_swa_v_prompt = jnp.stack(swv_p)
    new_swa_k_sample = jnp.stack(swk_s)
    new_swa_v_sample = jnp.stack(swv_s)
    return (xp, xs, new_mem_k_prompt, new_mem_v_prompt, new_hgrn_prompt, new_hgrn_sample,
            new_swa_k_prompt, new_swa_v_prompt, new_swa_k_sample, new_swa_v_sample)
```

```python
import functools

import numpy as np
import jax
import jax.numpy as jnp
from jax import lax
from jax.experimental import pallas as pl
from jax.experimental.pallas import tpu as pltpu

F32 = jnp.float32
BF16 = jnp.bfloat16
EPS = 1e-6
NEG = -1e30
LB_MAX = 0.999

HEAD64 = 64
MEM_HEADS = 4
SW_KV_HEADS = 4
SW_GROUP = 3
WINDOW = 128
HG_HEAD = 128
HG_ROWS = 64
SAMPLE_PAD = 8
LANES = 128

V7X_VMEM_BYTES = 64 * 1024 * 1024
VMEM_LIMIT = V7X_VMEM_BYTES - 8 * 1024 * 1024


def _params(sem):
    return pltpu.CompilerParams(dimension_semantics=sem, vmem_limit_bytes=VMEM_LIMIT)


def _dot(a, b):
    return jnp.dot(a, b, preferred_element_type=F32)


def _dot_nt(a, b):
    return lax.dot_general(a, b, (((1,), (1,)), ((), ())), preferred_element_type=F32)


def _dot_tn(a, b):
    return lax.dot_general(a, b, (((0,), (0,)), ((), ())), preferred_element_type=F32)


def _sigmoid(x):
    return 1.0 / (1.0 + jnp.exp(-x))


def _sigmoid_pair(z):
    e = jnp.exp(-jnp.abs(z))
    r = 1.0 / (1.0 + e)
    er = e * r
    pos = z >= 0
    return jnp.where(pos, r, er), jnp.where(pos, er, r)


def _rms_rows(x, g):
    ms = jnp.mean(x * x, axis=-1, keepdims=True)
    return x * lax.rsqrt(ms + EPS) * g


def _head_rms64(x, g, bd):
    ms = _dot((x * x).astype(BF16), bd)
    return x * lax.rsqrt(ms + EPS) * g


def _head_mask(width, h):
    lane = lax.broadcasted_iota(jnp.int32, (1, width), 1)
    return ((lane >> 6) == h).astype(F32)


def _norm_matmul_kernel(x_ref, g_ref, w_ref, o_ref):
    h = _rms_rows(x_ref[...], g_ref[...]).astype(BF16)
    o_ref[...] = _dot(h, w_ref[...])


def _norm_matmul(x, g, w, tm):
    m, d = x.shape
    n = w.shape[1]
    return pl.pallas_call(
        _norm_matmul_kernel,
        grid=(m // tm,),
        in_specs=[
            pl.BlockSpec((tm, d), lambda i: (i, 0)),
            pl.BlockSpec((1, d), lambda i: (0, 0)),
            pl.BlockSpec((d, n), lambda i: (0, 0)),
        ],
        out_specs=pl.BlockSpec((tm, n), lambda i: (i, 0)),
        out_shape=jax.ShapeDtypeStruct((m, n), F32),
        compiler_params=_params(("parallel",)),
        name="norm_matmul",
    )(x, g, w)


def _mix_ffn_kernel(x_ref, tok_ref, mo_ref, wo_ref, g_ref, w1_ref, w2_ref, o_ref, act_ref,
                    *, ff, chunk):
    tw = tok_ref.shape[1]
    x1 = (x_ref[...] + _dot(tok_ref[...], wo_ref[0:tw, :])
          + _dot(mo_ref[...], wo_ref[tw:, :]))
    o_ref[...] = x1
    h = _rms_rows(x1, g_ref[...]).astype(BF16)
    for c0 in range(0, ff, chunk):
        g = _dot(h, w1_ref[:, c0:c0 + chunk])
        u = _dot(h, w1_ref[:, ff + c0:ff + c0 + chunk])
        act_ref[:, c0:c0 + chunk] = (g * _sigmoid(g) * u).astype(BF16)
    o_ref[...] += _dot(act_ref[...], w2_ref[...])


def _mix_ffn(x, tok, mo, wo, g, w1, w2, tm):
    m, d = x.shape
    ff = w2.shape[0]
    tw, mw = tok.shape[1], mo.shape[1]
    const = lambda i: (0, 0)
    return pl.pallas_call(
        functools.partial(_mix_ffn_kernel, ff=ff, chunk=256),
        grid=(m // tm,),
        in_specs=[
            pl.BlockSpec((tm, d), lambda i: (i, 0)),
            pl.BlockSpec((tm, tw), lambda i: (i, 0)),
            pl.BlockSpec((tm, mw), lambda i: (i, 0)),
            pl.BlockSpec((tw + mw, d), const, pipeline_mode=pl.Buffered(1)),
            pl.BlockSpec((1, d), const),
            pl.BlockSpec((d, 2 * ff), const, pipeline_mode=pl.Buffered(1)),
            pl.BlockSpec((ff, d), const, pipeline_mode=pl.Buffered(1)),
        ],
        out_specs=pl.BlockSpec((tm, d), lambda i: (i, 0)),
        out_shape=jax.ShapeDtypeStruct((m, d), F32),
        scratch_shapes=[pltpu.VMEM((tm, ff), BF16)],
        compiler_params=_params(("parallel",)),
        name="mix_ffn",
    )(x, tok, mo, wo, g, w1, w2)


def _mem_kv_kernel(mem_ref, g_ref, w_ref, kn_ref, bd_ref, k_ref, v_ref):
    h = _rms_rows(mem_ref[...], g_ref[...]).astype(BF16)
    kv = _dot(h, w_ref[...])
    kw = k_ref.shape[-1]
    k_ref[...] = _head_rms64(kv[:, :kw], kn_ref[...], bd_ref[...])
    v_ref[...] = kv[:, kw:]


def _mem_kv(mem, g, w, kn, bd):
    depth = w.shape[0]
    b, nm, d = mem.shape
    kw = w.shape[2] // 2
    out = jax.ShapeDtypeStruct((depth, b, nm, kw), F32)
    return pl.pallas_call(
        _mem_kv_kernel,
        grid=(depth, b),
        in_specs=[
            pl.BlockSpec((None, nm, d), lambda i, j: (j, 0, 0)),
            pl.BlockSpec((None, 1, d), lambda i, j: (i, 0, 0)),
            pl.BlockSpec((None, d, 2 * kw), lambda i, j: (i, 0, 0)),
            pl.BlockSpec((None, 1, kw), lambda i, j: (i, 0, 0)),
            pl.BlockSpec((kw, kw), lambda i, j: (0, 0)),
        ],
        out_specs=[pl.BlockSpec((None, None, nm, kw), lambda i, j: (i, j, 0, 0))] * 2,
        out_shape=[out, out],
        compiler_params=_params(("parallel", "parallel")),
        name="mem_kv",
    )(mem, g, w, kn, bd)


def _mem_attend_kernel(q_ref, k_ref, v_ref, qn_ref, bd_ref, o_ref):
    nb, tq, w = q_ref.shape
    masks = [_head_mask(w, h) for h in range(MEM_HEADS)]
    for b in range(nb):
        qn = _head_rms64(q_ref[b], qn_ref[...], bd_ref[...]) * HEAD64 ** -0.5
        kb = k_ref[b].astype(BF16)
        vb = v_ref[b].astype(BF16)
        qbd = jnp.concatenate([(qn * m).astype(BF16) for m in masks], axis=0)
        s = _dot_nt(qbd, kb)
        p = jnp.exp(s - jnp.max(s, axis=-1, keepdims=True))
        o = _dot(p.astype(BF16), vb) / jnp.sum(p, axis=-1, keepdims=True)
        acc = o[0:tq] * masks[0]
        for h in range(1, MEM_HEADS):
            acc += o[h * tq:(h + 1) * tq] * masks[h]
        o_ref[b] = acc.astype(o_ref.dtype)


def _mem_attend(q, col, mk, mv, qn, bd, nb, tq):
    bsz, l, _ = q.shape
    nm, w = mk.shape[1], mk.shape[2]
    return pl.pallas_call(
        _mem_attend_kernel,
        grid=(bsz // nb, l // tq),
        in_specs=[
            pl.BlockSpec((nb, tq, w), lambda i, t: (i, t, col)),
            pl.BlockSpec((nb, nm, w), lambda i, t: (i, 0, 0)),
            pl.BlockSpec((nb, nm, w), lambda i, t: (i, 0, 0)),
            pl.BlockSpec((1, w), lambda i, t: (0, 0)),
            pl.BlockSpec((w, w), lambda i, t: (0, 0)),
        ],
        out_specs=pl.BlockSpec((nb, tq, w), lambda i, t: (i, t, 0)),
        out_shape=jax.ShapeDtypeStruct((bsz, l, w), BF16),
        compiler_params=_params(("parallel", "parallel")),
        name="mem_attend",
    )(q, mk, mv, qn, bd)


def _sink_softmax_pv(s, sink, vv):
    m = jnp.maximum(jnp.max(s, axis=-1, keepdims=True), sink)
    p = jnp.exp(s - m)
    den = jnp.sum(p, axis=-1, keepdims=True) + jnp.exp(sink - m)
    return _dot(p.astype(BF16), vv) / den


def _swa_prompt_kernel(q_ref, kc_ref, vc_ref, kp_ref, vp_ref, qn_ref, kn_ref, sink_ref, bd_ref,
                       tok_ref, kout_ref, vout_ref):
    n = pl.program_id(1)
    bd = bd_ref[...]
    w = kc_ref.shape[-1]
    kc = _head_rms64(kc_ref[...], kn_ref[...], bd)
    kp = _head_rms64(kp_ref[...], kn_ref[...], bd)
    vc = vc_ref[...]
    kk = jnp.concatenate([kp, kc], axis=0).astype(BF16)
    vv = jnp.concatenate([vp_ref[...], vc], axis=0).astype(BF16)
    qi = lax.broadcasted_iota(jnp.int32, (WINDOW, 2 * WINDOW), 0)
    r = lax.broadcasted_iota(jnp.int32, (WINDOW, 2 * WINDOW), 1)
    valid = (r > qi) & (r <= qi + WINDOW) & ((r >= WINDOW) | (n > 0))
    masks = [_head_mask(w, k) for k in range(SW_KV_HEADS)]
    for g in range(SW_GROUP):
        qg = _head_rms64(q_ref[:, g * w:(g + 1) * w], qn_ref[...], bd) * HEAD64 ** -0.5
        acc = jnp.zeros((WINDOW, w), F32)
        for k in range(SW_KV_HEADS):
            s = _dot_nt((qg * masks[k]).astype(BF16), kk)
            s = jnp.where(valid, s, NEG)
            acc += _sink_softmax_pv(s, sink_ref[g * SW_KV_HEADS + k], vv) * masks[k]
        tok_ref[:, g * w:(g + 1) * w] = acc.astype(tok_ref.dtype)
    kout_ref[...] = kc
    vout_ref[...] = vc


def _swa_prompt(p, qn, kn, sinks, bd):
    b, l, _ = p.shape
    w = SW_KV_HEADS * HEAD64
    nq = SW_GROUP
    prev = lambda i, n: (i, jnp.maximum(n - 1, 0), 0)
    cache = jax.ShapeDtypeStruct((b, WINDOW, w), F32)
    return pl.pallas_call(
        _swa_prompt_kernel,
        grid=(b, l // WINDOW),
        in_specs=[
            pl.BlockSpec((None, WINDOW, nq * w), lambda i, n: (i, n, 0)),
            pl.BlockSpec((None, WINDOW, w), lambda i, n: (i, n, nq)),
            pl.BlockSpec((None, WINDOW, w), lambda i, n: (i, n, nq + 1)),
            pl.BlockSpec((None, WINDOW, w), lambda i, n: (i, jnp.maximum(n - 1, 0), nq)),
            pl.BlockSpec((None, WINDOW, w), lambda i, n: (i, jnp.maximum(n - 1, 0), nq + 1)),
            pl.BlockSpec((1, w), lambda i, n: (0, 0)),
            pl.BlockSpec((1, w), lambda i, n: (0, 0)),
            pl.BlockSpec(memory_space=pltpu.SMEM),
            pl.BlockSpec((w, w), lambda i, n: (0, 0)),
        ],
        out_specs=[
            pl.BlockSpec((None, WINDOW, nq * w), lambda i, n: (i, n, 0)),
            pl.BlockSpec((None, WINDOW, w), lambda i, n: (i, 0, 0)),
            pl.BlockSpec((None, WINDOW, w), lambda i, n: (i, 0, 0)),
        ],
        out_shape=[jax.ShapeDtypeStruct((b, l, nq * w), BF16), cache, cache],
        compiler_params=_params(("parallel", "arbitrary")),
        name="swa_prompt",
    )(p, p, p, p, p, qn, kn, sinks, bd)


def _swa_sample_kernel(q_ref, kn_ref_in, vn_ref_in, ck_ref, cv_ref, qn_ref, kn_ref, sink_ref,
                       bd_ref, tok_ref, kout_ref, vout_ref, *, nl):
    nb = q_ref.shape[0]
    w = ck_ref.shape[-1]
    rows = SAMPLE_PAD
    nkeys = 2 * WINDOW
    bd = bd_ref[...]
    nh = SW_GROUP * SW_KV_HEADS
    masks = [_head_mask(w, k) for k in range(SW_KV_HEADS)]
    row = lax.broadcasted_iota(jnp.int32, (nh * rows, nkeys), 0)
    r = lax.broadcasted_iota(jnp.int32, (nh * rows, nkeys), 1)
    j = row & (rows - 1)
    valid = (r > j) & (r <= j + WINDOW)
    hrow = lax.broadcasted_iota(jnp.int32, (nh * rows, 1), 0) >> 3
    sink = jnp.zeros((nh * rows, 1), F32)
    for h in range(nh):
        sink = jnp.where(hrow == h, sink_ref[h], sink)
    zpad = jnp.zeros((nkeys - WINDOW - rows, w), F32)
    for b in range(nb):
        knew = _head_rms64(kn_ref_in[b], kn_ref[...], bd)
        vnew = vn_ref_in[b]
        kk = jnp.concatenate([ck_ref[b], knew, zpad], axis=0).astype(BF16)
        vv = jnp.concatenate([cv_ref[b], vnew, zpad], axis=0).astype(BF16)
        pieces = []
        for g in range(SW_GROUP):
            qg = _head_rms64(q_ref[b, :, g * w:(g + 1) * w], qn_ref[...], bd) * HEAD64 ** -0.5
            pieces += [(qg * masks[k]).astype(BF16) for k in range(SW_KV_HEADS)]
        s = _dot_nt(jnp.concatenate(pieces, axis=0), kk)
        s = jnp.where(valid, s, NEG)
        o = _sink_softmax_pv(s, sink, vv)
        for g in range(SW_GROUP):
            acc = jnp.zeros((rows, w), F32)
            for k in range(SW_KV_HEADS):
                h = g * SW_KV_HEADS + k
                acc += o[h * rows:(h + 1) * rows] * masks[k]
            tok_ref[b, :, g * w:(g + 1) * w] = acc.astype(tok_ref.dtype)
        kout_ref[b, 0:WINDOW - nl, :] = ck_ref[b, nl:WINDOW, :]
        kout_ref[b, WINDOW - nl:WINDOW, :] = knew[0:nl]
        vout_ref[b, 0:WINDOW - nl, :] = cv_ref[b, nl:WINDOW, :]
        vout_ref[b, WINDOW - nl:WINDOW, :] = vnew[0:nl]


def _swa_sample(p, ck, cv, qn, kn, sinks, bd, nl, nb):
    b = p.shape[0]
    w = ck.shape[-1]
    nq = SW_GROUP
    cache = jax.ShapeDtypeStruct((b, WINDOW, w), F32)
    c3 = lambda i: (i, 0, 0)
    return pl.pallas_call(
        functools.partial(_swa_sample_kernel, nl=nl),
        grid=(b // nb,),
        in_specs=[
            pl.BlockSpec((nb, SAMPLE_PAD, nq * w), c3),
            pl.BlockSpec((nb, SAMPLE_PAD, w), lambda i: (i, 0, nq)),
            pl.BlockSpec((nb, SAMPLE_PAD, w), lambda i: (i, 0, nq + 1)),
            pl.BlockSpec((nb, WINDOW, w), c3),
            pl.BlockSpec((nb, WINDOW, w), c3),
            pl.BlockSpec((1, w), lambda i: (0, 0)),
            pl.BlockSpec((1, w), lambda i: (0, 0)),
            pl.BlockSpec(memory_space=pltpu.SMEM),
            pl.BlockSpec((w, w), lambda i: (0, 0)),
        ],
        out_specs=[
            pl.BlockSpec((nb, SAMPLE_PAD, nq * w), c3),
            pl.BlockSpec((nb, WINDOW, w), c3),
            pl.BlockSpec((nb, WINDOW, w), c3),
        ],
        out_shape=[jax.ShapeDtypeStruct((b, SAMPLE_PAD, nq * w), BF16), cache, cache],
        compiler_params=_params(("parallel",)),
        name="swa_sample",
    )(p, p, p, ck, cv, qn, kn, sinks, bd)


def _hgrn_consts(rows, group):
    t = np.arange(rows)
    loc, run = t % group, t // group
    same = run[:, None] == run[None, :]
    r = t[None, :]
    mats = [same & (r <= t[:, None]), same & (r > t[:, None])]
    qs, ks, masks = [], [], []
    h = group // 2
    while h >= 1:
        par = loc // (2 * h)
        right = (loc % (2 * h)) >= h
        bnd = run * group + par * 2 * h + h - 1
        qs.append(right[:, None] & (r > bnd[:, None]) & (r <= t[:, None]))
        ks.append((~right)[:, None] & (r > t[:, None]) & (r <= bnd[:, None]))
        masks.append(same & (par[:, None] == par[None, :]) & right[:, None] & (~right)[None, :])
        h //= 2
    masks.append(np.eye(rows, dtype=bool))
    w = np.concatenate(mats + qs + ks, axis=0).astype(np.float32)
    w2 = np.concatenate([w, w], axis=1)
    return jnp.asarray(w2, BF16), jnp.asarray(np.stack(masks).astype(np.float32)), len(qs)


def _hgrn_gates(q_raw, z, lb):
    sp, sn = _sigmoid_pair(z)
    logf = jnp.log(lb + (1.0 - lb) * sp)
    kf = (1.0 - lb) * sn
    q = q_raw * _sigmoid(q_raw) * HG_HEAD ** -0.5
    return q, kf, logf


def _hgrn_intra(q, kf, v, logf, w2_ref, m_ref, nlev):
    rows = q.shape[0]
    hi = logf.astype(BF16)
    lo = (logf - hi.astype(F32)).astype(BF16)
    e = jnp.exp(_dot(w2_ref[...], jnp.concatenate([hi, lo], axis=0)))
    att = m_ref[nlev] * _dot_nt(q.astype(BF16), kf.astype(BF16))
    for l in range(nlev):
        ql = (q * e[(2 + l) * rows:(3 + l) * rows]).astype(BF16)
        kl = (kf * e[(2 + nlev + l) * rows:(3 + nlev + l) * rows]).astype(BF16)
        att += m_ref[l] * _dot_nt(ql, kl)
    o_intra = _dot(att.astype(BF16), v.astype(BF16))
    return o_intra, q * e[0:rows], kf * e[rows:2 * rows], e[0:rows]


def _hgrn_out(o, g, gn):
    ms = jnp.mean(o * o, axis=-1, keepdims=True)
    return (o * lax.rsqrt(ms + EPS) * gn * (g * _sigmoid(g))).astype(BF16)


def _hgrn_prompt_kernel(q_ref, f_ref, i_ref, g_ref, lb_ref, gn_ref, w2_ref, m_ref,
                        tok_ref, st_ref, s_scr, *, nlev):
    t = pl.program_id(2)

    @pl.when(t == 0)
    def _():
        s_scr[...] = jnp.zeros_like(s_scr)

    lb = lb_ref[...]
    gn = gn_ref[...]
    rows = HG_ROWS

    def chunk(c, st):
        sl = pl.ds(pl.multiple_of(c * rows, rows), rows)
        v = i_ref[sl, :]
        q, kf, logf = _hgrn_gates(q_ref[sl, :], f_ref[sl, :], lb)
        o_intra, qt, kh, e_cum = _hgrn_intra(q, kf, v, logf, w2_ref, m_ref, nlev)
        o = o_intra + _dot_nt(qt.astype(BF16), st.astype(BF16))
        tok_ref[sl, :] = _hgrn_out(o, g_ref[sl, :], gn)
        return st * e_cum[rows - 1:rows, :] + _dot_tn(v.astype(BF16), kh.astype(BF16))

    st = lax.fori_loop(0, q_ref.shape[0] // rows, chunk, s_scr[...])
    s_scr[...] = st

    @pl.when(t == pl.num_programs(2) - 1)
    def _():
        st_ref[...] = st.T


def _hgrn_prompt(p, lb, gn, tb):
    b, l, _ = p.shape
    nh = lb.shape[1] // HG_HEAD
    w2, masks, nlev = _hgrn_consts(HG_ROWS, HG_ROWS)
    sec = lambda s: pl.BlockSpec((None, tb, HG_HEAD), lambda i, h, t: (i, t, s * nh + h))
    return pl.pallas_call(
        functools.partial(_hgrn_prompt_kernel, nlev=nlev),
        grid=(b, nh, l // tb),
        in_specs=[
            sec(0), sec(1), sec(2), sec(3),
            pl.BlockSpec((1, HG_HEAD), lambda i, h, t: (0, h)),
            pl.BlockSpec((1, HG_HEAD), lambda i, h, t: (0, h)),
            pl.BlockSpec(w2.shape, lambda i, h, t: (0, 0)),
            pl.BlockSpec(masks.shape, lambda i, h, t: (0, 0, 0)),
        ],
        out_specs=[
            pl.BlockSpec((None, tb, HG_HEAD), lambda i, h, t: (i, t, h)),
            pl.BlockSpec((None, None, HG_HEAD, HG_HEAD), lambda i, h, t: (i, h, 0, 0)),
        ],
        out_shape=[jax.ShapeDtypeStruct((b, l, nh * HG_HEAD), BF16),
                   jax.ShapeDtypeStruct((b, nh, HG_HEAD, HG_HEAD), F32)],
        scratch_shapes=[pltpu.VMEM((HG_HEAD, HG_HEAD), F32)],
        compiler_params=_params(("parallel", "parallel", "arbitrary")),
        name="hgrn_prompt",
    )(p, p, p, p, lb, gn, w2, masks)


def _hgrn_sample_kernel(q_ref, f_ref, i_ref, g_ref, s0_ref, lb_ref, gn_ref, w2_ref, m_ref,
                        tok_ref, st_ref, *, nlev, nl):
    rows = q_ref.shape[0]
    nb = rows // nl
    v = i_ref[...]
    q, kf, logf = _hgrn_gates(q_ref[...], f_ref[...], lb_ref[...])
    o_intra, qt, kh, e_cum = _hgrn_intra(q, kf, v, logf, w2_ref, m_ref, nlev)
    shift = nl.bit_length() - 1
    rowb = lax.broadcasted_iota(jnp.int32, (rows, 1), 0) >> shift
    s0cat = jnp.concatenate([s0_ref[b].astype(BF16) for b in range(nb)], axis=1)
    full = _dot(qt.astype(BF16), s0cat)
    o = o_intra
    for b in range(nb):
        o += jnp.where(rowb == b, full[:, b * HG_HEAD:(b + 1) * HG_HEAD], 0.0)
    tok_ref[...] = _hgrn_out(o, g_ref[...], gn_ref[...])
    xt = jnp.concatenate([kh, e_cum], axis=0).T
    colb = lax.broadcasted_iota(jnp.int32, (rows, nb * HG_HEAD), 1) >> 7
    vbd = jnp.where(rowb == colb, jnp.concatenate([v] * nb, axis=1), 0.0).astype(BF16)
    upd = _dot(xt[:, 0:rows].astype(BF16), vbd)
    for b in range(nb):
        last = rows + b * nl + nl - 1
        st_ref[b] = s0_ref[b] * xt[:, last:last + 1] + upd[:, b * HG_HEAD:(b + 1) * HG_HEAD]


def _hgrn_sample(p, s0, lb, gn, nl):
    rows = HG_ROWS
    nb = rows // nl
    b, nh = s0.shape[0], s0.shape[1]
    w2, masks, nlev = _hgrn_consts(rows, nl)
    sec = lambda s: pl.BlockSpec((rows, HG_HEAD), lambda i, h: (i, s * nh + h))
    st_spec = pl.BlockSpec((nb, None, HG_HEAD, HG_HEAD), lambda i, h: (i, h, 0, 0))
    return pl.pallas_call(
        functools.partial(_hgrn_sample_kernel, nlev=nlev, nl=nl),
        grid=(b // nb, nh),
        in_specs=[
            sec(0), sec(1), sec(2), sec(3), st_spec,
            pl.BlockSpec((1, HG_HEAD), lambda i, h: (0, h)),
            pl.BlockSpec((1, HG_HEAD), lambda i, h: (0, h)),
            pl.BlockSpec(w2.shape, lambda i, h: (0, 0)),
            pl.BlockSpec(masks.shape, lambda i, h: (0, 0, 0)),
        ],
        out_specs=[pl.BlockSpec((rows, HG_HEAD), lambda i, h: (i, h)), st_spec],
        out_shape=[jax.ShapeDtypeStruct((b * nl, nh * HG_HEAD), BF16),
                   jax.ShapeDtypeStruct(s0.shape, F32)],
        compiler_params=_params(("parallel", "parallel")),
        name="hgrn_sample",
    )(p, p, p, p, s0, lb, gn, w2, masks)


def _swa_head_perm():
    idx = [(k * SW_GROUP + g) * HEAD64 + d
           for g in range(SW_GROUP) for k in range(SW_KV_HEADS) for d in range(HEAD64)]
    return np.asarray(idx, np.int32)


def kernel(x_prompt, x_sample, cache_mem_k, cache_mem_v, state_hgrn, cache_swa_k, cache_swa_v,
           mem_prompt, norm_mix, norm_ffn, norm_mem, w_in_hgrn, hgrn_lb_raw, hgrn_out_norm,
           w_in_swa, swa_q_norm, swa_k_norm, swa_sinks, w_mem_kv, mem_q_norm, mem_k_norm,
           w_out, w_ffn_in, w_ffn_out):
    depth, d = norm_mix.shape
    bp, lp, _ = x_prompt.shape
    bs, ls, _ = x_sample.shape
    nm = mem_prompt.shape[1]
    mw = MEM_HEADS * HEAD64
    kvw = SW_KV_HEADS * HEAD64
    tokw = d - mw
    tm = 512

    seg = np.arange(mw) // HEAD64
    bd = jnp.asarray((seg[:, None] == seg[None, :]).astype(np.float32) / HEAD64, BF16)
    tile4 = lambda a: jnp.tile(a.astype(F32), (1, mw // HEAD64))[:, None, :]

    sm = jax.nn.softmax(hgrn_lb_raw.astype(F32), axis=0)
    lbs = jnp.clip(jnp.cumsum(sm, axis=0) - sm[0], 0.0, LB_MAX)

    perm = _swa_head_perm()
    w_in_swa_p = jnp.concatenate([w_in_swa[:, :, perm], w_in_swa[:, :, tokw:]], axis=2).astype(BF16)
    w_in_hgrn_b = w_in_hgrn.astype(BF16)
    w_out_b = w_out.astype(BF16)
    w1_b = w_ffn_in.astype(BF16)
    w2_b = w_ffn_out.astype(BF16)
    sinks_p = swa_sinks.astype(F32).reshape(-1, SW_KV_HEADS, SW_GROUP).transpose(0, 2, 1)
    sinks_p = sinks_p.reshape(-1, SW_KV_HEADS * SW_GROUP)
    mem_qn = tile4(mem_q_norm)
    swa_qn = tile4(swa_q_norm)
    swa_kn = tile4(swa_k_norm)

    mk_p, mv_p = _mem_kv(mem_prompt, norm_mem[:, None, :], w_mem_kv.astype(BF16),
                         tile4(mem_k_norm), bd)

    xp = x_prompt.reshape(bp * lp, d)
    xs = x_sample.reshape(bs * ls, d)
    pad_rows = lambda a: jnp.pad(a.reshape(bs, ls, a.shape[-1]),
                                 ((0, 0), (0, SAMPLE_PAD - ls), (0, 0)))
    hg_p, hg_s, swk_p, swv_p, swk_s, swv_s = [], [], [], [], [], []
    for i in range(depth):
        j = i // 2
        g_mix = norm_mix[i][None, :]
        g_ffn = norm_ffn[i][None, :]
        cmk = cache_mem_k[i].reshape(bs, nm, mw)
        cmv = cache_mem_v[i].reshape(bs, nm, mw)
        if i % 2 == 0:
            w_in = w_in_hgrn_b[j]
            wo = w_out_b[i]
            lb = lbs[j][None, :]
            gn = hgrn_out_norm[j][None, :].astype(F32)
            pp = _norm_matmul(xp, g_mix, w_in, tm).reshape(bp, lp, -1)
            tok_p, st_p = _hgrn_prompt(pp, lb, gn, 512)
            ps = _norm_matmul(xs, g_mix, w_in, tm)
            tok_s, st_s = _hgrn_sample(ps, state_hgrn[j], lb, gn, ls)
            hg_p.append(st_p)
            hg_s.append(st_s)
            cq_s = pad_rows(ps[:, 4 * tokw:])
            cq_col = 4 * tokw // mw
        else:
            w_in = w_in_swa_p[j]
            wo = jnp.concatenate([w_out_b[i][perm], w_out_b[i][tokw:]], axis=0)
            pp = _norm_matmul(xp, g_mix, w_in, tm).reshape(bp, lp, -1)
            tok_p, kb, vb = _swa_prompt(pp, swa_qn[j], swa_kn[j], sinks_p[j], bd)
            swk_p.append(kb.reshape(bp, WINDOW, SW_KV_HEADS, HEAD64))
            swv_p.append(vb.reshape(bp, WINDOW, SW_KV_HEADS, HEAD64))
            ps = pad_rows(_norm_matmul(xs, g_mix, w_in, tm))
            tok_s, ks, vs = _swa_sample(ps, cache_swa_k[j].reshape(bs, WINDOW, kvw),
                                        cache_swa_v[j].reshape(bs, WINDOW, kvw),
                                        swa_qn[j], swa_kn[j], sinks_p[j], bd, ls, 8)
            swk_s.append(ks.reshape(bs, WINDOW, SW_KV_HEADS, HEAD64))
            swv_s.append(vs.reshape(bs, WINDOW, SW_KV_HEADS, HEAD64))
            tok_s = tok_s[:, :ls].reshape(bs * ls, tokw)
            cq_s = ps[:, :, tokw + 2 * kvw:]
            cq_col = (tokw + 2 * kvw) // mw
        mo_p = _mem_attend(pp, cq_col, mk_p[i], mv_p[i], mem_qn[i], bd, 1, 512)
        mo_s = _mem_attend(cq_s, 0, cmk, cmv, mem_qn[i], bd, 8, SAMPLE_PAD)
        mo_s = mo_s[:, :ls].reshape(bs * ls, mw)
        xp = _mix_ffn(xp, tok_p.reshape(bp * lp, tokw), mo_p.reshape(bp * lp, mw),
                      wo, g_ffn, w1_b[i], w2_b[i], tm)
        xs = _mix_ffn(xs, tok_s, mo_s, wo, g_ffn, w1_b[i], w2_b[i], tm)

    mshape = (depth, bp, nm, MEM_HEADS, HEAD64)
    return (xp.reshape(bp, lp, d), xs.reshape(bs, ls, d),
            mk_p.reshape(mshape), mv_p.reshape(mshape),
            jnp.stack(hg_p), jnp.stack(hg_s),
            jnp.stack(swk_p), jnp.stack(swv_p), jnp.stack(swk_s), jnp.stack(swv_s))
```

```python
import functools

import numpy as np
import jax
import jax.numpy as jnp
from jax import lax
from jax.experimental import pallas as pl
from jax.experimental.pallas import tpu as pltpu

F32 = jnp.float32
BF16 = jnp.bfloat16
EPS = 1e-6
NEG = -1e30
LB_MAX = 0.999

HEAD64 = 64
MEM_HEADS = 4
SW_KV_HEADS = 4
SW_GROUP = 3
WINDOW = 128
HG_HEAD = 128
HG_ROWS = 64
SAMPLE_PAD = 8
LANES = 128

V7X_VMEM_BYTES = 64 * 1024 * 1024
VMEM_LIMIT = V7X_VMEM_BYTES - 8 * 1024 * 1024


def _params(sem):
    return pltpu.CompilerParams(dimension_semantics=sem, vmem_limit_bytes=VMEM_LIMIT)


def _dot(a, b):
    return jnp.dot(a, b, preferred_element_type=F32)


def _dot_nt(a, b):
    return lax.dot_general(a, b, (((1,), (1,)), ((), ())), preferred_element_type=F32)


def _dot_tn(a, b):
    return lax.dot_general(a, b, (((0,), (0,)), ((), ())), preferred_element_type=F32)


def _sigmoid(x):
    return 1.0 / (1.0 + jnp.exp(-x))


def _sigmoid_pair(z):
    e = jnp.exp(-jnp.abs(z))
    r = 1.0 / (1.0 + e)
    er = e * r
    pos = z >= 0
    return jnp.where(pos, r, er), jnp.where(pos, er, r)


def _rms_rows(x, g):
    ms = jnp.mean(x * x, axis=-1, keepdims=True)
    return x * lax.rsqrt(ms + EPS) * g


def _head_rms64(x, g, bd):
    ms = _dot((x * x).astype(BF16), bd)
    return x * lax.rsqrt(ms + EPS) * g


def _head_mask(width, h):
    lane = lax.broadcasted_iota(jnp.int32, (1, width), 1)
    return ((lane >> 6) == h).astype(F32)


def _norm_matmul_kernel(x_ref, g_ref, w_ref, o_ref):
    h = _rms_rows(x_ref[...], g_ref[...]).astype(BF16)
    o_ref[...] = _dot(h, w_ref[...])


def _norm_matmul(x, g, w, tm):
    m, d = x.shape
    n = w.shape[1]
    return pl.pallas_call(
        _norm_matmul_kernel,
        grid=(m // tm,),
        in_specs=[
            pl.BlockSpec((tm, d), lambda i: (i, 0)),
            pl.BlockSpec((1, d), lambda i: (0, 0)),
            pl.BlockSpec((d, n), lambda i: (0, 0)),
        ],
        out_specs=pl.BlockSpec((tm, n), lambda i: (i, 0)),
        out_shape=jax.ShapeDtypeStruct((m, n), F32),
        compiler_params=_params(("parallel",)),
        name="norm_matmul",
    )(x, g, w)


def _mix_ffn_kernel(x_ref, tok_ref, mo_ref, wo_ref, g_ref, w1_ref, w2_ref, o_ref, act_ref,
                    *, ff, chunk):
    tw = tok_ref.shape[1]
    x1 = (x_ref[...] + _dot(tok_ref[...], wo_ref[0:tw, :])
          + _dot(mo_ref[...], wo_ref[tw:, :]))
    o_ref[...] = x1
    h = _rms_rows(x1, g_ref[...]).astype(BF16)
    for c0 in range(0, ff, chunk):
        g = _dot(h, w1_ref[:, c0:c0 + chunk])
        u = _dot(h, w1_ref[:, ff + c0:ff + c0 + chunk])
        act_ref[:, c0:c0 + chunk] = (g * _sigmoid(g) * u).astype(BF16)
    o_ref[...] += _dot(act_ref[...], w2_ref[...])


def _mix_ffn(x, tok, mo, wo, g, w1, w2, tm):
    m, d = x.shape
    ff = w2.shape[0]
    tw, mw = tok.shape[1], mo.shape[1]
    const = lambda i: (0, 0)
    return pl.pallas_call(
        functools.partial(_mix_ffn_kernel, ff=ff, chunk=256),
        grid=(m // tm,),
        in_specs=[
            pl.BlockSpec((tm, d), lambda i: (i, 0)),
            pl.BlockSpec((tm, tw), lambda i: (i, 0)),
            pl.BlockSpec((tm, mw), lambda i: (i, 0)),
            pl.BlockSpec((tw + mw, d), const, pipeline_mode=pl.Buffered(1)),
            pl.BlockSpec((1, d), const),
            pl.BlockSpec((d, 2 * ff), const, pipeline_mode=pl.Buffered(1)),
            pl.BlockSpec((ff, d), const, pipeline_mode=pl.Buffered(1)),
        ],
        out_specs=pl.BlockSpec((tm, d), lambda i: (i, 0)),
        out_shape=jax.ShapeDtypeStruct((m, d), F32),
        scratch_shapes=[pltpu.VMEM((tm, ff), BF16)],
        compiler_params=_params(("parallel",)),
        name="mix_ffn",
    )(x, tok, mo, wo, g, w1, w2)


def _mem_kv_kernel(mem_ref, g_ref, w_ref, kn_ref, bd_ref, k_ref, v_ref):
    h = _rms_rows(mem_ref[...], g_ref[...]).astype(BF16)
    kv = _dot(h, w_ref[...])
    kw = k_ref.shape[-1]
    k_ref[...] = _head_rms64(kv[:, :kw], kn_ref[...], bd_ref[...])
    v_ref[...] = kv[:, kw:]


def _mem_kv(mem, g, w, kn, bd):
    depth = w.shape[0]
    b, nm, d = mem.shape
    kw = w.shape[2] // 2
    out = jax.ShapeDtypeStruct((depth, b, nm, kw), F32)
    return pl.pallas_call(
        _mem_kv_kernel,
        grid=(depth, b),
        in_specs=[
            pl.BlockSpec((None, nm, d), lambda i, j: (j, 0, 0)),
            pl.BlockSpec((None, 1, d), lambda i, j: (i, 0, 0)),
            pl.BlockSpec((None, d, 2 * kw), lambda i, j: (i, 0, 0)),
            pl.BlockSpec((None, 1, kw), lambda i, j: (i, 0, 0)),
            pl.BlockSpec((kw, kw), lambda i, j: (0, 0)),
        ],
        out_specs=[pl.BlockSpec((None, None, nm, kw), lambda i, j: (i, j, 0, 0))] * 2,
        out_shape=[out, out],
        compiler_params=_params(("parallel", "parallel")),
        name="mem_kv",
    )(mem, g, w, kn, bd)


def _mem_attend_kernel(q_ref, k_ref, v_ref, qn_ref, bd_ref, o_ref):
    nb, tq, w = q_ref.shape
    masks = [_head_mask(w, h) for h in range(MEM_HEADS)]
    for b in range(nb):
        qn = _head_rms64(q_ref[b], qn_ref[...], bd_ref[...]) * HEAD64 ** -0.5
        kb = k_ref[b].astype(BF16)
        vb = v_ref[b].astype(BF16)
        qbd = jnp.concatenate([(qn * m).astype(BF16) for m in masks], axis=0)
        s = _dot_nt(qbd, kb)
        p = jnp.exp(s - jnp.max(s, axis=-1, keepdims=True))
        o = _dot(p.astype(BF16), vb) / jnp.sum(p, axis=-1, keepdims=True)
        acc = o[0:tq] * masks[0]
        for h in range(1, MEM_HEADS):
            acc += o[h * tq:(h + 1) * tq] * masks[h]
        o_ref[b] = acc.astype(o_ref.dtype)


def _mem_attend(q, col, mk, mv, qn, bd, nb, tq):
    bsz, l, _ = q.shape
    nm, w = mk.shape[1], mk.shape[2]
    return pl.pallas_call(
        _mem_attend_kernel,
        grid=(bsz // nb, l // tq),
        in_specs=[
            pl.BlockSpec((nb, tq, w), lambda i, t: (i, t, col)),
            pl.BlockSpec((nb, nm, w), lambda i, t: (i, 0, 0)),
            pl.BlockSpec((nb, nm, w), lambda i, t: (i, 0, 0)),
            pl.BlockSpec((1, w), lambda i, t: (0, 0)),
            pl.BlockSpec((w, w), lambda i, t: (0, 0)),
        ],
        out_specs=pl.BlockSpec((nb, tq, w), lambda i, t: (i, t, 0)),
        out_shape=jax.ShapeDtypeStruct((bsz, l, w), BF16),
        compiler_params=_params(("parallel", "parallel")),
        name="mem_attend",
    )(q, mk, mv, qn, bd)


def _sink_softmax_pv(s, sink, vv):
    m = jnp.maximum(jnp.max(s, axis=-1, keepdims=True), sink)
    p = jnp.exp(s - m)
    den = jnp.sum(p, axis=-1, keepdims=True) + jnp.exp(sink - m)
    return _dot(p.astype(BF16), vv) / den


def _swa_prompt_kernel(q_ref, kc_ref, vc_ref, kp_ref, vp_ref, qn_ref, kn_ref, sink_ref, bd_ref,
                       tok_ref, kout_ref, vout_ref):
    n = pl.program_id(1)
    bd = bd_ref[...]
    w = kc_ref.shape[-1]
    kc = _head_rms64(kc_ref[...], kn_ref[...], bd)
    kp = _head_rms64(kp_ref[...], kn_ref[...], bd)
    vc = vc_ref[...]
    kk = jnp.concatenate([kp, kc], axis=0).astype(BF16)
    vvt = jnp.concatenate([vp_ref[...], vc], axis=0).T.astype(BF16)
    r = lax.broadcasted_iota(jnp.int32, (2 * WINDOW, WINDOW), 0)
    qi = lax.broadcasted_iota(jnp.int32, (2 * WINDOW, WINDOW), 1)
    valid = (r > qi) & (r <= qi + WINDOW) & ((r >= WINDOW) | (n > 0))
    masks = [_head_mask(w, k) for k in range(SW_KV_HEADS)]
    pieces = []
    for g in range(SW_GROUP):
        qg = _head_rms64(q_ref[:, g * w:(g + 1) * w], qn_ref[...], bd) * HEAD64 ** -0.5
        pieces += [(qg * m).astype(BF16) for m in masks]
    st = _dot_nt(kk, jnp.concatenate(pieces, axis=0))
    outs = []
    for h in range(SW_GROUP * SW_KV_HEADS):
        k = h % SW_KV_HEADS
        s = jnp.where(valid, st[:, h * WINDOW:(h + 1) * WINDOW], NEG)
        sink = sink_ref[h]
        m = jnp.maximum(jnp.max(s, axis=0, keepdims=True), sink)
        p = jnp.exp(s - m)
        den = jnp.sum(p, axis=0, keepdims=True) + jnp.exp(sink - m)
        outs.append(_dot(vvt[k * HEAD64:(k + 1) * HEAD64, :], p.astype(BF16)) / den)
    tok_ref[...] = jnp.concatenate(outs, axis=0).T.astype(tok_ref.dtype)
    kout_ref[...] = kc
    vout_ref[...] = vc


def _swa_prompt(p, qn, kn, sinks, bd):
    b, l, _ = p.shape
    w = SW_KV_HEADS * HEAD64
    nq = SW_GROUP
    prev = lambda i, n: (i, jnp.maximum(n - 1, 0), 0)
    cache = jax.ShapeDtypeStruct((b, WINDOW, w), F32)
    return pl.pallas_call(
        _swa_prompt_kernel,
        grid=(b, l // WINDOW),
        in_specs=[
            pl.BlockSpec((None, WINDOW, nq * w), lambda i, n: (i, n, 0)),
            pl.BlockSpec((None, WINDOW, w), lambda i, n: (i, n, nq)),
            pl.BlockSpec((None, WINDOW, w), lambda i, n: (i, n, nq + 1)),
            pl.BlockSpec((None, WINDOW, w), lambda i, n: (i, jnp.maximum(n - 1, 0), nq)),
            pl.BlockSpec((None, WINDOW, w), lambda i, n: (i, jnp.maximum(n - 1, 0), nq + 1)),
            pl.BlockSpec((1, w), lambda i, n: (0, 0)),
            pl.BlockSpec((1, w), lambda i, n: (0, 0)),
            pl.BlockSpec(memory_space=pltpu.SMEM),
            pl.BlockSpec((w, w), lambda i, n: (0, 0)),
        ],
        out_specs=[
            pl.BlockSpec((None, WINDOW, nq * w), lambda i, n: (i, n, 0)),
            pl.BlockSpec((None, WINDOW, w), lambda i, n: (i, 0, 0)),
            pl.BlockSpec((None, WINDOW, w), lambda i, n: (i, 0, 0)),
        ],
        out_shape=[jax.ShapeDtypeStruct((b, l, nq * w), BF16), cache, cache],
        compiler_params=_params(("parallel", "arbitrary")),
        name="swa_prompt",
    )(p, p, p, p, p, qn, kn, sinks, bd)


def _swa_sample_kernel(q_ref, kn_ref_in, vn_ref_in, ck_ref, cv_ref, qn_ref, kn_ref, sink_ref,
                       bd_ref, tok_ref, kout_ref, vout_ref, *, nl):
    nb = q_ref.shape[0]
    w = ck_ref.shape[-1]
    rows = SAMPLE_PAD
    nkeys = 2 * WINDOW
    bd = bd_ref[...]
    nh = SW_GROUP * SW_KV_HEADS
    masks = [_head_mask(w, k) for k in range(SW_KV_HEADS)]
    row = lax.broadcasted_iota(jnp.int32, (nh * rows, nkeys), 0)
    r = lax.broadcasted_iota(jnp.int32, (nh * rows, nkeys), 1)
    j = row & (rows - 1)
    valid = (r > j) & (r <= j + WINDOW)
    hrow = lax.broadcasted_iota(jnp.int32, (nh * rows, 1), 0) >> 3
    sink = jnp.zeros((nh * rows, 1), F32)
    for h in range(nh):
        sink = jnp.where(hrow == h, sink_ref[h], sink)
    zpad = jnp.zeros((nkeys - WINDOW - rows, w), F32)
    for b in range(nb):
        knew = _head_rms64(kn_ref_in[b], kn_ref[...], bd)
        vnew = vn_ref_in[b]
        kk = jnp.concatenate([ck_ref[b], knew, zpad], axis=0).astype(BF16)
        vv = jnp.concatenate([cv_ref[b], vnew, zpad], axis=0).astype(BF16)
        pieces = []
        for g in range(SW_GROUP):
            qg = _head_rms64(q_ref[b, :, g * w:(g + 1) * w], qn_ref[...], bd) * HEAD64 ** -0.5
            pieces += [(qg * masks[k]).astype(BF16) for k in range(SW_KV_HEADS)]
        s = _dot_nt(jnp.concatenate(pieces, axis=0), kk)
        s = jnp.where(valid, s, NEG)
        o = _sink_softmax_pv(s, sink, vv)
        for g in range(SW_GROUP):
            acc = jnp.zeros((rows, w), F32)
            for k in range(SW_KV_HEADS):
                h = g * SW_KV_HEADS + k
                acc += o[h * rows:(h + 1) * rows] * masks[k]
            tok_ref[b, :, g * w:(g + 1) * w] = acc.astype(tok_ref.dtype)
        kout_ref[b, 0:WINDOW - nl, :] = ck_ref[b, nl:WINDOW, :]
        kout_ref[b, WINDOW - nl:WINDOW, :] = knew[0:nl]
        vout_ref[b, 0:WINDOW - nl, :] = cv_ref[b, nl:WINDOW, :]
        vout_ref[b, WINDOW - nl:WINDOW, :] = vnew[0:nl]


def _swa_sample(p, ck, cv, qn, kn, sinks, bd, nl, nb):
    b = p.shape[0]
    w = ck.shape[-1]
    nq = SW_GROUP
    cache = jax.ShapeDtypeStruct((b, WINDOW, w), F32)
    c3 = lambda i: (i, 0, 0)
    return pl.pallas_call(
        functools.partial(_swa_sample_kernel, nl=nl),
        grid=(b // nb,),
        in_specs=[
            pl.BlockSpec((nb, SAMPLE_PAD, nq * w), c3),
            pl.BlockSpec((nb, SAMPLE_PAD, w), lambda i: (i, 0, nq)),
            pl.BlockSpec((nb, SAMPLE_PAD, w), lambda i: (i, 0, nq + 1)),
            pl.BlockSpec((nb, WINDOW, w), c3),
            pl.BlockSpec((nb, WINDOW, w), c3),
            pl.BlockSpec((1, w), lambda i: (0, 0)),
            pl.BlockSpec((1, w), lambda i: (0, 0)),
            pl.BlockSpec(memory_space=pltpu.SMEM),
            pl.BlockSpec((w, w), lambda i: (0, 0)),
        ],
        out_specs=[
            pl.BlockSpec((nb, SAMPLE_PAD, nq * w), c3),
            pl.BlockSpec((nb, WINDOW, w), c3),
            pl.BlockSpec((nb, WINDOW, w), c3),
        ],
        out_shape=[jax.ShapeDtypeStruct((b, SAMPLE_PAD, nq * w), BF16), cache, cache],
        compiler_params=_params(("parallel",)),
        name="swa_sample",
    )(p, p, p, ck, cv, qn, kn, sinks, bd)


def _hgrn_consts(rows, group):
    t = np.arange(rows)
    loc, run = t % group, t // group
    same = run[:, None] == run[None, :]
    r = t[None, :]
    mats = [same & (r <= t[:, None]), same & (r > t[:, None])]
    masks = []
    h = group // 2
    while h >= 1:
        par = loc // (2 * h)
        right = (loc % (2 * h)) >= h
        bnd = (run * group + par * 2 * h + h - 1)[:, None]
        q_side = right[:, None] & (r > bnd) & (r <= t[:, None])
        k_side = (~right)[:, None] & (r > t[:, None]) & (r <= bnd)
        mats.append(q_side | k_side)
        masks.append(same & (par[:, None] == par[None, :]) & right[:, None] & (~right)[None, :])
        h //= 2
    masks.append(np.eye(rows, dtype=bool))
    w = np.concatenate(mats, axis=0).astype(np.float32)
    w2 = np.concatenate([w, w], axis=1)
    return jnp.asarray(w2, BF16), jnp.asarray(np.stack(masks).astype(np.float32)), len(masks) - 1


def _hgrn_gates(q_raw, z, lb):
    sp, sn = _sigmoid_pair(z)
    log2f = jnp.log2(lb + (1.0 - lb) * sp)
    kf = (1.0 - lb) * sn
    q = q_raw * _sigmoid(q_raw) * HG_HEAD ** -0.5
    return q, kf, log2f


def _hgrn_decays(log2f, w2_ref):
    hi = log2f.astype(BF16)
    lo = (log2f - hi.astype(F32)).astype(BF16)
    return jnp.exp2(_dot(w2_ref[...], jnp.concatenate([hi, lo], axis=0)))


def _hgrn_intra(q, kf, v, e, m_ref, nlev):
    rows = q.shape[0]
    att = m_ref[nlev] * _dot_nt(q.astype(BF16), kf.astype(BF16))
    for l in range(nlev):
        el = e[(2 + l) * rows:(3 + l) * rows]
        att += m_ref[l] * _dot_nt((q * el).astype(BF16), (kf * el).astype(BF16))
    return _dot(att.astype(BF16), v.astype(BF16))


def _hgrn_out(o, g, gn):
    ms = jnp.mean(o * o, axis=-1, keepdims=True)
    return (o * lax.rsqrt(ms + EPS) * gn * (g * _sigmoid(g))).astype(BF16)


def _hgrn_prompt_kernel(q_ref, f_ref, i_ref, g_ref, lb_ref, gn_ref, w2_ref, m_ref,
                        tok_ref, st_ref, s_scr, *, nlev):
    t = pl.program_id(1)

    @pl.when(t == 0)
    def _():
        s_scr[...] = jnp.zeros_like(s_scr)

    lb = lb_ref[...]
    gn = gn_ref[...]
    rows = HG_ROWS
    nh = q_ref.shape[1] // HG_HEAD

    def chunk(c, st):
        sl = pl.ds(pl.multiple_of(c * rows, rows), rows)
        v = i_ref[sl, :]
        g = g_ref[sl, :]
        q, kf, log2f = _hgrn_gates(q_ref[sl, :], f_ref[sl, :], lb)
        e = _hgrn_decays(log2f, w2_ref)
        qt = (q * e[0:rows]).astype(BF16)
        kh = (kf * e[rows:2 * rows]).astype(BF16)
        vb = v.astype(BF16)
        stb = st.astype(BF16)
        upd = []
        for h in range(nh):
            hs = slice(h * HG_HEAD, (h + 1) * HG_HEAD)
            o = _hgrn_intra(q[:, hs], kf[:, hs], v[:, hs], e[:, hs], m_ref, nlev)
            o += _dot_nt(qt[:, hs], stb[:, hs])
            tok_ref[sl, hs] = _hgrn_out(o, g[:, hs], gn[:, hs])
            upd.append(_dot_tn(vb[:, hs], kh[:, hs]))
        return st * e[rows - 1:rows, :] + jnp.concatenate(upd, axis=1)

    st = lax.fori_loop(0, q_ref.shape[0] // rows, chunk, s_scr[...])
    s_scr[...] = st

    @pl.when(t == pl.num_programs(1) - 1)
    def _():
        for h in range(nh):
            st_ref[h] = st[:, h * HG_HEAD:(h + 1) * HG_HEAD].T


def _hgrn_prompt(p, lb, gn, tb):
    b, l, _ = p.shape
    tw = lb.shape[1]
    nh = tw // HG_HEAD
    w2, masks, nlev = _hgrn_consts(HG_ROWS, HG_ROWS)
    sec = lambda s: pl.BlockSpec((None, tb, tw), lambda i, t: (i, t, s))
    return pl.pallas_call(
        functools.partial(_hgrn_prompt_kernel, nlev=nlev),
        grid=(b, l // tb),
        in_specs=[
            sec(0), sec(1), sec(2), sec(3),
            pl.BlockSpec((1, tw), lambda i, t: (0, 0)),
            pl.BlockSpec((1, tw), lambda i, t: (0, 0)),
            pl.BlockSpec(w2.shape, lambda i, t: (0, 0)),
            pl.BlockSpec(masks.shape, lambda i, t: (0, 0, 0)),
        ],
        out_specs=[
            pl.BlockSpec((None, tb, tw), lambda i, t: (i, t, 0)),
            pl.BlockSpec((None, nh, HG_HEAD, HG_HEAD), lambda i, t: (i, 0, 0, 0)),
        ],
        out_shape=[jax.ShapeDtypeStruct((b, l, tw), BF16),
                   jax.ShapeDtypeStruct((b, nh, HG_HEAD, HG_HEAD), F32)],
        scratch_shapes=[pltpu.VMEM((HG_HEAD, tw), F32)],
        compiler_params=_params(("parallel", "arbitrary")),
        name="hgrn_prompt",
    )(p, p, p, p, lb, gn, w2, masks)


def _hgrn_sample_kernel(q_ref, f_ref, i_ref, g_ref, s0_ref, lb_ref, gn_ref, w2_ref, m_ref,
                        tok_ref, st_ref, *, nlev, nl):
    rows = q_ref.shape[0]
    nb = rows // nl
    v = i_ref[...]
    q, kf, log2f = _hgrn_gates(q_ref[...], f_ref[...], lb_ref[...])
    e = _hgrn_decays(log2f, w2_ref)
    o_intra = _hgrn_intra(q, kf, v, e, m_ref, nlev)
    e_cum = e[0:rows]
    qt = q * e_cum
    kh = kf * e[rows:2 * rows]
    shift = nl.bit_length() - 1
    rowb = lax.broadcasted_iota(jnp.int32, (rows, 1), 0) >> shift
    s0cat = jnp.concatenate([s0_ref[b].astype(BF16) for b in range(nb)], axis=1)
    full = _dot(qt.astype(BF16), s0cat)
    o = o_intra
    for b in range(nb):
        o += jnp.where(rowb == b, full[:, b * HG_HEAD:(b + 1) * HG_HEAD], 0.0)
    tok_ref[...] = _hgrn_out(o, g_ref[...], gn_ref[...])
    xt = jnp.concatenate([kh, e_cum], axis=0).T
    colb = lax.broadcasted_iota(jnp.int32, (rows, nb * HG_HEAD), 1) >> 7
    vbd = jnp.where(rowb == colb, jnp.concatenate([v] * nb, axis=1), 0.0).astype(BF16)
    upd = _dot(xt[:, 0:rows].astype(BF16), vbd)
    for b in range(nb):
        last = rows + b * nl + nl - 1
        st_ref[b] = s0_ref[b] * xt[:, last:last + 1] + upd[:, b * HG_HEAD:(b + 1) * HG_HEAD]


def _hgrn_sample(p, s0, lb, gn, nl):
    rows = HG_ROWS
    nb = rows // nl
    b, nh = s0.shape[0], s0.shape[1]
    w2, masks, nlev = _hgrn_consts(rows, nl)
    sec = lambda s: pl.BlockSpec((rows, HG_HEAD), lambda i, h: (i, s * nh + h))
    st_spec = pl.BlockSpec((nb, None, HG_HEAD, HG_HEAD), lambda i, h: (i, h, 0, 0))
    return pl.pallas_call(
        functools.partial(_hgrn_sample_kernel, nlev=nlev, nl=nl),
        grid=(b // nb, nh),
        in_specs=[
            sec(0), sec(1), sec(2), sec(3), st_spec,
            pl.BlockSpec((1, HG_HEAD), lambda i, h: (0, h)),
            pl.BlockSpec((1, HG_HEAD), lambda i, h: (0, h)),
            pl.BlockSpec(w2.shape, lambda i, h: (0, 0)),
            pl.BlockSpec(masks.shape, lambda i, h: (0, 0, 0)),
        ],
        out_specs=[pl.BlockSpec((rows, HG_HEAD), lambda i, h: (i, h)), st_spec],
        out_shape=[jax.ShapeDtypeStruct((b * nl, nh * HG_HEAD), BF16),
                   jax.ShapeDtypeStruct(s0.shape, F32)],
        compiler_params=_params(("parallel", "parallel")),
        name="hgrn_sample",
    )(p, p, p, p, s0, lb, gn, w2, masks)


def _swa_head_perm():
    idx = [(k * SW_GROUP + g) * HEAD64 + d
           for g in range(SW_GROUP) for k in range(SW_KV_HEADS) for d in range(HEAD64)]
    return np.asarray(idx, np.int32)


def kernel(x_prompt, x_sample, cache_mem_k, cache_mem_v, state_hgrn, cache_swa_k, cache_swa_v,
           mem_prompt, norm_mix, norm_ffn, norm_mem, w_in_hgrn, hgrn_lb_raw, hgrn_out_norm,
           w_in_swa, swa_q_norm, swa_k_norm, swa_sinks, w_mem_kv, mem_q_norm, mem_k_norm,
           w_out, w_ffn_in, w_ffn_out):
    depth, d = norm_mix.shape
    bp, lp, _ = x_prompt.shape
    bs, ls, _ = x_sample.shape
    nm = mem_prompt.shape[1]
    mw = MEM_HEADS * HEAD64
    kvw = SW_KV_HEADS * HEAD64
    tokw = d - mw
    tm = 512

    seg = np.arange(mw) // HEAD64
    bd = jnp.asarray((seg[:, None] == seg[None, :]).astype(np.float32) / HEAD64, BF16)
    tile4 = lambda a: jnp.tile(a.astype(F32), (1, mw // HEAD64))[:, None, :]

    sm = jax.nn.softmax(hgrn_lb_raw.astype(F32), axis=0)
    lbs = jnp.clip(jnp.cumsum(sm, axis=0) - sm[0], 0.0, LB_MAX)

    perm = _swa_head_perm()
    w_in_swa_p = jnp.concatenate([w_in_swa[:, :, perm], w_in_swa[:, :, tokw:]], axis=2).astype(BF16)
    w_in_hgrn_b = w_in_hgrn.astype(BF16)
    w_out_b = w_out.astype(BF16)
    w1_b = w_ffn_in.astype(BF16)
    w2_b = w_ffn_out.astype(BF16)
    sinks_p = swa_sinks.astype(F32).reshape(-1, SW_KV_HEADS, SW_GROUP).transpose(0, 2, 1)
    sinks_p = sinks_p.reshape(-1, SW_KV_HEADS * SW_GROUP)
    mem_qn = tile4(mem_q_norm)
    swa_qn = tile4(swa_q_norm)
    swa_kn = tile4(swa_k_norm)

    mk_p, mv_p = _mem_kv(mem_prompt, norm_mem[:, None, :], w_mem_kv.astype(BF16),
                         tile4(mem_k_norm), bd)

    xp = x_prompt.reshape(bp * lp, d)
    xs = x_sample.reshape(bs * ls, d)
    pad_rows = lambda a: jnp.pad(a.reshape(bs, ls, a.shape[-1]),
                                 ((0, 0), (0, SAMPLE_PAD - ls), (0, 0)))
    hg_p, hg_s, swk_p, swv_p, swk_s, swv_s = [], [], [], [], [], []
    for i in range(depth):
        j = i // 2
        g_mix = norm_mix[i][None, :]
        g_ffn = norm_ffn[i][None, :]
        cmk = cache_mem_k[i].reshape(bs, nm, mw)
        cmv = cache_mem_v[i].reshape(bs, nm, mw)
        if i % 2 == 0:
            w_in = w_in_hgrn_b[j]
            wo = w_out_b[i]
            lb = lbs[j][None, :]
            gn = hgrn_out_norm[j][None, :].astype(F32)
            pp = _norm_matmul(xp, g_mix, w_in, tm).reshape(bp, lp, -1)
            tok_p, st_p = _hgrn_prompt(pp, lb, gn, 512)
            ps = _norm_matmul(xs, g_mix, w_in, tm)
            tok_s, st_s = _hgrn_sample(ps, state_hgrn[j], lb, gn, ls)
            hg_p.append(st_p)
            hg_s.append(st_s)
            cq_s = pad_rows(ps[:, 4 * tokw:])
            cq_col = 4 * tokw // mw
        else:
            w_in = w_in_swa_p[j]
            wo = jnp.concatenate([w_out_b[i][perm], w_out_b[i][tokw:]], axis=0)
            pp = _norm_matmul(xp, g_mix, w_in, tm).reshape(bp, lp, -1)
            tok_p, kb, vb = _swa_prompt(pp, swa_qn[j], swa_kn[j], sinks_p[j], bd)
            swk_p.append(kb.reshape(bp, WINDOW, SW_KV_HEADS, HEAD64))
            swv_p.append(vb.reshape(bp, WINDOW, SW_KV_HEADS, HEAD64))
            ps = pad_rows(_norm_matmul(xs, g_mix, w_in, tm))
            tok_s, ks, vs = _swa_sample(ps, cache_swa_k[j].reshape(bs, WINDOW, kvw),
                                        cache_swa_v[j].reshape(bs, WINDOW, kvw),
                                        swa_qn[j], swa_kn[j], sinks_p[j], bd, ls, 8)
            swk_s.append(ks.reshape(bs, WINDOW, SW_KV_HEADS, HEAD64))
            swv_s.append(vs.reshape(bs, WINDOW, SW_KV_HEADS, HEAD64))
            tok_s = tok_s[:, :ls].reshape(bs * ls, tokw)
            cq_s = ps[:, :, tokw + 2 * kvw:]
            cq_col = (tokw + 2 * kvw) // mw
        mo_p = _mem_attend(pp, cq_col, mk_p[i], mv_p[i], mem_qn[i], bd, 1, 512)
        mo_s = _mem_attend(cq_s, 0, cmk, cmv, mem_qn[i], bd, 8, SAMPLE_PAD)
        mo_s = mo_s[:, :ls].reshape(bs * ls, mw)
        xp = _mix_ffn(xp, tok_p.reshape(bp * lp, tokw), mo_p.reshape(bp * lp, mw),
                      wo, g_ffn, w1_b[i], w2_b[i], tm)
        xs = _mix_ffn(xs, tok_s, mo_s, wo, g_ffn, w1_b[i], w2_b[i], tm)

    mshape = (depth, bp, nm, MEM_HEADS, HEAD64)
    return (xp.reshape(bp, lp, d), xs.reshape(bs, ls, d),
            mk_p.reshape(mshape), mv_p.reshape(mshape),
            jnp.stack(hg_p), jnp.stack(hg_s),
            jnp.stack(swk_p), jnp.stack(swv_p), jnp.stack(swk_s), jnp.stack(swv_s))
```

```python
import functools

import numpy as np
import jax
import jax.numpy as jnp
from jax import lax
from jax.experimental import pallas as pl
from jax.experimental.pallas import tpu as pltpu

F32 = jnp.float32
BF16 = jnp.bfloat16
EPS = 1e-6
NEG = -1e30
LB_MAX = 0.999

HEAD64 = 64
MEM_HEADS = 4
SW_KV_HEADS = 4
SW_GROUP = 3
WINDOW = 128
HG_HEAD = 128
HG_ROWS = 64
SAMPLE_PAD = 8
LANES = 128

V7X_VMEM_BYTES = 64 * 1024 * 1024
VMEM_LIMIT = V7X_VMEM_BYTES - 8 * 1024 * 1024


def _params(sem):
    return pltpu.CompilerParams(dimension_semantics=sem, vmem_limit_bytes=VMEM_LIMIT)


def _dot(a, b):
    return jnp.dot(a, b, preferred_element_type=F32)


def _dot_nt(a, b):
    return lax.dot_general(a, b, (((1,), (1,)), ((), ())), preferred_element_type=F32)


def _dot_tn(a, b):
    return lax.dot_general(a, b, (((0,), (0,)), ((), ())), preferred_element_type=F32)


def _sigmoid(x):
    return 1.0 / (1.0 + jnp.exp(-x))


def _sigmoid_pair(z):
    e = jnp.exp(-jnp.abs(z))
    r = 1.0 / (1.0 + e)
    er = e * r
    pos = z >= 0
    return jnp.where(pos, r, er), jnp.where(pos, er, r)


def _rms_rows(x, g):
    ms = jnp.mean(x * x, axis=-1, keepdims=True)
    return x * lax.rsqrt(ms + EPS) * g


def _head_rms64(x, g, bd):
    ms = _dot((x * x).astype(BF16), bd)
    return x * lax.rsqrt(ms + EPS) * g


def _head_mask(width, h):
    lane = lax.broadcasted_iota(jnp.int32, (1, width), 1)
    return ((lane >> 6) == h).astype(F32)


def _norm_matmul_kernel(x_ref, g_ref, w_ref, o_ref):
    h = _rms_rows(x_ref[...], g_ref[...]).astype(BF16)
    o_ref[...] = _dot(h, w_ref[...])


def _norm_matmul(x, g, w, tm):
    m, d = x.shape
    n = w.shape[1]
    return pl.pallas_call(
        _norm_matmul_kernel,
        grid=(m // tm,),
        in_specs=[
            pl.BlockSpec((tm, d), lambda i: (i, 0)),
            pl.BlockSpec((1, d), lambda i: (0, 0)),
            pl.BlockSpec((d, n), lambda i: (0, 0)),
        ],
        out_specs=pl.BlockSpec((tm, n), lambda i: (i, 0)),
        out_shape=jax.ShapeDtypeStruct((m, n), F32),
        compiler_params=_params(("parallel",)),
        name="norm_matmul",
    )(x, g, w)


def _mix_ffn_kernel(x_ref, tok_ref, mo_ref, wo_ref, g_ref, w1_ref, w2_ref, o_ref, act_ref,
                    *, ff, chunk):
    tw = tok_ref.shape[1]
    x1 = (x_ref[...] + _dot(tok_ref[...], wo_ref[0:tw, :])
          + _dot(mo_ref[...], wo_ref[tw:, :]))
    o_ref[...] = x1
    h = _rms_rows(x1, g_ref[...]).astype(BF16)
    for c0 in range(0, ff, chunk):
        g = _dot(h, w1_ref[:, c0:c0 + chunk])
        u = _dot(h, w1_ref[:, ff + c0:ff + c0 + chunk])
        act_ref[:, c0:c0 + chunk] = (g * _sigmoid(g) * u).astype(BF16)
    o_ref[...] += _dot(act_ref[...], w2_ref[...])


def _mix_ffn(x, tok, mo, wo, g, w1, w2, tm):
    m, d = x.shape
    ff = w2.shape[0]
    tw, mw = tok.shape[1], mo.shape[1]
    const = lambda i: (0, 0)
    return pl.pallas_call(
        functools.partial(_mix_ffn_kernel, ff=ff, chunk=256),
        grid=(m // tm,),
        in_specs=[
            pl.BlockSpec((tm, d), lambda i: (i, 0)),
            pl.BlockSpec((tm, tw), lambda i: (i, 0)),
            pl.BlockSpec((tm, mw), lambda i: (i, 0)),
            pl.BlockSpec((tw + mw, d), const, pipeline_mode=pl.Buffered(1)),
            pl.BlockSpec((1, d), const),
            pl.BlockSpec((d, 2 * ff), const, pipeline_mode=pl.Buffered(1)),
            pl.BlockSpec((ff, d), const, pipeline_mode=pl.Buffered(1)),
        ],
        out_specs=pl.BlockSpec((tm, d), lambda i: (i, 0)),
        out_shape=jax.ShapeDtypeStruct((m, d), F32),
        scratch_shapes=[pltpu.VMEM((tm, ff), BF16)],
        compiler_params=_params(("parallel",)),
        name="mix_ffn",
    )(x, tok, mo, wo, g, w1, w2)


def _mem_kv_kernel(mem_ref, g_ref, w_ref, kn_ref, bd_ref, k_ref, v_ref):
    h = _rms_rows(mem_ref[...], g_ref[...]).astype(BF16)
    kv = _dot(h, w_ref[...])
    kw = k_ref.shape[0]
    k_ref[...] = _head_rms64(kv[:, :kw], kn_ref[...], bd_ref[...]).T
    v_ref[...] = kv[:, kw:].T


def _mem_kv(mem, g, w, kn, bd):
    depth = w.shape[0]
    b, nm, d = mem.shape
    kw = w.shape[2] // 2
    out = jax.ShapeDtypeStruct((depth, b, kw, nm), F32)
    return pl.pallas_call(
        _mem_kv_kernel,
        grid=(depth, b),
        in_specs=[
            pl.BlockSpec((None, nm, d), lambda i, j: (j, 0, 0)),
            pl.BlockSpec((None, 1, d), lambda i, j: (i, 0, 0)),
            pl.BlockSpec((None, d, 2 * kw), lambda i, j: (i, 0, 0)),
            pl.BlockSpec((None, 1, kw), lambda i, j: (i, 0, 0)),
            pl.BlockSpec((kw, kw), lambda i, j: (0, 0)),
        ],
        out_specs=[pl.BlockSpec((None, None, kw, nm), lambda i, j: (i, j, 0, 0))] * 2,
        out_shape=[out, out],
        compiler_params=_params(("parallel", "parallel")),
        name="mem_kv",
    )(mem, g, w, kn, bd)


def _mem_attend_kernel(q_ref, k_ref, v_ref, qn_ref, bd_ref, o_ref):
    nb, tq, w = q_ref.shape
    masks = [_head_mask(w, h) for h in range(MEM_HEADS)]
    for b in range(nb):
        qn = _head_rms64(q_ref[b], qn_ref[...], bd_ref[...]) * HEAD64 ** -0.5
        kt = k_ref[b].astype(BF16)
        vt = v_ref[b].astype(BF16)
        qbd = jnp.concatenate([(qn * m).astype(BF16) for m in masks], axis=0)
        s = _dot(qbd, kt)
        p = jnp.exp(s - jnp.max(s, axis=-1, keepdims=True))
        o = _dot_nt(p.astype(BF16), vt) / jnp.sum(p, axis=-1, keepdims=True)
        acc = o[0:tq] * masks[0]
        for h in range(1, MEM_HEADS):
            acc += o[h * tq:(h + 1) * tq] * masks[h]
        o_ref[b] = acc.astype(o_ref.dtype)


def _mem_attend(q, col, mkt, mvt, layer, qn, bd, nb, tq):
    bsz, l, _ = q.shape
    w, nm = mkt.shape[2], mkt.shape[3]
    return pl.pallas_call(
        _mem_attend_kernel,
        grid=(bsz // nb, l // tq),
        in_specs=[
            pl.BlockSpec((nb, tq, w), lambda i, t: (i, t, col)),
            pl.BlockSpec((None, nb, w, nm), lambda i, t: (layer, i, 0, 0)),
            pl.BlockSpec((None, nb, w, nm), lambda i, t: (layer, i, 0, 0)),
            pl.BlockSpec((1, w), lambda i, t: (0, 0)),
            pl.BlockSpec((w, w), lambda i, t: (0, 0)),
        ],
        out_specs=pl.BlockSpec((nb, tq, w), lambda i, t: (i, t, 0)),
        out_shape=jax.ShapeDtypeStruct((bsz, l, w), BF16),
        compiler_params=_params(("parallel", "parallel")),
        name="mem_attend",
    )(q, mkt, mvt, qn, bd)


def _swa_prompt_kernel(q_ref, kc_ref, vc_ref, kp_ref, vp_ref, qn_ref, kn_ref, sink_ref, bd_ref,
                       tok_ref, kout_ref, vout_ref):
    n = pl.program_id(1)
    bd = bd_ref[...]
    w = kc_ref.shape[-1]
    kc = _head_rms64(kc_ref[...], kn_ref[...], bd)
    kp = _head_rms64(kp_ref[...], kn_ref[...], bd)
    vc = vc_ref[...]
    kk = jnp.concatenate([kp, kc], axis=0).astype(BF16)
    vvt = jnp.concatenate([vp_ref[...], vc], axis=0).T.astype(BF16)
    r = lax.broadcasted_iota(jnp.int32, (2 * WINDOW, WINDOW), 0)
    qi = lax.broadcasted_iota(jnp.int32, (2 * WINDOW, WINDOW), 1)
    valid = (r > qi) & (r <= qi + WINDOW) & ((r >= WINDOW) | (n > 0))
    masks = [_head_mask(w, k) for k in range(SW_KV_HEADS)]
    pieces = []
    for g in range(SW_GROUP):
        qg = _head_rms64(q_ref[:, g * w:(g + 1) * w], qn_ref[...], bd) * HEAD64 ** -0.5
        pieces += [(qg * m).astype(BF16) for m in masks]
    st = _dot_nt(kk, jnp.concatenate(pieces, axis=0))
    outs = []
    for h in range(SW_GROUP * SW_KV_HEADS):
        k = h % SW_KV_HEADS
        s = jnp.where(valid, st[:, h * WINDOW:(h + 1) * WINDOW], NEG)
        sink = sink_ref[h]
        m = jnp.maximum(jnp.max(s, axis=0, keepdims=True), sink)
        p = jnp.exp(s - m)
        den = jnp.sum(p, axis=0, keepdims=True) + jnp.exp(sink - m)
        outs.append(_dot(vvt[k * HEAD64:(k + 1) * HEAD64, :], p.astype(BF16)) / den)
    tok_ref[...] = jnp.concatenate(outs, axis=0).T.astype(tok_ref.dtype)

    @pl.when(n == pl.num_programs(1) - 1)
    def _():
        kout_ref[...] = kc.T
        vout_ref[...] = vc.T


def _swa_prompt(p, qn, kn, sinks, bd):
    b, l, _ = p.shape
    w = SW_KV_HEADS * HEAD64
    nq = SW_GROUP
    cache = jax.ShapeDtypeStruct((b, w, WINDOW), F32)
    return pl.pallas_call(
        _swa_prompt_kernel,
        grid=(b, l // WINDOW),
        in_specs=[
            pl.BlockSpec((None, WINDOW, nq * w), lambda i, n: (i, n, 0)),
            pl.BlockSpec((None, WINDOW, w), lambda i, n: (i, n, nq)),
            pl.BlockSpec((None, WINDOW, w), lambda i, n: (i, n, nq + 1)),
            pl.BlockSpec((None, WINDOW, w), lambda i, n: (i, jnp.maximum(n - 1, 0), nq)),
            pl.BlockSpec((None, WINDOW, w), lambda i, n: (i, jnp.maximum(n - 1, 0), nq + 1)),
            pl.BlockSpec((1, w), lambda i, n: (0, 0)),
            pl.BlockSpec((1, w), lambda i, n: (0, 0)),
            pl.BlockSpec(memory_space=pltpu.SMEM),
            pl.BlockSpec((w, w), lambda i, n: (0, 0)),
        ],
        out_specs=[
            pl.BlockSpec((None, WINDOW, nq * w), lambda i, n: (i, n, 0)),
            pl.BlockSpec((None, w, WINDOW), lambda i, n: (i, 0, 0)),
            pl.BlockSpec((None, w, WINDOW), lambda i, n: (i, 0, 0)),
        ],
        out_shape=[jax.ShapeDtypeStruct((b, l, nq * w), BF16), cache, cache],
        compiler_params=_params(("parallel", "arbitrary")),
        name="swa_prompt",
    )(p, p, p, p, p, qn, kn, sinks, bd)


def _swa_sample_kernel(q_ref, kn_ref_in, vn_ref_in, ck_ref, cv_ref, qn_ref, kn_ref, sink_ref,
                       bd_ref, tok_ref, kout_ref, vout_ref, *, nl):
    nb = q_ref.shape[0]
    w = ck_ref.shape[1]
    rows = SAMPLE_PAD
    nkeys = 2 * WINDOW
    bd = bd_ref[...]
    nh = SW_GROUP * SW_KV_HEADS
    masks = [_head_mask(w, k) for k in range(SW_KV_HEADS)]
    row = lax.broadcasted_iota(jnp.int32, (nh * rows, nkeys), 0)
    r = lax.broadcasted_iota(jnp.int32, (nh * rows, nkeys), 1)
    j = row & (rows - 1)
    valid = (r > j) & (r <= j + WINDOW)
    hrow = lax.broadcasted_iota(jnp.int32, (nh * rows, 1), 0) >> 3
    sink = jnp.zeros((nh * rows, 1), F32)
    for h in range(nh):
        sink = jnp.where(hrow == h, sink_ref[h], sink)
    zpad = jnp.zeros((WINDOW - rows, w), F32)
    for b in range(nb):
        knew = _head_rms64(kn_ref_in[b], kn_ref[...], bd)
        kt = jnp.concatenate([ck_ref[b], jnp.concatenate([knew, zpad], axis=0).T], axis=1)
        vt = jnp.concatenate([cv_ref[b], jnp.concatenate([vn_ref_in[b], zpad], axis=0).T], axis=1)
        pieces = []
        for g in range(SW_GROUP):
            qg = _head_rms64(q_ref[b, :, g * w:(g + 1) * w], qn_ref[...], bd) * HEAD64 ** -0.5
            pieces += [(qg * masks[k]).astype(BF16) for k in range(SW_KV_HEADS)]
        s = _dot(jnp.concatenate(pieces, axis=0), kt.astype(BF16))
        s = jnp.where(valid, s, NEG)
        m = jnp.maximum(jnp.max(s, axis=-1, keepdims=True), sink)
        p = jnp.exp(s - m)
        den = jnp.sum(p, axis=-1, keepdims=True) + jnp.exp(sink - m)
        o = _dot_nt(p.astype(BF16), vt.astype(BF16)) / den
        for g in range(SW_GROUP):
            acc = jnp.zeros((rows, w), F32)
            for k in range(SW_KV_HEADS):
                h = g * SW_KV_HEADS + k
                acc += o[h * rows:(h + 1) * rows] * masks[k]
            tok_ref[b, :, g * w:(g + 1) * w] = acc.astype(tok_ref.dtype)
        kout_ref[b] = kt[:, nl:nl + WINDOW]
        vout_ref[b] = vt[:, nl:nl + WINDOW]


def _swa_sample(p, ckt, cvt, layer, qn, kn, sinks, bd, nl, nb):
    b = p.shape[0]
    w = ckt.shape[2]
    nq = SW_GROUP
    cache = jax.ShapeDtypeStruct((b, w, WINDOW), F32)
    c3 = lambda i: (i, 0, 0)
    c4 = lambda i: (layer, i, 0, 0)
    return pl.pallas_call(
        functools.partial(_swa_sample_kernel, nl=nl),
        grid=(b // nb,),
        in_specs=[
            pl.BlockSpec((nb, SAMPLE_PAD, nq * w), c3),
            pl.BlockSpec((nb, SAMPLE_PAD, w), lambda i: (i, 0, nq)),
            pl.BlockSpec((nb, SAMPLE_PAD, w), lambda i: (i, 0, nq + 1)),
            pl.BlockSpec((None, nb, w, WINDOW), c4),
            pl.BlockSpec((None, nb, w, WINDOW), c4),
            pl.BlockSpec((1, w), lambda i: (0, 0)),
            pl.BlockSpec((1, w), lambda i: (0, 0)),
            pl.BlockSpec(memory_space=pltpu.SMEM),
            pl.BlockSpec((w, w), lambda i: (0, 0)),
        ],
        out_specs=[
            pl.BlockSpec((nb, SAMPLE_PAD, nq * w), c3),
            pl.BlockSpec((nb, w, WINDOW), c3),
            pl.BlockSpec((nb, w, WINDOW), c3),
        ],
        out_shape=[jax.ShapeDtypeStruct((b, SAMPLE_PAD, nq * w), BF16), cache, cache],
        compiler_params=_params(("parallel",)),
        name="swa_sample",
    )(p, p, p, ckt, cvt, qn, kn, sinks, bd)


def _hgrn_consts(rows, group):
    t = np.arange(rows)
    loc, run = t % group, t // group
    same = run[:, None] == run[None, :]
    r = t[None, :]
    mats = [same & (r <= t[:, None]), same & (r > t[:, None])]
    masks = []
    h = group // 2
    while h >= 1:
        par = loc // (2 * h)
        right = (loc % (2 * h)) >= h
        bnd = (run * group + par * 2 * h + h - 1)[:, None]
        q_side = right[:, None] & (r > bnd) & (r <= t[:, None])
        k_side = (~right)[:, None] & (r > t[:, None]) & (r <= bnd)
        mats.append(q_side | k_side)
        masks.append(same & (par[:, None] == par[None, :]) & right[:, None] & (~right)[None, :])
        h //= 2
    masks.append(np.eye(rows, dtype=bool))
    w = np.concatenate(mats, axis=0).astype(np.float32)
    w2 = np.concatenate([w, w], axis=1)
    return jnp.asarray(w2, BF16), jnp.asarray(np.stack(masks).astype(np.float32)), len(masks) - 1


def _hgrn_gates(q_raw, z, lb):
    sp, sn = _sigmoid_pair(z)
    log2f = jnp.log2(lb + (1.0 - lb) * sp)
    kf = (1.0 - lb) * sn
    q = q_raw * _sigmoid(q_raw) * HG_HEAD ** -0.5
    return q, kf, log2f


def _hgrn_decays(log2f, w2_ref):
    hi = log2f.astype(BF16)
    lo = (log2f - hi.astype(F32)).astype(BF16)
    return jnp.exp2(_dot(w2_ref[...], jnp.concatenate([hi, lo], axis=0)))


def _hgrn_intra(q, kf, v, e, m_ref, nlev):
    rows = q.shape[0]
    att = m_ref[nlev] * _dot_nt(q.astype(BF16), kf.astype(BF16))
    for l in range(nlev):
        el = e[(2 + l) * rows:(3 + l) * rows]
        att += m_ref[l] * _dot_nt((q * el).astype(BF16), (kf * el).astype(BF16))
    return _dot(att.astype(BF16), v.astype(BF16))


def _hgrn_out(o, g, gn):
    ms = jnp.mean(o * o, axis=-1, keepdims=True)
    return (o * lax.rsqrt(ms + EPS) * gn * (g * _sigmoid(g))).astype(BF16)


def _hgrn_prompt_kernel(q_ref, f_ref, i_ref, g_ref, lb_ref, gn_ref, w2_ref, m_ref,
                        tok_ref, st_ref, s_scr, *, nlev):
    t = pl.program_id(1)

    @pl.when(t == 0)
    def _():
        s_scr[...] = jnp.zeros_like(s_scr)

    lb = lb_ref[...]
    gn = gn_ref[...]
    rows = HG_ROWS
    nh = q_ref.shape[1] // HG_HEAD

    def chunk(c, st):
        sl = pl.ds(pl.multiple_of(c * rows, rows), rows)
        v = i_ref[sl, :]
        g = g_ref[sl, :]
        q, kf, log2f = _hgrn_gates(q_ref[sl, :], f_ref[sl, :], lb)
        e = _hgrn_decays(log2f, w2_ref)
        qt = (q * e[0:rows]).astype(BF16)
        kh = (kf * e[rows:2 * rows]).astype(BF16)
        vb = v.astype(BF16)
        stb = st.astype(BF16)
        upd = []
        for h in range(nh):
            hs = slice(h * HG_HEAD, (h + 1) * HG_HEAD)
            o = _hgrn_intra(q[:, hs], kf[:, hs], v[:, hs], e[:, hs], m_ref, nlev)
            o += _dot_nt(qt[:, hs], stb[:, hs])
            tok_ref[sl, hs] = _hgrn_out(o, g[:, hs], gn[:, hs])
            upd.append(_dot_tn(vb[:, hs], kh[:, hs]))
        return st * e[rows - 1:rows, :] + jnp.concatenate(upd, axis=1)

    st = lax.fori_loop(0, q_ref.shape[0] // rows, chunk, s_scr[...], unroll=2)
    s_scr[...] = st

    @pl.when(t == pl.num_programs(1) - 1)
    def _():
        for h in range(nh):
            st_ref[h] = st[:, h * HG_HEAD:(h + 1) * HG_HEAD].T


def _hgrn_prompt(p, lb, gn, tb):
    b, l, _ = p.shape
    tw = lb.shape[1]
    nh = tw // HG_HEAD
    w2, masks, nlev = _hgrn_consts(HG_ROWS, HG_ROWS)
    sec = lambda s: pl.BlockSpec((None, tb, tw), lambda i, t: (i, t, s))
    return pl.pallas_call(
        functools.partial(_hgrn_prompt_kernel, nlev=nlev),
        grid=(b, l // tb),
        in_specs=[
            sec(0), sec(1), sec(2), sec(3),
            pl.BlockSpec((1, tw), lambda i, t: (0, 0)),
            pl.BlockSpec((1, tw), lambda i, t: (0, 0)),
            pl.BlockSpec(w2.shape, lambda i, t: (0, 0)),
            pl.BlockSpec(masks.shape, lambda i, t: (0, 0, 0)),
        ],
        out_specs=[
            pl.BlockSpec((None, tb, tw), lambda i, t: (i, t, 0)),
            pl.BlockSpec((None, nh, HG_HEAD, HG_HEAD), lambda i, t: (i, 0, 0, 0)),
        ],
        out_shape=[jax.ShapeDtypeStruct((b, l, tw), BF16),
                   jax.ShapeDtypeStruct((b, nh, HG_HEAD, HG_HEAD), F32)],
        scratch_shapes=[pltpu.VMEM((HG_HEAD, tw), F32)],
        compiler_params=_params(("parallel", "arbitrary")),
        name="hgrn_prompt",
    )(p, p, p, p, lb, gn, w2, masks)


def _hgrn_sample_kernel(q_ref, f_ref, i_ref, g_ref, s0_ref, lb_ref, gn_ref, w2_ref, m_ref,
                        tok_ref, st_ref, *, nlev, nl):
    rows = q_ref.shape[0]
    nb = rows // nl
    v = i_ref[...]
    q, kf, log2f = _hgrn_gates(q_ref[...], f_ref[...], lb_ref[...])
    e = _hgrn_decays(log2f, w2_ref)
    o_intra = _hgrn_intra(q, kf, v, e, m_ref, nlev)
    e_cum = e[0:rows]
    qt = q * e_cum
    kh = kf * e[rows:2 * rows]
    shift = nl.bit_length() - 1
    rowb = lax.broadcasted_iota(jnp.int32, (rows, 1), 0) >> shift
    s0cat = jnp.concatenate([s0_ref[b].astype(BF16) for b in range(nb)], axis=1)
    full = _dot(qt.astype(BF16), s0cat)
    o = o_intra
    for b in range(nb):
        o += jnp.where(rowb == b, full[:, b * HG_HEAD:(b + 1) * HG_HEAD], 0.0)
    tok_ref[...] = _hgrn_out(o, g_ref[...], gn_ref[...])
    xt = jnp.concatenate([kh, e_cum], axis=0).T
    colb = lax.broadcasted_iota(jnp.int32, (rows, nb * HG_HEAD), 1) >> 7
    vbd = jnp.where(rowb == colb, jnp.concatenate([v] * nb, axis=1), 0.0).astype(BF16)
    upd = _dot(xt[:, 0:rows].astype(BF16), vbd)
    for b in range(nb):
        last = rows + b * nl + nl - 1
        st_ref[b] = s0_ref[b] * xt[:, last:last + 1] + upd[:, b * HG_HEAD:(b + 1) * HG_HEAD]


def _hgrn_sample(p, s0, layer, lb, gn, nl):
    rows = HG_ROWS
    nb = rows // nl
    b, nh = s0.shape[1], s0.shape[2]
    w2, masks, nlev = _hgrn_consts(rows, nl)
    sec = lambda s: pl.BlockSpec((rows, HG_HEAD), lambda i, h: (i, s * nh + h))
    st_spec = pl.BlockSpec((nb, None, HG_HEAD, HG_HEAD), lambda i, h: (i, h, 0, 0))
    s0_spec = pl.BlockSpec((None, nb, None, HG_HEAD, HG_HEAD), lambda i, h: (layer, i, h, 0, 0))
    return pl.pallas_call(
        functools.partial(_hgrn_sample_kernel, nlev=nlev, nl=nl),
        grid=(b // nb, nh),
        in_specs=[
            sec(0), sec(1), sec(2), sec(3), s0_spec,
            pl.BlockSpec((1, HG_HEAD), lambda i, h: (0, h)),
            pl.BlockSpec((1, HG_HEAD), lambda i, h: (0, h)),
            pl.BlockSpec(w2.shape, lambda i, h: (0, 0)),
            pl.BlockSpec(masks.shape, lambda i, h: (0, 0, 0)),
        ],
        out_specs=[pl.BlockSpec((rows, HG_HEAD), lambda i, h: (i, h)), st_spec],
        out_shape=[jax.ShapeDtypeStruct((b * nl, nh * HG_HEAD), BF16),
                   jax.ShapeDtypeStruct(s0.shape[1:], F32)],
        compiler_params=_params(("parallel", "parallel")),
        name="hgrn_sample",
    )(p, p, p, p, s0, lb, gn, w2, masks)


def _swa_head_perm():
    idx = [(k * SW_GROUP + g) * HEAD64 + d
           for g in range(SW_GROUP) for k in range(SW_KV_HEADS) for d in range(HEAD64)]
    return np.asarray(idx, np.int32)


def kernel(x_prompt, x_sample, cache_mem_k, cache_mem_v, state_hgrn, cache_swa_k, cache_swa_v,
           mem_prompt, norm_mix, norm_ffn, norm_mem, w_in_hgrn, hgrn_lb_raw, hgrn_out_norm,
           w_in_swa, swa_q_norm, swa_k_norm, swa_sinks, w_mem_kv, mem_q_norm, mem_k_norm,
           w_out, w_ffn_in, w_ffn_out):
    depth, d = norm_mix.shape
    bp, lp, _ = x_prompt.shape
    bs, ls, _ = x_sample.shape
    nm = mem_prompt.shape[1]
    mw = MEM_HEADS * HEAD64
    kvw = SW_KV_HEADS * HEAD64
    tokw = d - mw
    tm = 512

    seg = np.arange(mw) // HEAD64
    bd = jnp.asarray((seg[:, None] == seg[None, :]).astype(np.float32) / HEAD64, BF16)
    tile4 = lambda a: jnp.tile(a.astype(F32), (1, mw // HEAD64))[:, None, :]

    sm = jax.nn.softmax(hgrn_lb_raw.astype(F32), axis=0)
    lbs = jnp.clip(jnp.cumsum(sm, axis=0) - sm[0], 0.0, LB_MAX)

    perm = _swa_head_perm()
    w_in_swa_p = jnp.concatenate([w_in_swa[:, :, perm], w_in_swa[:, :, tokw:]], axis=2).astype(BF16)
    w_in_hgrn_b = w_in_hgrn.astype(BF16)
    w_out_b = w_out.astype(BF16)
    w1_b = w_ffn_in.astype(BF16)
    w2_b = w_ffn_out.astype(BF16)
    sinks_p = swa_sinks.astype(F32).reshape(-1, SW_KV_HEADS, SW_GROUP).transpose(0, 2, 1)
    sinks_p = sinks_p.reshape(-1, SW_KV_HEADS * SW_GROUP)
    mem_qn = tile4(mem_q_norm)
    swa_qn = tile4(swa_q_norm)
    swa_kn = tile4(swa_k_norm)

    mk_p, mv_p = _mem_kv(mem_prompt, norm_mem[:, None, :], w_mem_kv.astype(BF16),
                         tile4(mem_k_norm), bd)

    to_t = lambda a: a.transpose(0, 1, 3, 4, 2).reshape(a.shape[0], a.shape[1], -1, a.shape[2])
    from_t = lambda a, heads: a.reshape(a.shape[:-2] + (heads, HEAD64, a.shape[-1])).transpose(
        *range(a.ndim - 2), a.ndim, a.ndim - 2, a.ndim - 1)
    cmk_t, cmv_t = to_t(cache_mem_k), to_t(cache_mem_v)
    csk_t, csv_t = to_t(cache_swa_k), to_t(cache_swa_v)

    xp = x_prompt.reshape(bp * lp, d)
    xs = x_sample.reshape(bs * ls, d)
    pad_rows = lambda a: jnp.pad(a.reshape(bs, ls, a.shape[-1]),
                                 ((0, 0), (0, SAMPLE_PAD - ls), (0, 0)))
    hg_p, hg_s, swk_p, swv_p, swk_s, swv_s = [], [], [], [], [], []
    for i in range(depth):
        j = i // 2
        g_mix = norm_mix[i][None, :]
        g_ffn = norm_ffn[i][None, :]
        if i % 2 == 0:
            w_in = w_in_hgrn_b[j]
            wo = w_out_b[i]
            lb = lbs[j][None, :]
            gn = hgrn_out_norm[j][None, :].astype(F32)
            pp = _norm_matmul(xp, g_mix, w_in, tm).reshape(bp, lp, -1)
            tok_p, st_p = _hgrn_prompt(pp, lb, gn, 512)
            ps = _norm_matmul(xs, g_mix, w_in, tm)
            tok_s, st_s = _hgrn_sample(ps, state_hgrn, j, lb, gn, ls)
            hg_p.append(st_p)
            hg_s.append(st_s)
            cq_s = pad_rows(ps[:, 4 * tokw:])
            cq_col = 4 * tokw // mw
        else:
            w_in = w_in_swa_p[j]
            wo = jnp.concatenate([w_out_b[i][perm], w_out_b[i][tokw:]], axis=0)
            pp = _norm_matmul(xp, g_mix, w_in, tm).reshape(bp, lp, -1)
            tok_p, kb, vb = _swa_prompt(pp, swa_qn[j], swa_kn[j], sinks_p[j], bd)
            swk_p.append(kb)
            swv_p.append(vb)
            ps = pad_rows(_norm_matmul(xs, g_mix, w_in, tm))
            tok_s, ks, vs = _swa_sample(ps, csk_t, csv_t, j, swa_qn[j], swa_kn[j], sinks_p[j],
                                        bd, ls, 8)
            swk_s.append(ks)
            swv_s.append(vs)
            tok_s = tok_s[:, :ls].reshape(bs * ls, tokw)
            cq_s = ps[:, :, tokw + 2 * kvw:]
            cq_col = (tokw + 2 * kvw) // mw
        mo_p = _mem_attend(pp, cq_col, mk_p, mv_p, i, mem_qn[i], bd, 1, 512)
        mo_s = _mem_attend(cq_s, 0, cmk_t, cmv_t, i, mem_qn[i], bd, 8, SAMPLE_PAD)
        mo_s = mo_s[:, :ls].reshape(bs * ls, mw)
        xp = _mix_ffn(xp, tok_p.reshape(bp * lp, tokw), mo_p.reshape(bp * lp, mw),
                      wo, g_ffn, w1_b[i], w2_b[i], tm)
        xs = _mix_ffn(xs, tok_s, mo_s, wo, g_ffn, w1_b[i], w2_b[i], tm)

    return (xp.reshape(bp, lp, d), xs.reshape(bs, ls, d),
            from_t(mk_p, MEM_HEADS), from_t(mv_p, MEM_HEADS),
            jnp.stack(hg_p), jnp.stack(hg_s),
            from_t(jnp.stack(swk_p), SW_KV_HEADS), from_t(jnp.stack(swv_p), SW_KV_HEADS),
            from_t(jnp.stack(swk_s), SW_KV_HEADS), from_t(jnp.stack(swv_s), SW_KV_HEADS))
```

```python
import functools

import numpy as np
import jax
import jax.numpy as jnp
from jax import lax
from jax.experimental import pallas as pl
from jax.experimental.pallas import tpu as pltpu

F32 = jnp.float32
BF16 = jnp.bfloat16
EPS = 1e-6
NEG = -1e30
LB_MAX = 0.999
LOG2E = 1.4426950408889634

HEAD64 = 64
MEM_HEADS = 4
SW_KV_HEADS = 4
SW_GROUP = 3
WINDOW = 128
HG_HEAD = 128
HG_ROWS = 64
SAMPLE_PAD = 8
LANES = 128

V7X_VMEM_BYTES = 64 * 1024 * 1024
VMEM_LIMIT = V7X_VMEM_BYTES - 8 * 1024 * 1024


def _params(sem):
    return pltpu.CompilerParams(dimension_semantics=sem, vmem_limit_bytes=VMEM_LIMIT)


def _dot(a, b):
    return jnp.dot(a, b, preferred_element_type=F32)


def _dot_nt(a, b):
    return lax.dot_general(a, b, (((1,), (1,)), ((), ())), preferred_element_type=F32)


def _dot_tn(a, b):
    return lax.dot_general(a, b, (((0,), (0,)), ((), ())), preferred_element_type=F32)


def _sigmoid(x):
    return 1.0 / (1.0 + jnp.exp(-x))


def _sigmoid_pair(z):
    e = jnp.exp(-jnp.abs(z))
    r = 1.0 / (1.0 + e)
    er = e * r
    pos = z >= 0
    return jnp.where(pos, r, er), jnp.where(pos, er, r)


def _rms_rows(x, g):
    ms = jnp.mean(x * x, axis=-1, keepdims=True)
    return x * lax.rsqrt(ms + EPS) * g


def _head_rms64(x, g, bd):
    ms = _dot((x * x).astype(BF16), bd)
    return x * lax.rsqrt(ms + EPS) * g


def _head_mask(width, h):
    lane = lax.broadcasted_iota(jnp.int32, (1, width), 1)
    return ((lane >> 6) == h).astype(F32)


def _norm_matmul_kernel(x_ref, g_ref, w_ref, o_ref):
    h = _rms_rows(x_ref[...], g_ref[...]).astype(BF16)
    o_ref[...] = _dot(h, w_ref[...])


def _norm_matmul(x, g, g_idx, w, w_idx, tm):
    m, d = x.shape
    n = w.shape[2]
    tm = min(tm, m)
    return pl.pallas_call(
        _norm_matmul_kernel,
        grid=(m // tm,),
        in_specs=[
            pl.BlockSpec((tm, d), lambda i: (i, 0)),
            pl.BlockSpec((None, 1, d), lambda i: (g_idx, 0, 0)),
            pl.BlockSpec((None, d, n), lambda i: (w_idx, 0, 0)),
        ],
        out_specs=pl.BlockSpec((tm, n), lambda i: (i, 0)),
        out_shape=jax.ShapeDtypeStruct((m, n), F32),
        compiler_params=_params(("parallel",)),
        name="norm_matmul",
    )(x, g, w)


def _mix_ffn_kernel(x_ref, tok_ref, mo_ref, wo_ref, g_ref, w1_ref, w2_ref, o_ref, act_ref,
                    *, ff, chunk):
    tw = tok_ref.shape[1]
    x1 = (x_ref[...] + _dot(tok_ref[...], wo_ref[0:tw, :])
          + _dot(mo_ref[...], wo_ref[tw:, :]))
    o_ref[...] = x1
    h = _rms_rows(x1, g_ref[...]).astype(BF16)
    for c0 in range(0, ff, chunk):
        g = _dot(h, w1_ref[:, c0:c0 + chunk])
        u = _dot(h, w1_ref[:, ff + c0:ff + c0 + chunk])
        act_ref[:, c0:c0 + chunk] = (g * _sigmoid(g) * u).astype(BF16)
    o_ref[...] += _dot(act_ref[...], w2_ref[...])


def _mix_ffn(x, tok, mo, wo, g, w1, w2, layer, tm):
    m, d = x.shape
    ff = w2.shape[1]
    tm = min(tm, m)
    tw, mw = tok.shape[1], mo.shape[1]
    const = lambda i: (layer, 0, 0)
    return pl.pallas_call(
        functools.partial(_mix_ffn_kernel, ff=ff, chunk=256),
        grid=(m // tm,),
        in_specs=[
            pl.BlockSpec((tm, d), lambda i: (i, 0)),
            pl.BlockSpec((tm, tw), lambda i: (i, 0)),
            pl.BlockSpec((tm, mw), lambda i: (i, 0)),
            pl.BlockSpec((None, tw + mw, d), const, pipeline_mode=pl.Buffered(1)),
            pl.BlockSpec((None, 1, d), const),
            pl.BlockSpec((None, d, 2 * ff), const, pipeline_mode=pl.Buffered(1)),
            pl.BlockSpec((None, ff, d), const, pipeline_mode=pl.Buffered(1)),
        ],
        out_specs=pl.BlockSpec((tm, d), lambda i: (i, 0)),
        out_shape=jax.ShapeDtypeStruct((m, d), F32),
        scratch_shapes=[pltpu.VMEM((tm, ff), BF16)],
        compiler_params=_params(("parallel",)),
        name="mix_ffn",
    )(x, tok, mo, wo, g, w1, w2)


def _mem_kv_kernel(mem_ref, g_ref, w_ref, kn_ref, bd_ref, k_ref, v_ref):
    h = _rms_rows(mem_ref[...], g_ref[...]).astype(BF16)
    kv = _dot(h, w_ref[...])
    kw = k_ref.shape[0]
    k_ref[...] = _head_rms64(kv[:, :kw], kn_ref[...], bd_ref[...]).T
    v_ref[...] = kv[:, kw:].T


def _mem_kv(mem, g, w, kn, bd):
    depth = w.shape[0]
    b, nm, d = mem.shape
    kw = w.shape[2] // 2
    out = jax.ShapeDtypeStruct((depth, b, kw, nm), F32)
    return pl.pallas_call(
        _mem_kv_kernel,
        grid=(depth, b),
        in_specs=[
            pl.BlockSpec((None, nm, d), lambda i, j: (j, 0, 0)),
            pl.BlockSpec((None, 1, d), lambda i, j: (i, 0, 0)),
            pl.BlockSpec((None, d, 2 * kw), lambda i, j: (i, 0, 0)),
            pl.BlockSpec((None, 1, kw), lambda i, j: (i, 0, 0)),
            pl.BlockSpec((kw, kw), lambda i, j: (0, 0)),
        ],
        out_specs=[pl.BlockSpec((None, None, kw, nm), lambda i, j: (i, j, 0, 0))] * 2,
        out_shape=[out, out],
        compiler_params=_params(("parallel", "parallel")),
        name="mem_kv",
    )(mem, g, w, kn, bd)


def _mem_attend_kernel(q_ref, k_ref, v_ref, qn_ref, bd_ref, o_ref):
    nb, tq, w = q_ref.shape
    masks = [_head_mask(w, h) for h in range(MEM_HEADS)]
    for b in range(nb):
        qn = _head_rms64(q_ref[b], qn_ref[...], bd_ref[...]) * HEAD64 ** -0.5
        kt = k_ref[b].astype(BF16)
        vt = v_ref[b].astype(BF16)
        qbd = jnp.concatenate([(qn * m).astype(BF16) for m in masks], axis=0)
        s = _dot(qbd, kt)
        p = jnp.exp(s - jnp.max(s, axis=-1, keepdims=True))
        o = _dot_nt(p.astype(BF16), vt) / jnp.sum(p, axis=-1, keepdims=True)
        acc = o[0:tq] * masks[0]
        for h in range(1, MEM_HEADS):
            acc += o[h * tq:(h + 1) * tq] * masks[h]
        o_ref[b] = acc.astype(o_ref.dtype)


def _mem_attend(q, col, mkt, mvt, layer, qn, bd, nb, tq):
    bsz, l, _ = q.shape
    w, nm = mkt.shape[2], mkt.shape[3]
    return pl.pallas_call(
        _mem_attend_kernel,
        grid=(bsz // nb, l // tq),
        in_specs=[
            pl.BlockSpec((nb, tq, w), lambda i, t: (i, t, col)),
            pl.BlockSpec((None, nb, w, nm), lambda i, t: (layer, i, 0, 0)),
            pl.BlockSpec((None, nb, w, nm), lambda i, t: (layer, i, 0, 0)),
            pl.BlockSpec((1, w), lambda i, t: (0, 0)),
            pl.BlockSpec((w, w), lambda i, t: (0, 0)),
        ],
        out_specs=pl.BlockSpec((nb, tq, w), lambda i, t: (i, t, 0)),
        out_shape=jax.ShapeDtypeStruct((bsz, l, w), BF16),
        compiler_params=_params(("parallel", "parallel")),
        name="mem_attend",
    )(q, mkt, mvt, qn, bd)


def _swa_prompt_kernel(q_ref, kc_ref, vc_ref, kp_ref, vp_ref, qn_ref, kn_ref, sink_ref, bd_ref,
                       tok_ref, kout_ref, vout_ref):
    n = pl.program_id(1)
    bd = bd_ref[...]
    w = kc_ref.shape[-1]
    kc = _head_rms64(kc_ref[...], kn_ref[...], bd)
    kp = _head_rms64(kp_ref[...], kn_ref[...], bd)
    vc = vc_ref[...]
    kk = jnp.concatenate([kp, kc], axis=0).astype(BF16)
    vvt = jnp.concatenate([vp_ref[...], vc], axis=0).T.astype(BF16)
    r = lax.broadcasted_iota(jnp.int32, (2 * WINDOW, WINDOW), 0)
    qi = lax.broadcasted_iota(jnp.int32, (2 * WINDOW, WINDOW), 1)
    valid = (r > qi) & (r <= qi + WINDOW) & ((r >= WINDOW) | (n > 0))
    masks = [_head_mask(w, k) for k in range(SW_KV_HEADS)]
    pieces = []
    for g in range(SW_GROUP):
        qg = _head_rms64(q_ref[:, g * w:(g + 1) * w], qn_ref[...], bd) * (HEAD64 ** -0.5 * LOG2E)
        pieces += [(qg * m).astype(BF16) for m in masks]
    st = _dot_nt(kk, jnp.concatenate(pieces, axis=0))
    outs = []
    for h in range(SW_GROUP * SW_KV_HEADS):
        k = h % SW_KV_HEADS
        s = jnp.where(valid, st[:, h * WINDOW:(h + 1) * WINDOW], NEG)
        sink = sink_ref[h] * LOG2E
        m = jnp.maximum(jnp.max(s, axis=0, keepdims=True), sink)
        p = jnp.exp2(s - m)
        den = jnp.sum(p, axis=0, keepdims=True) + jnp.exp2(sink - m)
        outs.append(_dot(vvt[k * HEAD64:(k + 1) * HEAD64, :], p.astype(BF16)) / den)
    tok_ref[...] = jnp.concatenate(outs, axis=0).T.astype(tok_ref.dtype)

    @pl.when(n == pl.num_programs(1) - 1)
    def _():
        kout_ref[...] = kc.T
        vout_ref[...] = vc.T


def _swa_prompt(p, qn, kn, sinks, bd):
    b, l, _ = p.shape
    w = SW_KV_HEADS * HEAD64
    nq = SW_GROUP
    cache = jax.ShapeDtypeStruct((b, w, WINDOW), F32)
    return pl.pallas_call(
        _swa_prompt_kernel,
        grid=(b, l // WINDOW),
        in_specs=[
            pl.BlockSpec((None, WINDOW, nq * w), lambda i, n: (i, n, 0)),
            pl.BlockSpec((None, WINDOW, w), lambda i, n: (i, n, nq)),
            pl.BlockSpec((None, WINDOW, w), lambda i, n: (i, n, nq + 1)),
            pl.BlockSpec((None, WINDOW, w), lambda i, n: (i, jnp.maximum(n - 1, 0), nq)),
            pl.BlockSpec((None, WINDOW, w), lambda i, n: (i, jnp.maximum(n - 1, 0), nq + 1)),
            pl.BlockSpec((1, w), lambda i, n: (0, 0)),
            pl.BlockSpec((1, w), lambda i, n: (0, 0)),
            pl.BlockSpec(memory_space=pltpu.SMEM),
            pl.BlockSpec((w, w), lambda i, n: (0, 0)),
        ],
        out_specs=[
            pl.BlockSpec((None, WINDOW, nq * w), lambda i, n: (i, n, 0)),
            pl.BlockSpec((None, w, WINDOW), lambda i, n: (i, 0, 0)),
            pl.BlockSpec((None, w, WINDOW), lambda i, n: (i, 0, 0)),
        ],
        out_shape=[jax.ShapeDtypeStruct((b, l, nq * w), BF16), cache, cache],
        compiler_params=_params(("parallel", "arbitrary")),
        name="swa_prompt",
    )(p, p, p, p, p, qn, kn, sinks, bd)


def _swa_sample_kernel(q_ref, kn_ref_in, vn_ref_in, ck_ref, cv_ref, qn_ref, kn_ref, sink_ref,
                       bd_ref, *rest, nl, nprev):
    if nprev:
        pk_ref, pv_ref, tok_ref, kstack_ref, vstack_ref = rest
        kstack_ref[0:nprev] = pk_ref[...]
        vstack_ref[0:nprev] = pv_ref[...]
    else:
        tok_ref, kstack_ref, vstack_ref = rest
    kout_ref = kstack_ref.at[nprev]
    vout_ref = vstack_ref.at[nprev]
    nb = q_ref.shape[0]
    w = ck_ref.shape[1]
    rows = SAMPLE_PAD
    nkeys = 2 * WINDOW
    bd = bd_ref[...]
    nh = SW_GROUP * SW_KV_HEADS
    masks = [_head_mask(w, k) for k in range(SW_KV_HEADS)]
    row = lax.broadcasted_iota(jnp.int32, (nh * rows, nkeys), 0)
    r = lax.broadcasted_iota(jnp.int32, (nh * rows, nkeys), 1)
    j = row & (rows - 1)
    valid = (r > j) & (r <= j + WINDOW)
    hrow = lax.broadcasted_iota(jnp.int32, (nh * rows, 1), 0) >> 3
    sink = jnp.zeros((nh * rows, 1), F32)
    for h in range(nh):
        sink = jnp.where(hrow == h, sink_ref[h], sink)
    zpad = jnp.zeros((WINDOW - rows, w), F32)
    for b in range(nb):
        knew = _head_rms64(kn_ref_in[b], kn_ref[...], bd)
        kt = jnp.concatenate([ck_ref[b], jnp.concatenate([knew, zpad], axis=0).T], axis=1)
        vt = jnp.concatenate([cv_ref[b], jnp.concatenate([vn_ref_in[b], zpad], axis=0).T], axis=1)
        pieces = []
        for g in range(SW_GROUP):
            qg = _head_rms64(q_ref[b, :, g * w:(g + 1) * w], qn_ref[...], bd) * HEAD64 ** -0.5
            pieces += [(qg * masks[k]).astype(BF16) for k in range(SW_KV_HEADS)]
        s = _dot(jnp.concatenate(pieces, axis=0), kt.astype(BF16))
        s = jnp.where(valid, s, NEG)
        m = jnp.maximum(jnp.max(s, axis=-1, keepdims=True), sink)
        p = jnp.exp(s - m)
        den = jnp.sum(p, axis=-1, keepdims=True) + jnp.exp(sink - m)
        o = _dot_nt(p.astype(BF16), vt.astype(BF16)) / den
        for g in range(SW_GROUP):
            acc = jnp.zeros((rows, w), F32)
            for k in range(SW_KV_HEADS):
                h = g * SW_KV_HEADS + k
                acc += o[h * rows:(h + 1) * rows] * masks[k]
            tok_ref[b, :, g * w:(g + 1) * w] = acc.astype(tok_ref.dtype)
        kout_ref[b] = kt[:, nl:nl + WINDOW]
        vout_ref[b] = vt[:, nl:nl + WINDOW]


def _swa_sample(p, ckt, cvt, layer, qn, kn, sinks, bd, nl, nb, prev):
    b = p.shape[0]
    w = ckt.shape[2]
    nq = SW_GROUP
    nprev = 0 if prev is None else prev[0].shape[0]
    cache = jax.ShapeDtypeStruct((nprev + 1, b, w, WINDOW), F32)
    c3 = lambda i: (i, 0, 0)
    c4 = lambda i: (layer, i, 0, 0)
    stack = lambda n: pl.BlockSpec((n, nb, w, WINDOW), lambda i: (0, i, 0, 0))
    return pl.pallas_call(
        functools.partial(_swa_sample_kernel, nl=nl, nprev=nprev),
        grid=(b // nb,),
        in_specs=[
            pl.BlockSpec((nb, SAMPLE_PAD, nq * w), c3),
            pl.BlockSpec((nb, SAMPLE_PAD, w), lambda i: (i, 0, nq)),
            pl.BlockSpec((nb, SAMPLE_PAD, w), lambda i: (i, 0, nq + 1)),
            pl.BlockSpec((None, nb, w, WINDOW), c4),
            pl.BlockSpec((None, nb, w, WINDOW), c4),
            pl.BlockSpec((1, w), lambda i: (0, 0)),
            pl.BlockSpec((1, w), lambda i: (0, 0)),
            pl.BlockSpec(memory_space=pltpu.SMEM),
            pl.BlockSpec((w, w), lambda i: (0, 0)),
        ] + ([stack(nprev)] * 2 if nprev else []),
        out_specs=[
            pl.BlockSpec((nb, SAMPLE_PAD, nq * w), c3),
            stack(nprev + 1),
            stack(nprev + 1),
        ],
        out_shape=[jax.ShapeDtypeStruct((b, SAMPLE_PAD, nq * w), BF16), cache, cache],
        compiler_params=_params(("parallel",)),
        name="swa_sample",
    )(p, p, p, ckt, cvt, qn, kn, sinks, bd, *(() if prev is None else prev))


def _hgrn_consts(rows, group):
    t = np.arange(rows)
    loc, run = t % group, t // group
    same = run[:, None] == run[None, :]
    r = t[None, :]
    mats = [same & (r <= t[:, None]), same & (r > t[:, None])]
    masks = []
    h = group // 2
    while h >= 1:
        par = loc // (2 * h)
        right = (loc % (2 * h)) >= h
        bnd = (run * group + par * 2 * h + h - 1)[:, None]
        q_side = right[:, None] & (r > bnd) & (r <= t[:, None])
        k_side = (~right)[:, None] & (r > t[:, None]) & (r <= bnd)
        mats.append(q_side | k_side)
        masks.append(same & (par[:, None] == par[None, :]) & right[:, None] & (~right)[None, :])
        h //= 2
    masks.append(np.eye(rows, dtype=bool))
    w = np.concatenate(mats, axis=0).astype(np.float32)
    w2 = np.concatenate([w, w], axis=1)
    return jnp.asarray(w2, BF16), jnp.asarray(np.stack(masks).astype(np.float32)), len(masks) - 1


def _hgrn_gates(q_raw, z, lb):
    sp, sn = _sigmoid_pair(z)
    log2f = jnp.log2(lb + (1.0 - lb) * sp)
    kf = (1.0 - lb) * sn
    q = q_raw * _sigmoid(q_raw) * HG_HEAD ** -0.5
    return q, kf, log2f


def _hgrn_decays(log2f, w2_ref):
    hi = log2f.astype(BF16)
    lo = (log2f - hi.astype(F32)).astype(BF16)
    return jnp.exp2(_dot(w2_ref[...], jnp.concatenate([hi, lo], axis=0)))


def _hgrn_intra(q, kf, v, e, m_ref, nlev):
    rows = q.shape[0]
    att = m_ref[nlev] * _dot_nt(q.astype(BF16), kf.astype(BF16))
    for l in range(nlev):
        el = e[(2 + l) * rows:(3 + l) * rows]
        att += m_ref[l] * _dot_nt((q * el).astype(BF16), (kf * el).astype(BF16))
    return _dot(att.astype(BF16), v.astype(BF16))


def _hgrn_out(o, g, gn):
    ms = jnp.mean(o * o, axis=-1, keepdims=True)
    return (o * lax.rsqrt(ms + EPS) * gn * (g * _sigmoid(g))).astype(BF16)


def _hgrn_prompt_kernel(q_ref, f_ref, i_ref, g_ref, lb_ref, gn_ref, w2_ref, m_ref,
                        tok_ref, st_ref, s_scr, *, nlev):
    t = pl.program_id(1)

    @pl.when(t == 0)
    def _():
        s_scr[...] = jnp.zeros_like(s_scr)

    lb = lb_ref[...]
    gn = gn_ref[...]
    rows = HG_ROWS
    nh = q_ref.shape[1] // HG_HEAD

    def chunk(c, st):
        sl = pl.ds(pl.multiple_of(c * rows, rows), rows)
        v = i_ref[sl, :]
        g = g_ref[sl, :]
        q, kf, log2f = _hgrn_gates(q_ref[sl, :], f_ref[sl, :], lb)
        e = _hgrn_decays(log2f, w2_ref)
        qt = (q * e[0:rows]).astype(BF16)
        kh = (kf * e[rows:2 * rows]).astype(BF16)
        vb = v.astype(BF16)
        stb = st.astype(BF16)
        qb = q.astype(BF16)
        kb = kf.astype(BF16)
        ql, kl = [qb], [kb]
        rowi = lax.broadcasted_iota(jnp.int32, (rows, 1), 0)
        for l in range(nlev):
            half = rows >> (l + 1)
            if half >= 8:
                base = jnp.concatenate([(q if (r0 // half) & 1 else kf)[r0:r0 + half]
                                        for r0 in range(0, rows, half)], axis=0)
            else:
                base = jnp.where((rowi & (2 * half - 1)) >= half, q, kf)
            x = (base * e[(2 + l) * rows:(3 + l) * rows]).astype(BF16)
            ql.append(x)
            kl.append(x)
        mk = [m_ref[nlev]] + [m_ref[l] for l in range(nlev)]
        zr = jnp.zeros((rows, HG_HEAD), BF16)
        zs = jnp.zeros((HG_HEAD, HG_HEAD), BF16)

        def bdiag(a, b, z):
            return jnp.concatenate([jnp.concatenate([a, z], axis=1),
                                    jnp.concatenate([z, b], axis=1)], axis=0)

        upd = []
        for p in range(nh // 2):
            h0 = slice(2 * p * HG_HEAD, (2 * p + 1) * HG_HEAD)
            h1 = slice((2 * p + 1) * HG_HEAD, (2 * p + 2) * HG_HEAD)
            pr = slice(2 * p * HG_HEAD, (2 * p + 2) * HG_HEAD)
            att = jnp.zeros((rows, 2 * rows), F32)
            for a, b, m in zip(ql, kl, mk):
                att += m * _dot_nt(a[:, pr], bdiag(b[:, h0], b[:, h1], zr))
            o = _dot(att.astype(BF16), bdiag(vb[:, h0], vb[:, h1], zr))
            o += _dot_nt(qt[:, pr], bdiag(stb[:, h0], stb[:, h1], zs))
            for hs in (h0, h1):
                lo = hs.start - pr.start
                tok_ref[sl, hs] = _hgrn_out(o[:, lo:lo + HG_HEAD], g[:, hs], gn[:, hs])
                upd.append(_dot_tn(vb[:, hs], kh[:, hs]))
        return st * e[rows - 1:rows, :] + jnp.concatenate(upd, axis=1)

    st = lax.fori_loop(0, q_ref.shape[0] // rows, chunk, s_scr[...], unroll=True)
    s_scr[...] = st

    @pl.when(t == pl.num_programs(1) - 1)
    def _():
        for h in range(nh):
            st_ref[h] = st[:, h * HG_HEAD:(h + 1) * HG_HEAD].T


def _hgrn_prompt(p, lb, gn, tb):
    b, l, _ = p.shape
    tw = lb.shape[1]
    nh = tw // HG_HEAD
    w2, masks, nlev = _hgrn_consts(HG_ROWS, HG_ROWS)
    masks = jnp.tile(masks, (1, 1, 2))
    sec = lambda s: pl.BlockSpec((None, tb, tw), lambda i, t: (i, t, s))
    return pl.pallas_call(
        functools.partial(_hgrn_prompt_kernel, nlev=nlev),
        grid=(b, l // tb),
        in_specs=[
            sec(0), sec(1), sec(2), sec(3),
            pl.BlockSpec((1, tw), lambda i, t: (0, 0)),
            pl.BlockSpec((1, tw), lambda i, t: (0, 0)),
            pl.BlockSpec(w2.shape, lambda i, t: (0, 0)),
            pl.BlockSpec(masks.shape, lambda i, t: (0, 0, 0)),
        ],
        out_specs=[
            pl.BlockSpec((None, tb, tw), lambda i, t: (i, t, 0)),
            pl.BlockSpec((None, nh, HG_HEAD, HG_HEAD), lambda i, t: (i, 0, 0, 0)),
        ],
        out_shape=[jax.ShapeDtypeStruct((b, l, tw), BF16),
                   jax.ShapeDtypeStruct((b, nh, HG_HEAD, HG_HEAD), F32)],
        scratch_shapes=[pltpu.VMEM((HG_HEAD, tw), F32)],
        compiler_params=_params(("parallel", "arbitrary")),
        name="hgrn_prompt",
    )(p, p, p, p, lb, gn, w2, masks)


def _hgrn_sample_kernel(q_ref, f_ref, i_ref, g_ref, s0_ref, lb_ref, gn_ref, w2_ref, m_ref,
                        *rest, nlev, nl, nprev):
    if nprev:
        prev_ref, tok_ref, stack_ref = rest
        stack_ref[0:nprev] = prev_ref[...]
    else:
        tok_ref, stack_ref = rest
    st_ref = stack_ref.at[nprev]
    rows = q_ref.shape[0]
    nb = rows // nl
    v = i_ref[...]
    q, kf, log2f = _hgrn_gates(q_ref[...], f_ref[...], lb_ref[...])
    e = _hgrn_decays(log2f, w2_ref)
    o_intra = _hgrn_intra(q, kf, v, e, m_ref, nlev)
    e_cum = e[0:rows]
    qt = q * e_cum
    kh = kf * e[rows:2 * rows]
    shift = nl.bit_length() - 1
    rowb = lax.broadcasted_iota(jnp.int32, (rows, 1), 0) >> shift
    s0cat = jnp.concatenate([s0_ref[b].astype(BF16) for b in range(nb)], axis=1)
    full = _dot(qt.astype(BF16), s0cat)
    o = o_intra
    for b in range(nb):
        o += jnp.where(rowb == b, full[:, b * HG_HEAD:(b + 1) * HG_HEAD], 0.0)
    tok_ref[...] = _hgrn_out(o, g_ref[...], gn_ref[...])
    xt = jnp.concatenate([kh, e_cum], axis=0).T
    colb = lax.broadcasted_iota(jnp.int32, (rows, nb * HG_HEAD), 1) >> 7
    vbd = jnp.where(rowb == colb, jnp.concatenate([v] * nb, axis=1), 0.0).astype(BF16)
    upd = _dot(xt[:, 0:rows].astype(BF16), vbd)
    for b in range(nb):
        last = rows + b * nl + nl - 1
        st_ref[b] = s0_ref[b] * xt[:, last:last + 1] + upd[:, b * HG_HEAD:(b + 1) * HG_HEAD]


def _hgrn_sample(p, s0, layer, lb, gn, nl, prev):
    rows = HG_ROWS
    nb = rows // nl
    b, nh = s0.shape[1], s0.shape[2]
    nprev = 0 if prev is None else prev.shape[0]
    w2, masks, nlev = _hgrn_consts(rows, nl)
    sec = lambda s: pl.BlockSpec((rows, HG_HEAD), lambda i, h: (i, s * nh + h))
    stack = lambda n: pl.BlockSpec((n, nb, None, HG_HEAD, HG_HEAD), lambda i, h: (0, i, h, 0, 0))
    s0_spec = pl.BlockSpec((None, nb, None, HG_HEAD, HG_HEAD), lambda i, h: (layer, i, h, 0, 0))
    return pl.pallas_call(
        functools.partial(_hgrn_sample_kernel, nlev=nlev, nl=nl, nprev=nprev),
        grid=(b // nb, nh),
        in_specs=[
            sec(0), sec(1), sec(2), sec(3), s0_spec,
            pl.BlockSpec((1, HG_HEAD), lambda i, h: (0, h)),
            pl.BlockSpec((1, HG_HEAD), lambda i, h: (0, h)),
            pl.BlockSpec(w2.shape, lambda i, h: (0, 0)),
            pl.BlockSpec(masks.shape, lambda i, h: (0, 0, 0)),
        ] + ([stack(nprev)] if nprev else []),
        out_specs=[pl.BlockSpec((rows, HG_HEAD), lambda i, h: (i, h)), stack(nprev + 1)],
        out_shape=[jax.ShapeDtypeStruct((b * nl, nh * HG_HEAD), BF16),
                   jax.ShapeDtypeStruct((nprev + 1,) + s0.shape[1:], F32)],
        compiler_params=_params(("parallel", "parallel")),
        name="hgrn_sample",
    )(p, p, p, p, s0, lb, gn, w2, masks, *(() if prev is None else (prev,)))


def _swa_head_perm():
    idx = [(k * SW_GROUP + g) * HEAD64 + d
           for g in range(SW_GROUP) for k in range(SW_KV_HEADS) for d in range(HEAD64)]
    return np.asarray(idx, np.int32)


def kernel(x_prompt, x_sample, cache_mem_k, cache_mem_v, state_hgrn, cache_swa_k, cache_swa_v,
           mem_prompt, norm_mix, norm_ffn, norm_mem, w_in_hgrn, hgrn_lb_raw, hgrn_out_norm,
           w_in_swa, swa_q_norm, swa_k_norm, swa_sinks, w_mem_kv, mem_q_norm, mem_k_norm,
           w_out, w_ffn_in, w_ffn_out):
    depth, d = norm_mix.shape
    bp, lp, _ = x_prompt.shape
    bs, ls, _ = x_sample.shape
    nm = mem_prompt.shape[1]
    mw = MEM_HEADS * HEAD64
    kvw = SW_KV_HEADS * HEAD64
    tokw = d - mw
    tm = 512

    seg = np.arange(mw) // HEAD64
    bd = jnp.asarray((seg[:, None] == seg[None, :]).astype(np.float32) / HEAD64, BF16)
    tile4 = lambda a: jnp.tile(a.astype(F32), (1, mw // HEAD64))[:, None, :]

    sm = jax.nn.softmax(hgrn_lb_raw.astype(F32), axis=0)
    lbs = jnp.clip(jnp.cumsum(sm, axis=0) - sm[0], 0.0, LB_MAX)

    perm = _swa_head_perm()
    w_in_swa_p = jnp.concatenate([w_in_swa[:, :, perm], w_in_swa[:, :, tokw:]], axis=2).astype(BF16)
    w_in_hgrn_b = w_in_hgrn.astype(BF16)
    w_out_b = jnp.stack([w_out[i] if i % 2 == 0 else
                         jnp.concatenate([w_out[i][perm], w_out[i][tokw:]], axis=0)
                         for i in range(depth)]).astype(BF16)
    w1_b = w_ffn_in.astype(BF16)
    w2_b = w_ffn_out.astype(BF16)
    sinks_p = swa_sinks.astype(F32).reshape(-1, SW_KV_HEADS, SW_GROUP).transpose(0, 2, 1)
    sinks_p = sinks_p.reshape(-1, SW_KV_HEADS * SW_GROUP)
    mem_qn = tile4(mem_q_norm)
    swa_qn = tile4(swa_q_norm)
    swa_kn = tile4(swa_k_norm)

    mk_p, mv_p = _mem_kv(mem_prompt, norm_mem[:, None, :], w_mem_kv.astype(BF16),
                         tile4(mem_k_norm), bd)

    to_t = lambda a: a.transpose(0, 1, 3, 4, 2).reshape(a.shape[0], a.shape[1], -1, a.shape[2])
    from_t = lambda a, heads: a.reshape(a.shape[:-2] + (heads, HEAD64, a.shape[-1])).transpose(
        *range(a.ndim - 2), a.ndim, a.ndim - 2, a.ndim - 1)
    cmk_t, cmv_t = to_t(cache_mem_k), to_t(cache_mem_v)
    csk_t, csv_t = to_t(cache_swa_k), to_t(cache_swa_v)

    xp = x_prompt.reshape(bp * lp, d)
    xs = x_sample.reshape(bs * ls, d)
    pad_rows = lambda a: jnp.pad(a.reshape(bs, ls, a.shape[-1]),
                                 ((0, 0), (0, SAMPLE_PAD - ls), (0, 0)))
    g_mix = norm_mix[:, None, :]
    g_ffn = norm_ffn[:, None, :]
    hg_p, swk_p, swv_p = [], [], []
    hg_s = sw_s = None
    for i in range(depth):
        j = i // 2
        if i % 2 == 0:
            lb = lbs[j][None, :]
            gn = hgrn_out_norm[j][None, :].astype(F32)
            pp = _norm_matmul(xp, g_mix, i, w_in_hgrn_b, j, tm).reshape(bp, lp, -1)
            tok_p, st_p = _hgrn_prompt(pp, lb, gn, 512)
            ps = _norm_matmul(xs, g_mix, i, w_in_hgrn_b, j, tm)
            tok_s, hg_s = _hgrn_sample(ps, state_hgrn, j, lb, gn, ls, hg_s)
            hg_p.append(st_p)
            cq_s = pad_rows(ps[:, 4 * tokw:])
            cq_col = 4 * tokw // mw
        else:
            pp = _norm_matmul(xp, g_mix, i, w_in_swa_p, j, tm).reshape(bp, lp, -1)
            tok_p, kb, vb = _swa_prompt(pp, swa_qn[j], swa_kn[j], sinks_p[j], bd)
            swk_p.append(kb)
            swv_p.append(vb)
            ps = pad_rows(_norm_matmul(xs, g_mix, i, w_in_swa_p, j, tm))
            tok_s, *sw_s = _swa_sample(ps, csk_t, csv_t, j, swa_qn[j], swa_kn[j], sinks_p[j],
                                       bd, ls, 8, sw_s)
            tok_s = tok_s[:, :ls].reshape(bs * ls, tokw)
            cq_s = ps[:, :, tokw + 2 * kvw:]
            cq_col = (tokw + 2 * kvw) // mw
        mo_p = _mem_attend(pp, cq_col, mk_p, mv_p, i, mem_qn[i], bd, 1, 512)
        mo_s = _mem_attend(cq_s, 0, cmk_t, cmv_t, i, mem_qn[i], bd, 8, SAMPLE_PAD)
        mo_s = mo_s[:, :ls].reshape(bs * ls, mw)
        xp = _mix_ffn(xp, tok_p.reshape(bp * lp, tokw), mo_p.reshape(bp * lp, mw),
                      w_out_b, g_ffn, w1_b, w2_b, i, tm)
        xs = _mix_ffn(xs, tok_s, mo_s, w_out_b, g_ffn, w1_b, w2_b, i, tm)

    return (xp.reshape(bp, lp, d), xs.reshape(bs, ls, d),
            from_t(mk_p, MEM_HEADS), from_t(mv_p, MEM_HEADS),
            jnp.stack(hg_p), hg_s,
            from_t(jnp.stack(swk_p), SW_KV_HEADS), from_t(jnp.stack(swv_p), SW_KV_HEADS),
            from_t(sw_s[0], SW_KV_HEADS), from_t(sw_s[1], SW_KV_HEADS))
```

```python
import functools

import numpy as np
import jax
import jax.numpy as jnp
from jax import lax
from jax.experimental import pallas as pl
from jax.experimental.pallas import tpu as pltpu

F32 = jnp.float32
BF16 = jnp.bfloat16
EPS = 1e-6
NEG = -1e30
LB_MAX = 0.999
LOG2E = 1.4426950408889634

HEAD64 = 64
MEM_HEADS = 4
SW_KV_HEADS = 4
SW_GROUP = 3
WINDOW = 128
HG_HEAD = 128
HG_ROWS = 64
SAMPLE_PAD = 8
LANES = 128

V7X_VMEM_BYTES = 64 * 1024 * 1024
VMEM_LIMIT = V7X_VMEM_BYTES - 8 * 1024 * 1024


def _params(sem):
    return pltpu.CompilerParams(dimension_semantics=sem, vmem_limit_bytes=VMEM_LIMIT)


def _dot(a, b):
    return jnp.dot(a, b, preferred_element_type=F32)


def _dot_nt(a, b):
    return lax.dot_general(a, b, (((1,), (1,)), ((), ())), preferred_element_type=F32)


def _dot_tn(a, b):
    return lax.dot_general(a, b, (((0,), (0,)), ((), ())), preferred_element_type=F32)


def _sigmoid(x):
    return 1.0 / (1.0 + jnp.exp(-x))


def _sigmoid_pair(z):
    e = jnp.exp(-jnp.abs(z))
    r = 1.0 / (1.0 + e)
    er = e * r
    pos = z >= 0
    return jnp.where(pos, r, er), jnp.where(pos, er, r)


def _rms_rows(x, g):
    ms = jnp.mean(x * x, axis=-1, keepdims=True)
    return x * lax.rsqrt(ms + EPS) * g


def _head_rms64(x, g, bd):
    ms = _dot((x * x).astype(BF16), bd)
    return x * lax.rsqrt(ms + EPS) * g


def _head_mask(width, h):
    lane = lax.broadcasted_iota(jnp.int32, (1, width), 1)
    return ((lane >> 6) == h).astype(F32)


def _norm_matmul_kernel(x_ref, g_ref, w_ref, o_ref):
    h = _rms_rows(x_ref[...], g_ref[...]).astype(BF16)
    o_ref[...] = _dot(h, w_ref[...])


def _norm_matmul(x, g, g_idx, w, w_idx, tm):
    m, d = x.shape
    n = w.shape[2]
    tm = min(tm, m)
    return pl.pallas_call(
        _norm_matmul_kernel,
        grid=(m // tm,),
        in_specs=[
            pl.BlockSpec((tm, d), lambda i: (i, 0)),
            pl.BlockSpec((None, 1, d), lambda i: (g_idx, 0, 0)),
            pl.BlockSpec((None, d, n), lambda i: (w_idx, 0, 0)),
        ],
        out_specs=pl.BlockSpec((tm, n), lambda i: (i, 0)),
        out_shape=jax.ShapeDtypeStruct((m, n), F32),
        compiler_params=_params(("parallel",)),
        name="norm_matmul",
    )(x, g, w)


def _mix_ffn_kernel(x_ref, tok_ref, mo_ref, wo_ref, g_ref, w1_ref, w2_ref, o_ref, act_ref,
                    *, ff, chunk):
    tw = tok_ref.shape[1]
    x1 = (x_ref[...] + _dot(tok_ref[...], wo_ref[0:tw, :])
          + _dot(mo_ref[...], wo_ref[tw:, :]))
    o_ref[...] = x1
    h = _rms_rows(x1, g_ref[...]).astype(BF16)
    for c0 in range(0, ff, chunk):
        g = _dot(h, w1_ref[:, c0:c0 + chunk])
        u = _dot(h, w1_ref[:, ff + c0:ff + c0 + chunk])
        act_ref[:, c0:c0 + chunk] = (g * _sigmoid(g) * u).astype(BF16)
    o_ref[...] += _dot(act_ref[...], w2_ref[...])


def _mix_ffn(x, tok, mo, wo, g, w1, w2, layer, tm):
    m, d = x.shape
    ff = w2.shape[1]
    tm = min(tm, m)
    tw, mw = tok.shape[1], mo.shape[1]
    const = lambda i: (layer, 0, 0)
    return pl.pallas_call(
        functools.partial(_mix_ffn_kernel, ff=ff, chunk=256),
        grid=(m // tm,),
        in_specs=[
            pl.BlockSpec((tm, d), lambda i: (i, 0)),
            pl.BlockSpec((tm, tw), lambda i: (i, 0)),
            pl.BlockSpec((tm, mw), lambda i: (i, 0)),
            pl.BlockSpec((None, tw + mw, d), const, pipeline_mode=pl.Buffered(1)),
            pl.BlockSpec((None, 1, d), const),
            pl.BlockSpec((None, d, 2 * ff), const, pipeline_mode=pl.Buffered(1)),
            pl.BlockSpec((None, ff, d), const, pipeline_mode=pl.Buffered(1)),
        ],
        out_specs=pl.BlockSpec((tm, d), lambda i: (i, 0)),
        out_shape=jax.ShapeDtypeStruct((m, d), F32),
        scratch_shapes=[pltpu.VMEM((tm, ff), BF16)],
        compiler_params=_params(("parallel",)),
        name="mix_ffn",
    )(x, tok, mo, wo, g, w1, w2)


def _mem_kv_kernel(mem_ref, g_ref, w_ref, kn_ref, bd_ref, k_ref, v_ref, kb_ref):
    h = _rms_rows(mem_ref[...], g_ref[...]).astype(BF16)
    kv = _dot(h, w_ref[...])
    kw = k_ref.shape[0]
    k = _head_rms64(kv[:, :kw], kn_ref[...], bd_ref[...])
    k_ref[...] = k.T
    v_ref[...] = kv[:, kw:].T
    kb_ref[...] = jnp.concatenate([(k * _head_mask(kw, hd)).astype(BF16)
                                   for hd in range(MEM_HEADS)], axis=0)


def _mem_kv(mem, g, w, kn, bd):
    depth = w.shape[0]
    b, nm, d = mem.shape
    kw = w.shape[2] // 2
    out = jax.ShapeDtypeStruct((depth, b, kw, nm), F32)
    out_b = jax.ShapeDtypeStruct((depth, b, MEM_HEADS * nm, kw), BF16)
    return pl.pallas_call(
        _mem_kv_kernel,
        grid=(depth, b),
        in_specs=[
            pl.BlockSpec((None, nm, d), lambda i, j: (j, 0, 0)),
            pl.BlockSpec((None, 1, d), lambda i, j: (i, 0, 0)),
            pl.BlockSpec((None, d, 2 * kw), lambda i, j: (i, 0, 0)),
            pl.BlockSpec((None, 1, kw), lambda i, j: (i, 0, 0)),
            pl.BlockSpec((kw, kw), lambda i, j: (0, 0)),
        ],
        out_specs=[pl.BlockSpec((None, None, kw, nm), lambda i, j: (i, j, 0, 0))] * 2
        + [pl.BlockSpec((None, None, MEM_HEADS * nm, kw), lambda i, j: (i, j, 0, 0))],
        out_shape=[out, out, out_b],
        compiler_params=_params(("parallel", "parallel")),
        name="mem_kv",
    )(mem, g, w, kn, bd)


def _mem_attend_kernel(q_ref, k_ref, v_ref, qn_ref, bd_ref, o_ref):
    nb, tq, w = q_ref.shape
    masks = [_head_mask(w, h) for h in range(MEM_HEADS)]
    for b in range(nb):
        qn = _head_rms64(q_ref[b], qn_ref[...], bd_ref[...]) * HEAD64 ** -0.5
        kt = k_ref[b].astype(BF16)
        vt = v_ref[b].astype(BF16)
        qbd = jnp.concatenate([(qn * m).astype(BF16) for m in masks], axis=0)
        s = _dot(qbd, kt)
        p = jnp.exp(s - jnp.max(s, axis=-1, keepdims=True))
        o = _dot_nt(p.astype(BF16), vt) / jnp.sum(p, axis=-1, keepdims=True)
        acc = o[0:tq] * masks[0]
        for h in range(1, MEM_HEADS):
            acc += o[h * tq:(h + 1) * tq] * masks[h]
        o_ref[b] = acc.astype(o_ref.dtype)


def _mem_attend_prompt_kernel(q_ref, kb_ref, vt_ref, qn_ref, bd_ref, o_ref):
    tq, w = q_ref.shape
    qn = _head_rms64(q_ref[...], qn_ref[...], bd_ref[...]) * (HEAD64 ** -0.5 * LOG2E)
    nm = vt_ref.shape[1]
    st = _dot_nt(kb_ref[...], qn.astype(BF16))
    vt = vt_ref[...].astype(BF16)
    outs = []
    for h in range(MEM_HEADS):
        s = st[h * nm:(h + 1) * nm]
        p = jnp.exp2(s - jnp.max(s, axis=0, keepdims=True))
        den = jnp.sum(p, axis=0, keepdims=True)
        outs.append(_dot(vt[h * HEAD64:(h + 1) * HEAD64, :], p.astype(BF16)) / den)
    o_ref[...] = jnp.concatenate(outs, axis=0).T.astype(o_ref.dtype)


def _mem_attend_prompt(q, col, mkb, mvt, layer, qn, bd, tq):
    bsz, l, _ = q.shape
    w, nm = mvt.shape[2], mvt.shape[3]
    return pl.pallas_call(
        _mem_attend_prompt_kernel,
        grid=(bsz, l // tq),
        in_specs=[
            pl.BlockSpec((None, tq, w), lambda i, t: (i, t, col)),
            pl.BlockSpec((None, None, MEM_HEADS * nm, w), lambda i, t: (layer, i, 0, 0)),
            pl.BlockSpec((None, None, w, nm), lambda i, t: (layer, i, 0, 0)),
            pl.BlockSpec((1, w), lambda i, t: (0, 0)),
            pl.BlockSpec((w, w), lambda i, t: (0, 0)),
        ],
        out_specs=pl.BlockSpec((None, tq, w), lambda i, t: (i, t, 0)),
        out_shape=jax.ShapeDtypeStruct((bsz, l, w), BF16),
        compiler_params=_params(("parallel", "parallel")),
        name="mem_attend_prompt",
    )(q, mkb, mvt, qn, bd)


def _mem_attend(q, col, mkt, mvt, layer, qn, bd, nb, tq):
    bsz, l, _ = q.shape
    w, nm = mkt.shape[2], mkt.shape[3]
    return pl.pallas_call(
        _mem_attend_kernel,
        grid=(bsz // nb, l // tq),
        in_specs=[
            pl.BlockSpec((nb, tq, w), lambda i, t: (i, t, col)),
            pl.BlockSpec((None, nb, w, nm), lambda i, t: (layer, i, 0, 0)),
            pl.BlockSpec((None, nb, w, nm), lambda i, t: (layer, i, 0, 0)),
            pl.BlockSpec((1, w), lambda i, t: (0, 0)),
            pl.BlockSpec((w, w), lambda i, t: (0, 0)),
        ],
        out_specs=pl.BlockSpec((nb, tq, w), lambda i, t: (i, t, 0)),
        out_shape=jax.ShapeDtypeStruct((bsz, l, w), BF16),
        compiler_params=_params(("parallel", "parallel")),
        name="mem_attend",
    )(q, mkt, mvt, qn, bd)


def _swa_prompt_kernel(q_ref, kc_ref, vc_ref, kp_ref, vp_ref, qn_ref, kn_ref, sink_ref, bd_ref,
                       tok_ref, kout_ref, vout_ref):
    n = pl.program_id(1)
    bd = bd_ref[...]
    w = kc_ref.shape[-1]
    nwin = kc_ref.shape[0] // WINDOW
    kk = _head_rms64(jnp.concatenate([kp_ref[...], kc_ref[...]], axis=0), kn_ref[...], bd)
    vv = jnp.concatenate([vp_ref[...], vc_ref[...]], axis=0)
    kb = kk.astype(BF16)
    vvt = vv.T.astype(BF16)
    r = lax.broadcasted_iota(jnp.int32, (2 * WINDOW, WINDOW), 0)
    qi = lax.broadcasted_iota(jnp.int32, (2 * WINDOW, WINDOW), 1)
    band = (r > qi) & (r <= qi + WINDOW)
    masks = [_head_mask(w, k) for k in range(SW_KV_HEADS)]
    for i in range(nwin):
        rows = slice(i * WINDOW, (i + 1) * WINDOW)
        keys = slice(i * WINDOW, (i + 2) * WINDOW)
        valid = band if i else band & ((r >= WINDOW) | (n > 0))
        pieces = []
        for g in range(SW_GROUP):
            qg = _head_rms64(q_ref[rows, g * w:(g + 1) * w], qn_ref[...], bd)
            qg = qg * (HEAD64 ** -0.5 * LOG2E)
            pieces += [(qg * m).astype(BF16) for m in masks]
        st = _dot_nt(kb[keys], jnp.concatenate(pieces, axis=0))
        outs = []
        for h in range(SW_GROUP * SW_KV_HEADS):
            k = h % SW_KV_HEADS
            s = jnp.where(valid, st[:, h * WINDOW:(h + 1) * WINDOW], NEG)
            sink = sink_ref[h] * LOG2E
            m = jnp.maximum(jnp.max(s, axis=0, keepdims=True), sink)
            p = jnp.exp2(s - m)
            den = jnp.sum(p, axis=0, keepdims=True) + jnp.exp2(sink - m)
            outs.append(_dot(vvt[k * HEAD64:(k + 1) * HEAD64, keys], p.astype(BF16)) / den)
        tok_ref[rows, :] = jnp.concatenate(outs, axis=0).T.astype(tok_ref.dtype)

    @pl.when(n == pl.num_programs(1) - 1)
    def _():
        kout_ref[...] = kk[nwin * WINDOW:].T
        vout_ref[...] = vv[nwin * WINDOW:].T


def _swa_prompt(p, qn, kn, sinks, bd, nwin):
    b, l, _ = p.shape
    w = SW_KV_HEADS * HEAD64
    nq = SW_GROUP
    tq = nwin * WINDOW
    cache = jax.ShapeDtypeStruct((b, w, WINDOW), F32)
    prev = lambda c: (lambda i, n: (i, jnp.maximum(n * nwin - 1, 0), c))
    return pl.pallas_call(
        _swa_prompt_kernel,
        grid=(b, l // tq),
        in_specs=[
            pl.BlockSpec((None, tq, nq * w), lambda i, n: (i, n, 0)),
            pl.BlockSpec((None, tq, w), lambda i, n: (i, n, nq)),
            pl.BlockSpec((None, tq, w), lambda i, n: (i, n, nq + 1)),
            pl.BlockSpec((None, WINDOW, w), prev(nq)),
            pl.BlockSpec((None, WINDOW, w), prev(nq + 1)),
            pl.BlockSpec((1, w), lambda i, n: (0, 0)),
            pl.BlockSpec((1, w), lambda i, n: (0, 0)),
            pl.BlockSpec(memory_space=pltpu.SMEM),
            pl.BlockSpec((w, w), lambda i, n: (0, 0)),
        ],
        out_specs=[
            pl.BlockSpec((None, tq, nq * w), lambda i, n: (i, n, 0)),
            pl.BlockSpec((None, w, WINDOW), lambda i, n: (i, 0, 0)),
            pl.BlockSpec((None, w, WINDOW), lambda i, n: (i, 0, 0)),
        ],
        out_shape=[jax.ShapeDtypeStruct((b, l, nq * w), BF16), cache, cache],
        compiler_params=_params(("parallel", "arbitrary")),
        name="swa_prompt",
    )(p, p, p, p, p, qn, kn, sinks, bd)


def _swa_sample_kernel(q_ref, kn_ref_in, vn_ref_in, ck_ref, cv_ref, qn_ref, kn_ref, sink_ref,
                       bd_ref, *rest, nl, nprev):
    if nprev:
        pk_ref, pv_ref, tok_ref, kstack_ref, vstack_ref = rest
        kstack_ref[0:nprev] = pk_ref[...]
        vstack_ref[0:nprev] = pv_ref[...]
    else:
        tok_ref, kstack_ref, vstack_ref = rest
    kout_ref = kstack_ref.at[nprev]
    vout_ref = vstack_ref.at[nprev]
    nb = q_ref.shape[0]
    w = ck_ref.shape[1]
    rows = SAMPLE_PAD
    nkeys = 2 * WINDOW
    bd = bd_ref[...]
    nh = SW_GROUP * SW_KV_HEADS
    masks = [_head_mask(w, k) for k in range(SW_KV_HEADS)]
    row = lax.broadcasted_iota(jnp.int32, (nh * rows, nkeys), 0)
    r = lax.broadcasted_iota(jnp.int32, (nh * rows, nkeys), 1)
    j = row & (rows - 1)
    valid = (r > j) & (r <= j + WINDOW)
    hrow = lax.broadcasted_iota(jnp.int32, (nh * rows, 1), 0) >> 3
    sink = jnp.zeros((nh * rows, 1), F32)
    for h in range(nh):
        sink = jnp.where(hrow == h, sink_ref[h], sink)
    zpad = jnp.zeros((WINDOW - rows, w), F32)
    for b in range(nb):
        knew = _head_rms64(kn_ref_in[b], kn_ref[...], bd)
        kt = jnp.concatenate([ck_ref[b], jnp.concatenate([knew, zpad], axis=0).T], axis=1)
        vt = jnp.concatenate([cv_ref[b], jnp.concatenate([vn_ref_in[b], zpad], axis=0).T], axis=1)
        pieces = []
        for g in range(SW_GROUP):
            qg = _head_rms64(q_ref[b, :, g * w:(g + 1) * w], qn_ref[...], bd) * HEAD64 ** -0.5
            pieces += [(qg * masks[k]).astype(BF16) for k in range(SW_KV_HEADS)]
        s = _dot(jnp.concatenate(pieces, axis=0), kt.astype(BF16))
        s = jnp.where(valid, s, NEG)
        m = jnp.maximum(jnp.max(s, axis=-1, keepdims=True), sink)
        p = jnp.exp(s - m)
        den = jnp.sum(p, axis=-1, keepdims=True) + jnp.exp(sink - m)
        o = _dot_nt(p.astype(BF16), vt.astype(BF16)) / den
        for g in range(SW_GROUP):
            acc = jnp.zeros((rows, w), F32)
            for k in range(SW_KV_HEADS):
                h = g * SW_KV_HEADS + k
                acc += o[h * rows:(h + 1) * rows] * masks[k]
            tok_ref[b, :, g * w:(g + 1) * w] = acc.astype(tok_ref.dtype)
        kout_ref[b] = kt[:, nl:nl + WINDOW]
        vout_ref[b] = vt[:, nl:nl + WINDOW]


def _swa_sample(p, ckt, cvt, layer, qn, kn, sinks, bd, nl, nb, prev):
    b = p.shape[0]
    w = ckt.shape[2]
    nq = SW_GROUP
    nprev = 0 if prev is None else prev[0].shape[0]
    cache = jax.ShapeDtypeStruct((nprev + 1, b, w, WINDOW), F32)
    c3 = lambda i: (i, 0, 0)
    c4 = lambda i: (layer, i, 0, 0)
    stack = lambda n: pl.BlockSpec((n, nb, w, WINDOW), lambda i: (0, i, 0, 0))
    return pl.pallas_call(
        functools.partial(_swa_sample_kernel, nl=nl, nprev=nprev),
        grid=(b // nb,),
        in_specs=[
            pl.BlockSpec((nb, SAMPLE_PAD, nq * w), c3),
            pl.BlockSpec((nb, SAMPLE_PAD, w), lambda i: (i, 0, nq)),
            pl.BlockSpec((nb, SAMPLE_PAD, w), lambda i: (i, 0, nq + 1)),
            pl.BlockSpec((None, nb, w, WINDOW), c4),
            pl.BlockSpec((None, nb, w, WINDOW), c4),
            pl.BlockSpec((1, w), lambda i: (0, 0)),
            pl.BlockSpec((1, w), lambda i: (0, 0)),
            pl.BlockSpec(memory_space=pltpu.SMEM),
            pl.BlockSpec((w, w), lambda i: (0, 0)),
        ] + ([stack(nprev)] * 2 if nprev else []),
        out_specs=[
            pl.BlockSpec((nb, SAMPLE_PAD, nq * w), c3),
            stack(nprev + 1),
            stack(nprev + 1),
        ],
        out_shape=[jax.ShapeDtypeStruct((b, SAMPLE_PAD, nq * w), BF16), cache, cache],
        compiler_params=_params(("parallel",)),
        name="swa_sample",
    )(p, p, p, ckt, cvt, qn, kn, sinks, bd, *(() if prev is None else prev))


def _hgrn_consts(rows, group):
    t = np.arange(rows)
    loc, run = t % group, t // group
    same = run[:, None] == run[None, :]
    r = t[None, :]
    mats = [same & (r <= t[:, None]), same & (r > t[:, None])]
    masks = []
    h = group // 2
    while h >= 1:
        par = loc // (2 * h)
        right = (loc % (2 * h)) >= h
        bnd = (run * group + par * 2 * h + h - 1)[:, None]
        q_side = right[:, None] & (r > bnd) & (r <= t[:, None])
        k_side = (~right)[:, None] & (r > t[:, None]) & (r <= bnd)
        mats.append(q_side | k_side)
        masks.append(same & (par[:, None] == par[None, :]) & right[:, None] & (~right)[None, :])
        h //= 2
    masks.append(np.eye(rows, dtype=bool))
    w = np.concatenate(mats, axis=0).astype(np.float32)
    w2 = np.concatenate([w, w], axis=1)
    return jnp.asarray(w2, BF16), jnp.asarray(np.stack(masks).astype(np.float32)), len(masks) - 1


def _hgrn_gates(q_raw, z, lb):
    sp, sn = _sigmoid_pair(z)
    log2f = jnp.log2(lb + (1.0 - lb) * sp)
    kf = (1.0 - lb) * sn
    q = q_raw * _sigmoid(q_raw) * HG_HEAD ** -0.5
    return q, kf, log2f


def _hgrn_decays(log2f, w2_ref):
    hi = log2f.astype(BF16)
    lo = (log2f - hi.astype(F32)).astype(BF16)
    return jnp.exp2(_dot(w2_ref[...], jnp.concatenate([hi, lo], axis=0)))


def _hgrn_intra(q, kf, v, e, m_ref, nlev):
    rows = q.shape[0]
    att = m_ref[nlev] * _dot_nt(q.astype(BF16), kf.astype(BF16))
    for l in range(nlev):
        el = e[(2 + l) * rows:(3 + l) * rows]
        att += m_ref[l] * _dot_nt((q * el).astype(BF16), (kf * el).astype(BF16))
    return _dot(att.astype(BF16), v.astype(BF16))


def _hgrn_out(o, g, gn):
    ms = jnp.mean(o * o, axis=-1, keepdims=True)
    return (o * lax.rsqrt(ms + EPS) * gn * (g * _sigmoid(g))).astype(BF16)


def _hgrn_prompt_kernel(q_ref, f_ref, i_ref, g_ref, lb_ref, gn_ref, w2_ref, m_ref,
                        tok_ref, st_ref, s_scr, *, nlev):
    t = pl.program_id(1)

    @pl.when(t == 0)
    def _():
        s_scr[...] = jnp.zeros_like(s_scr)

    lb = lb_ref[...]
    gn = gn_ref[...]
    rows = HG_ROWS
    nh = q_ref.shape[1] // HG_HEAD

    def chunk(c, st):
        sl = pl.ds(pl.multiple_of(c * rows, rows), rows)
        v = i_ref[sl, :]
        g = g_ref[sl, :]
        q, kf, log2f = _hgrn_gates(q_ref[sl, :], f_ref[sl, :], lb)
        e = _hgrn_decays(log2f, w2_ref)
        qt = (q * e[0:rows]).astype(BF16)
        kh = (kf * e[rows:2 * rows]).astype(BF16)
        vb = v.astype(BF16)
        stb = st.astype(BF16)
        qb = q.astype(BF16)
        kb = kf.astype(BF16)
        ql, kl = [qb], [kb]
        rowi = lax.broadcasted_iota(jnp.int32, (rows, 1), 0)
        for l in range(nlev):
            half = rows >> (l + 1)
            if half >= 8:
                base = jnp.concatenate([(q if (r0 // half) & 1 else kf)[r0:r0 + half]
                                        for r0 in range(0, rows, half)], axis=0)
            else:
                base = jnp.where((rowi & (2 * half - 1)) >= half, q, kf)
            x = (base * e[(2 + l) * rows:(3 + l) * rows]).astype(BF16)
            ql.append(x)
            kl.append(x)
        mk = [m_ref[nlev]] + [m_ref[l] for l in range(nlev)]
        zr = jnp.zeros((rows, HG_HEAD), BF16)
        zs = jnp.zeros((HG_HEAD, HG_HEAD), BF16)

        def bdiag(a, b, z):
            return jnp.concatenate([jnp.concatenate([a, z], axis=1),
                                    jnp.concatenate([z, b], axis=1)], axis=0)

        upd = []
        for p in range(nh // 2):
            h0 = slice(2 * p * HG_HEAD, (2 * p + 1) * HG_HEAD)
            h1 = slice((2 * p + 1) * HG_HEAD, (2 * p + 2) * HG_HEAD)
            pr = slice(2 * p * HG_HEAD, (2 * p + 2) * HG_HEAD)
            att = jnp.zeros((rows, 2 * rows), F32)
            for a, b, m in zip(ql, kl, mk):
                att += m * _dot_nt(a[:, pr], bdiag(b[:, h0], b[:, h1], zr))
            o = _dot(att.astype(BF16), bdiag(vb[:, h0], vb[:, h1], zr))
            o += _dot_nt(qt[:, pr], bdiag(stb[:, h0], stb[:, h1], zs))
            for hs in (h0, h1):
                lo = hs.start - pr.start
                tok_ref[sl, hs] = _hgrn_out(o[:, lo:lo + HG_HEAD], g[:, hs], gn[:, hs])
                upd.append(_dot_tn(vb[:, hs], kh[:, hs]))
        return st * e[rows - 1:rows, :] + jnp.concatenate(upd, axis=1)

    st = lax.fori_loop(0, q_ref.shape[0] // rows, chunk, s_scr[...], unroll=True)
    s_scr[...] = st

    @pl.when(t == pl.num_programs(1) - 1)
    def _():
        for h in range(nh):
            st_ref[h] = st[:, h * HG_HEAD:(h + 1) * HG_HEAD].T


def _hgrn_prompt(p, lb, gn, tb):
    b, l, _ = p.shape
    tw = lb.shape[1]
    nh = tw // HG_HEAD
    w2, masks, nlev = _hgrn_consts(HG_ROWS, HG_ROWS)
    masks = jnp.tile(masks, (1, 1, 2))
    sec = lambda s: pl.BlockSpec((None, tb, tw), lambda i, t: (i, t, s))
    return pl.pallas_call(
        functools.partial(_hgrn_prompt_kernel, nlev=nlev),
        grid=(b, l // tb),
        in_specs=[
            sec(0), sec(1), sec(2), sec(3),
            pl.BlockSpec((1, tw), lambda i, t: (0, 0)),
            pl.BlockSpec((1, tw), lambda i, t: (0, 0)),
            pl.BlockSpec(w2.shape, lambda i, t: (0, 0)),
            pl.BlockSpec(masks.shape, lambda i, t: (0, 0, 0)),
        ],
        out_specs=[
            pl.BlockSpec((None, tb, tw), lambda i, t: (i, t, 0)),
            pl.BlockSpec((None, nh, HG_HEAD, HG_HEAD), lambda i, t: (i, 0, 0, 0)),
        ],
        out_shape=[jax.ShapeDtypeStruct((b, l, tw), BF16),
                   jax.ShapeDtypeStruct((b, nh, HG_HEAD, HG_HEAD), F32)],
        scratch_shapes=[pltpu.VMEM((HG_HEAD, tw), F32)],
        compiler_params=_params(("parallel", "arbitrary")),
        name="hgrn_prompt",
    )(p, p, p, p, lb, gn, w2, masks)


def _hgrn_sample_kernel(q_ref, f_ref, i_ref, g_ref, s0_ref, lb_ref, gn_ref, w2_ref, m_ref,
                        *rest, nlev, nl, nprev):
    if nprev:
        prev_ref, tok_ref, stack_ref = rest
        stack_ref[0:nprev] = prev_ref[...]
    else:
        tok_ref, stack_ref = rest
    st_ref = stack_ref.at[nprev]
    rows = q_ref.shape[0]
    nb = rows // nl
    v = i_ref[...]
    q, kf, log2f = _hgrn_gates(q_ref[...], f_ref[...], lb_ref[...])
    e = _hgrn_decays(log2f, w2_ref)
    o_intra = _hgrn_intra(q, kf, v, e, m_ref, nlev)
    e_cum = e[0:rows]
    qt = q * e_cum
    kh = kf * e[rows:2 * rows]
    shift = nl.bit_length() - 1
    rowb = lax.broadcasted_iota(jnp.int32, (rows, 1), 0) >> shift
    s0cat = jnp.concatenate([s0_ref[b].astype(BF16) for b in range(nb)], axis=1)
    full = _dot(qt.astype(BF16), s0cat)
    o = o_intra
    for b in range(nb):
        o += jnp.where(rowb == b, full[:, b * HG_HEAD:(b + 1) * HG_HEAD], 0.0)
    tok_ref[...] = _hgrn_out(o, g_ref[...], gn_ref[...])
    xt = jnp.concatenate([kh, e_cum], axis=0).T
    colb = lax.broadcasted_iota(jnp.int32, (rows, nb * HG_HEAD), 1) >> 7
    vbd = jnp.where(rowb == colb, jnp.concatenate([v] * nb, axis=1), 0.0).astype(BF16)
    upd = _dot(xt[:, 0:rows].astype(BF16), vbd)
    for b in range(nb):
        last = rows + b * nl + nl - 1
        st_ref[b] = s0_ref[b] * xt[:, last:last + 1] + upd[:, b * HG_HEAD:(b + 1) * HG_HEAD]


def _hgrn_sample(p, s0, layer, lb, gn, nl, prev):
    rows = HG_ROWS
    nb = rows // nl
    b, nh = s0.shape[1], s0.shape[2]
    nprev = 0 if prev is None else prev.shape[0]
    w2, masks, nlev = _hgrn_consts(rows, nl)
    sec = lambda s: pl.BlockSpec((rows, HG_HEAD), lambda i, h: (i, s * nh + h))
    stack = lambda n: pl.BlockSpec((n, nb, None, HG_HEAD, HG_HEAD), lambda i, h: (0, i, h, 0, 0))
    s0_spec = pl.BlockSpec((None, nb, None, HG_HEAD, HG_HEAD), lambda i, h: (layer, i, h, 0, 0))
    return pl.pallas_call(
        functools.partial(_hgrn_sample_kernel, nlev=nlev, nl=nl, nprev=nprev),
        grid=(b // nb, nh),
        in_specs=[
            sec(0), sec(1), sec(2), sec(3), s0_spec,
            pl.BlockSpec((1, HG_HEAD), lambda i, h: (0, h)),
            pl.BlockSpec((1, HG_HEAD), lambda i, h: (0, h)),
            pl.BlockSpec(w2.shape, lambda i, h: (0, 0)),
            pl.BlockSpec(masks.shape, lambda i, h: (0, 0, 0)),
        ] + ([stack(nprev)] if nprev else []),
        out_specs=[pl.BlockSpec((rows, HG_HEAD), lambda i, h: (i, h)), stack(nprev + 1)],
        out_shape=[jax.ShapeDtypeStruct((b * nl, nh * HG_HEAD), BF16),
                   jax.ShapeDtypeStruct((nprev + 1,) + s0.shape[1:], F32)],
        compiler_params=_params(("parallel", "parallel")),
        name="hgrn_sample",
    )(p, p, p, p, s0, lb, gn, w2, masks, *(() if prev is None else (prev,)))


def _swa_regroup(w, axis):
    axis %= w.ndim
    shape = w.shape[:axis] + (SW_KV_HEADS, SW_GROUP, HEAD64) + w.shape[axis + 1:]
    return jnp.swapaxes(w.reshape(shape), axis, axis + 1).reshape(w.shape)


def kernel(x_prompt, x_sample, cache_mem_k, cache_mem_v, state_hgrn, cache_swa_k, cache_swa_v,
           mem_prompt, norm_mix, norm_ffn, norm_mem, w_in_hgrn, hgrn_lb_raw, hgrn_out_norm,
           w_in_swa, swa_q_norm, swa_k_norm, swa_sinks, w_mem_kv, mem_q_norm, mem_k_norm,
           w_out, w_ffn_in, w_ffn_out):
    depth, d = norm_mix.shape
    bp, lp, _ = x_prompt.shape
    bs, ls, _ = x_sample.shape
    nm = mem_prompt.shape[1]
    mw = MEM_HEADS * HEAD64
    kvw = SW_KV_HEADS * HEAD64
    tokw = d - mw
    tm = 512

    seg = np.arange(mw) // HEAD64
    bd = jnp.asarray((seg[:, None] == seg[None, :]).astype(np.float32) / HEAD64, BF16)
    tile4 = lambda a: jnp.tile(a.astype(F32), (1, mw // HEAD64))[:, None, :]

    sm = jax.nn.softmax(hgrn_lb_raw.astype(F32), axis=0)
    lbs = jnp.clip(jnp.cumsum(sm, axis=0) - sm[0], 0.0, LB_MAX)

    w_in_swa_b = w_in_swa.astype(BF16)
    w_in_swa_p = jnp.concatenate([_swa_regroup(w_in_swa_b[..., :tokw], -1),
                                  w_in_swa_b[..., tokw:]], axis=-1)
    w_in_hgrn_b = w_in_hgrn.astype(BF16)
    w_out_b = w_out.astype(BF16)
    w_out_b = jnp.stack([w_out_b[i] if i % 2 == 0 else
                         jnp.concatenate([_swa_regroup(w_out_b[i, :tokw], 0), w_out_b[i, tokw:]],
                                         axis=0)
                         for i in range(depth)])
    w1_b = w_ffn_in.astype(BF16)
    w2_b = w_ffn_out.astype(BF16)
    sinks_p = swa_sinks.astype(F32).reshape(-1, SW_KV_HEADS, SW_GROUP).transpose(0, 2, 1)
    sinks_p = sinks_p.reshape(-1, SW_KV_HEADS * SW_GROUP)
    mem_qn = tile4(mem_q_norm)
    swa_qn = tile4(swa_q_norm)
    swa_kn = tile4(swa_k_norm)

    mk_p, mv_p, mk_b = _mem_kv(mem_prompt, norm_mem[:, None, :], w_mem_kv.astype(BF16),
                               tile4(mem_k_norm), bd)

    to_t = lambda a: a.transpose(0, 1, 3, 4, 2).reshape(a.shape[0], a.shape[1], -1, a.shape[2])
    from_t = lambda a, heads: a.reshape(a.shape[:-2] + (heads, HEAD64, a.shape[-1])).transpose(
        *range(a.ndim - 2), a.ndim, a.ndim - 2, a.ndim - 1)
    cmk_t, cmv_t = to_t(cache_mem_k), to_t(cache_mem_v)
    csk_t, csv_t = to_t(cache_swa_k), to_t(cache_swa_v)

    xp = x_prompt.reshape(bp * lp, d)
    xs = x_sample.reshape(bs * ls, d)
    pad_rows = lambda a: jnp.pad(a.reshape(bs, ls, a.shape[-1]),
                                 ((0, 0), (0, SAMPLE_PAD - ls), (0, 0)))
    g_mix = norm_mix[:, None, :]
    g_ffn = norm_ffn[:, None, :]
    hg_p, swk_p, swv_p = [], [], []
    hg_s = sw_s = None
    for i in range(depth):
        j = i // 2
        if i % 2 == 0:
            lb = lbs[j][None, :]
            gn = hgrn_out_norm[j][None, :].astype(F32)
            pp = _norm_matmul(xp, g_mix, i, w_in_hgrn_b, j, tm).reshape(bp, lp, -1)
            tok_p, st_p = _hgrn_prompt(pp, lb, gn, 512)
            ps = _norm_matmul(xs, g_mix, i, w_in_hgrn_b, j, tm)
            tok_s, hg_s = _hgrn_sample(ps, state_hgrn, j, lb, gn, ls, hg_s)
            hg_p.append(st_p)
            cq_s = pad_rows(ps[:, 4 * tokw:])
            cq_col = 4 * tokw // mw
        else:
            pp = _norm_matmul(xp, g_mix, i, w_in_swa_p, j, tm).reshape(bp, lp, -1)
            tok_p, kb, vb = _swa_prompt(pp, swa_qn[j], swa_kn[j], sinks_p[j], bd, 4)
            swk_p.append(kb)
            swv_p.append(vb)
            ps = pad_rows(_norm_matmul(xs, g_mix, i, w_in_swa_p, j, tm))
            tok_s, *sw_s = _swa_sample(ps, csk_t, csv_t, j, swa_qn[j], swa_kn[j], sinks_p[j],
                                       bd, ls, 8, sw_s)
            tok_s = tok_s[:, :ls].reshape(bs * ls, tokw)
            cq_s = ps[:, :, tokw + 2 * kvw:]
            cq_col = (tokw + 2 * kvw) // mw
        mo_p = _mem_attend_prompt(pp, cq_col, mk_b, mv_p, i, mem_qn[i], bd, 512)
        mo_s = _mem_attend(cq_s, 0, cmk_t, cmv_t, i, mem_qn[i], bd, 8, SAMPLE_PAD)
        mo_s = mo_s[:, :ls].reshape(bs * ls, mw)
        xp = _mix_ffn(xp, tok_p.reshape(bp * lp, tokw), mo_p.reshape(bp * lp, mw),
                      w_out_b, g_ffn, w1_b, w2_b, i, tm)
        xs = _mix_ffn(xs, tok_s, mo_s, w_out_b, g_ffn, w1_b, w2_b, i, tm)

    return (xp.reshape(bp, lp, d), xs.reshape(bs, ls, d),
            from_t(mk_p, MEM_HEADS), from_t(mv_p, MEM_HEADS),
            jnp.stack(hg_p), hg_s,
            from_t(jnp.stack(swk_p), SW_KV_HEADS), from_t(jnp.stack(swv_p), SW_KV_HEADS),
            from_t(sw_s[0], SW_KV_HEADS), from_t(sw_s[1], SW_KV_HEADS))
```

```python
import functools

import numpy as np
import jax
import jax.numpy as jnp
from jax import lax
from jax.experimental import pallas as pl
from jax.experimental.pallas import tpu as pltpu

F32 = jnp.float32
BF16 = jnp.bfloat16
EPS = 1e-6
NEG = -1e30
LB_MAX = 0.999
LOG2E = 1.4426950408889634

HEAD64 = 64
MEM_HEADS = 4
SW_KV_HEADS = 4
SW_GROUP = 3
WINDOW = 128
HG_HEAD = 128
HG_ROWS = 64
SAMPLE_PAD = 8
LANES = 128

V7X_VMEM_BYTES = 64 * 1024 * 1024
VMEM_LIMIT = V7X_VMEM_BYTES - 8 * 1024 * 1024


def _params(sem):
    return pltpu.CompilerParams(dimension_semantics=sem, vmem_limit_bytes=VMEM_LIMIT)


def _dot(a, b):
    return jnp.dot(a, b, preferred_element_type=F32)


def _dot_nt(a, b):
    return lax.dot_general(a, b, (((1,), (1,)), ((), ())), preferred_element_type=F32)


def _dot_tn(a, b):
    return lax.dot_general(a, b, (((0,), (0,)), ((), ())), preferred_element_type=F32)


def _sigmoid(x):
    return 1.0 / (1.0 + jnp.exp(-x))


def _sigmoid_pair(z):
    e = jnp.exp(-jnp.abs(z))
    r = 1.0 / (1.0 + e)
    er = e * r
    pos = z >= 0
    return jnp.where(pos, r, er), jnp.where(pos, er, r)


def _rms_rows(x, g):
    ms = jnp.mean(x * x, axis=-1, keepdims=True)
    return x * lax.rsqrt(ms + EPS) * g


def _head_rms64(x, g, bd):
    ms = _dot((x * x).astype(BF16), bd)
    return x * lax.rsqrt(ms + EPS) * g


def _head_mask(width, h):
    lane = lax.broadcasted_iota(jnp.int32, (1, width), 1)
    return ((lane >> 6) == h).astype(F32)


def _norm_matmul_kernel(x_ref, g_ref, w_ref, o_ref):
    h = _rms_rows(x_ref[...], g_ref[...]).astype(BF16)
    o_ref[...] = _dot(h, w_ref[...])


def _norm_matmul(x, g, g_idx, w, w_idx, tm):
    m, d = x.shape
    n = w.shape[2]
    tm = min(tm, m)
    return pl.pallas_call(
        _norm_matmul_kernel,
        grid=(m // tm,),
        in_specs=[
            pl.BlockSpec((tm, d), lambda i: (i, 0)),
            pl.BlockSpec((None, 1, d), lambda i: (g_idx, 0, 0)),
            pl.BlockSpec((None, d, n), lambda i: (w_idx, 0, 0)),
        ],
        out_specs=pl.BlockSpec((tm, n), lambda i: (i, 0)),
        out_shape=jax.ShapeDtypeStruct((m, n), F32),
        compiler_params=_params(("parallel",)),
        name="norm_matmul",
    )(x, g, w)


def _mix_ffn_kernel(x_ref, tok_ref, mo_ref, wo_ref, g_ref, w1_ref, w2_ref, o_ref, act_ref,
                    *, ff, chunk):
    tw = tok_ref.shape[1]
    x1 = (x_ref[...] + _dot(tok_ref[...], wo_ref[0:tw, :])
          + _dot(mo_ref[...], wo_ref[tw:, :]))
    o_ref[...] = x1
    h = _rms_rows(x1, g_ref[...]).astype(BF16)
    for c0 in range(0, ff, chunk):
        g = _dot(h, w1_ref[:, c0:c0 + chunk])
        u = _dot(h, w1_ref[:, ff + c0:ff + c0 + chunk])
        act_ref[:, c0:c0 + chunk] = (g * _sigmoid(g) * u).astype(BF16)
    o_ref[...] += _dot(act_ref[...], w2_ref[...])


def _mix_ffn(x, tok, mo, wo, g, w1, w2, layer, tm):
    m, d = x.shape
    ff = w2.shape[1]
    tm = min(tm, m)
    tw, mw = tok.shape[1], mo.shape[1]
    const = lambda i: (layer, 0, 0)
    return pl.pallas_call(
        functools.partial(_mix_ffn_kernel, ff=ff, chunk=256),
        grid=(m // tm,),
        in_specs=[
            pl.BlockSpec((tm, d), lambda i: (i, 0)),
            pl.BlockSpec((tm, tw), lambda i: (i, 0)),
            pl.BlockSpec((tm, mw), lambda i: (i, 0)),
            pl.BlockSpec((None, tw + mw, d), const, pipeline_mode=pl.Buffered(1)),
            pl.BlockSpec((None, 1, d), const),
            pl.BlockSpec((None, d, 2 * ff), const, pipeline_mode=pl.Buffered(1)),
            pl.BlockSpec((None, ff, d), const, pipeline_mode=pl.Buffered(1)),
        ],
        out_specs=pl.BlockSpec((tm, d), lambda i: (i, 0)),
        out_shape=jax.ShapeDtypeStruct((m, d), F32),
        scratch_shapes=[pltpu.VMEM((tm, ff), BF16)],
        compiler_params=_params(("parallel",)),
        name="mix_ffn",
    )(x, tok, mo, wo, g, w1, w2)


def _mem_kv_kernel(mem_ref, g_ref, w_ref, kn_ref, bd_ref, k_ref, v_ref, kb_ref):
    h = _rms_rows(mem_ref[...], g_ref[...]).astype(BF16)
    kv = _dot(h, w_ref[...])
    kw = k_ref.shape[0]
    k = _head_rms64(kv[:, :kw], kn_ref[...], bd_ref[...])
    k_ref[...] = k.T
    v_ref[...] = kv[:, kw:].T
    kb_ref[...] = jnp.concatenate([(k * _head_mask(kw, hd)).astype(BF16)
                                   for hd in range(MEM_HEADS)], axis=0)


def _mem_kv(mem, g, w, kn, bd):
    depth = w.shape[0]
    b, nm, d = mem.shape
    kw = w.shape[2] // 2
    out = jax.ShapeDtypeStruct((depth, b, kw, nm), F32)
    out_b = jax.ShapeDtypeStruct((depth, b, MEM_HEADS * nm, kw), BF16)
    return pl.pallas_call(
        _mem_kv_kernel,
        grid=(depth, b),
        in_specs=[
            pl.BlockSpec((None, nm, d), lambda i, j: (j, 0, 0)),
            pl.BlockSpec((None, 1, d), lambda i, j: (i, 0, 0)),
            pl.BlockSpec((None, d, 2 * kw), lambda i, j: (i, 0, 0)),
            pl.BlockSpec((None, 1, kw), lambda i, j: (i, 0, 0)),
            pl.BlockSpec((kw, kw), lambda i, j: (0, 0)),
        ],
        out_specs=[pl.BlockSpec((None, None, kw, nm), lambda i, j: (i, j, 0, 0))] * 2
        + [pl.BlockSpec((None, None, MEM_HEADS * nm, kw), lambda i, j: (i, j, 0, 0))],
        out_shape=[out, out, out_b],
        compiler_params=_params(("parallel", "parallel")),
        name="mem_kv",
    )(mem, g, w, kn, bd)


def _mem_attend_kernel(q_ref, k_ref, v_ref, qn_ref, bd_ref, o_ref):
    nb, tq, w = q_ref.shape
    masks = [_head_mask(w, h) for h in range(MEM_HEADS)]
    for b in range(nb):
        qn = _head_rms64(q_ref[b], qn_ref[...], bd_ref[...]) * HEAD64 ** -0.5
        kt = k_ref[b].astype(BF16)
        vt = v_ref[b].astype(BF16)
        qbd = jnp.concatenate([(qn * m).astype(BF16) for m in masks], axis=0)
        s = _dot(qbd, kt)
        p = jnp.exp(s - jnp.max(s, axis=-1, keepdims=True))
        o = _dot_nt(p.astype(BF16), vt) / jnp.sum(p, axis=-1, keepdims=True)
        acc = o[0:tq] * masks[0]
        for h in range(1, MEM_HEADS):
            acc += o[h * tq:(h + 1) * tq] * masks[h]
        o_ref[b] = acc.astype(o_ref.dtype)


def _mem_attend_rows(q, kb, vt, qn, bd):
    nm = vt.shape[1]
    qn = _head_rms64(q, qn, bd) * (HEAD64 ** -0.5 * LOG2E)
    st = _dot_nt(kb, qn.astype(BF16))
    vt = vt.astype(BF16)
    outs = []
    for h in range(MEM_HEADS):
        s = st[h * nm:(h + 1) * nm]
        p = jnp.exp2(s - jnp.max(s, axis=0, keepdims=True))
        den = jnp.sum(p, axis=0, keepdims=True)
        outs.append(_dot(vt[h * HEAD64:(h + 1) * HEAD64, :], p.astype(BF16)) / den)
    return jnp.concatenate(outs, axis=0).T


def _mem_prompt_specs(mkb, mvt, layer, col, tq):
    w, nm = mvt.shape[2], mvt.shape[3]
    ins = [
        pl.BlockSpec((None, tq, w), lambda i, t: (i, t, col)),
        pl.BlockSpec((None, None, MEM_HEADS * nm, w), lambda i, t: (layer, i, 0, 0)),
        pl.BlockSpec((None, None, w, nm), lambda i, t: (layer, i, 0, 0)),
        pl.BlockSpec((1, w), lambda i, t: (0, 0)),
        pl.BlockSpec((w, w), lambda i, t: (0, 0)),
    ]
    return ins, pl.BlockSpec((None, tq, w), lambda i, t: (i, t, 0))


def _mem_attend(q, col, mkt, mvt, layer, qn, bd, nb, tq):
    bsz, l, _ = q.shape
    w, nm = mkt.shape[2], mkt.shape[3]
    return pl.pallas_call(
        _mem_attend_kernel,
        grid=(bsz // nb, l // tq),
        in_specs=[
            pl.BlockSpec((nb, tq, w), lambda i, t: (i, t, col)),
            pl.BlockSpec((None, nb, w, nm), lambda i, t: (layer, i, 0, 0)),
            pl.BlockSpec((None, nb, w, nm), lambda i, t: (layer, i, 0, 0)),
            pl.BlockSpec((1, w), lambda i, t: (0, 0)),
            pl.BlockSpec((w, w), lambda i, t: (0, 0)),
        ],
        out_specs=pl.BlockSpec((nb, tq, w), lambda i, t: (i, t, 0)),
        out_shape=jax.ShapeDtypeStruct((bsz, l, w), BF16),
        compiler_params=_params(("parallel", "parallel")),
        name="mem_attend",
    )(q, mkt, mvt, qn, bd)


def _swa_prompt_kernel(q_ref, kc_ref, vc_ref, kp_ref, vp_ref, qn_ref, kn_ref, sink_ref, bd_ref,
                       cq_ref, mkb_ref, mvt_ref, mqn_ref, mbd_ref,
                       tok_ref, kout_ref, vout_ref, mo_ref):
    n = pl.program_id(1)
    bd = bd_ref[...]
    mo_ref[...] = _mem_attend_rows(cq_ref[...], mkb_ref[...], mvt_ref[...], mqn_ref[...],
                                   mbd_ref[...]).astype(mo_ref.dtype)
    w = kc_ref.shape[-1]
    nwin = kc_ref.shape[0] // WINDOW
    kk = _head_rms64(jnp.concatenate([kp_ref[...], kc_ref[...]], axis=0), kn_ref[...], bd)
    vv = jnp.concatenate([vp_ref[...], vc_ref[...]], axis=0)
    kb = kk.astype(BF16)
    vvt = vv.T.astype(BF16)
    r = lax.broadcasted_iota(jnp.int32, (2 * WINDOW, WINDOW), 0)
    qi = lax.broadcasted_iota(jnp.int32, (2 * WINDOW, WINDOW), 1)
    band = (r > qi) & (r <= qi + WINDOW)
    masks = [_head_mask(w, k) for k in range(SW_KV_HEADS)]
    for i in range(nwin):
        rows = slice(i * WINDOW, (i + 1) * WINDOW)
        keys = slice(i * WINDOW, (i + 2) * WINDOW)
        valid = band if i else band & ((r >= WINDOW) | (n > 0))
        pieces = []
        for g in range(SW_GROUP):
            qg = _head_rms64(q_ref[rows, g * w:(g + 1) * w], qn_ref[...], bd)
            qg = qg * (HEAD64 ** -0.5 * LOG2E)
            pieces += [(qg * m).astype(BF16) for m in masks]
        st = _dot_nt(kb[keys], jnp.concatenate(pieces, axis=0))
        outs = []
        for h in range(SW_GROUP * SW_KV_HEADS):
            k = h % SW_KV_HEADS
            s = jnp.where(valid, st[:, h * WINDOW:(h + 1) * WINDOW], NEG)
            sink = sink_ref[h] * LOG2E
            m = jnp.maximum(jnp.max(s, axis=0, keepdims=True), sink)
            p = jnp.exp2(s - m)
            den = jnp.sum(p, axis=0, keepdims=True) + jnp.exp2(sink - m)
            outs.append(_dot(vvt[k * HEAD64:(k + 1) * HEAD64, keys], p.astype(BF16)) / den)
        tok_ref[rows, :] = jnp.concatenate(outs, axis=0).T.astype(tok_ref.dtype)

    @pl.when(n == pl.num_programs(1) - 1)
    def _():
        kout_ref[...] = kk[nwin * WINDOW:].T
        vout_ref[...] = vv[nwin * WINDOW:].T


def _swa_prompt(p, qn, kn, sinks, bd, nwin, mkb, mvt, layer, mqn):
    b, l, _ = p.shape
    w = SW_KV_HEADS * HEAD64
    nq = SW_GROUP
    tq = nwin * WINDOW
    cache = jax.ShapeDtypeStruct((b, w, WINDOW), F32)
    prev = lambda c: (lambda i, n: (i, jnp.maximum(n * nwin - 1, 0), c))
    mem_in, mem_out = _mem_prompt_specs(mkb, mvt, layer, nq + 2, tq)
    return pl.pallas_call(
        _swa_prompt_kernel,
        grid=(b, l // tq),
        in_specs=[
            pl.BlockSpec((None, tq, nq * w), lambda i, n: (i, n, 0)),
            pl.BlockSpec((None, tq, w), lambda i, n: (i, n, nq)),
            pl.BlockSpec((None, tq, w), lambda i, n: (i, n, nq + 1)),
            pl.BlockSpec((None, WINDOW, w), prev(nq)),
            pl.BlockSpec((None, WINDOW, w), prev(nq + 1)),
            pl.BlockSpec((1, w), lambda i, n: (0, 0)),
            pl.BlockSpec((1, w), lambda i, n: (0, 0)),
            pl.BlockSpec(memory_space=pltpu.SMEM),
            pl.BlockSpec((w, w), lambda i, n: (0, 0)),
        ] + mem_in,
        out_specs=[
            pl.BlockSpec((None, tq, nq * w), lambda i, n: (i, n, 0)),
            pl.BlockSpec((None, w, WINDOW), lambda i, n: (i, 0, 0)),
            pl.BlockSpec((None, w, WINDOW), lambda i, n: (i, 0, 0)),
            mem_out,
        ],
        out_shape=[jax.ShapeDtypeStruct((b, l, nq * w), BF16), cache, cache,
                   jax.ShapeDtypeStruct((b, l, mvt.shape[2]), BF16)],
        compiler_params=_params(("parallel", "arbitrary")),
        name="swa_prompt",
    )(p, p, p, p, p, qn, kn, sinks, bd, p, mkb, mvt, mqn, bd)


def _swa_sample_kernel(q_ref, kn_ref_in, vn_ref_in, ck_ref, cv_ref, qn_ref, kn_ref, sink_ref,
                       bd_ref, *rest, nl, nprev):
    if nprev:
        pk_ref, pv_ref, tok_ref, kstack_ref, vstack_ref = rest
        kstack_ref[0:nprev] = pk_ref[...]
        vstack_ref[0:nprev] = pv_ref[...]
    else:
        tok_ref, kstack_ref, vstack_ref = rest
    kout_ref = kstack_ref.at[nprev]
    vout_ref = vstack_ref.at[nprev]
    nb = q_ref.shape[0]
    w = ck_ref.shape[1]
    rows = SAMPLE_PAD
    nkeys = 2 * WINDOW
    bd = bd_ref[...]
    nh = SW_GROUP * SW_KV_HEADS
    masks = [_head_mask(w, k) for k in range(SW_KV_HEADS)]
    row = lax.broadcasted_iota(jnp.int32, (nh * rows, nkeys), 0)
    r = lax.broadcasted_iota(jnp.int32, (nh * rows, nkeys), 1)
    j = row & (rows - 1)
    valid = (r > j) & (r <= j + WINDOW)
    hrow = lax.broadcasted_iota(jnp.int32, (nh * rows, 1), 0) >> 3
    sink = jnp.zeros((nh * rows, 1), F32)
    for h in range(nh):
        sink = jnp.where(hrow == h, sink_ref[h], sink)
    zpad = jnp.zeros((WINDOW - rows, w), F32)
    for b in range(nb):
        knew = _head_rms64(kn_ref_in[b], kn_ref[...], bd)
        kt = jnp.concatenate([ck_ref[b], jnp.concatenate([knew, zpad], axis=0).T], axis=1)
        vt = jnp.concatenate([cv_ref[b], jnp.concatenate([vn_ref_in[b], zpad], axis=0).T], axis=1)
        pieces = []
        for g in range(SW_GROUP):
            qg = _head_rms64(q_ref[b, :, g * w:(g + 1) * w], qn_ref[...], bd) * HEAD64 ** -0.5
            pieces += [(qg * masks[k]).astype(BF16) for k in range(SW_KV_HEADS)]
        s = _dot(jnp.concatenate(pieces, axis=0), kt.astype(BF16))
        s = jnp.where(valid, s, NEG)
        m = jnp.maximum(jnp.max(s, axis=-1, keepdims=True), sink)
        p = jnp.exp(s - m)
        den = jnp.sum(p, axis=-1, keepdims=True) + jnp.exp(sink - m)
        o = _dot_nt(p.astype(BF16), vt.astype(BF16)) / den
        for g in range(SW_GROUP):
            acc = jnp.zeros((rows, w), F32)
            for k in range(SW_KV_HEADS):
                h = g * SW_KV_HEADS + k
                acc += o[h * rows:(h + 1) * rows] * masks[k]
            tok_ref[b, :, g * w:(g + 1) * w] = acc.astype(tok_ref.dtype)
        kout_ref[b] = kt[:, nl:nl + WINDOW]
        vout_ref[b] = vt[:, nl:nl + WINDOW]


def _swa_sample(p, ckt, cvt, layer, qn, kn, sinks, bd, nl, nb, prev):
    b = p.shape[0]
    w = ckt.shape[2]
    nq = SW_GROUP
    nprev = 0 if prev is None else prev[0].shape[0]
    cache = jax.ShapeDtypeStruct((nprev + 1, b, w, WINDOW), F32)
    c3 = lambda i: (i, 0, 0)
    c4 = lambda i: (layer, i, 0, 0)
    stack = lambda n: pl.BlockSpec((n, nb, w, WINDOW), lambda i: (0, i, 0, 0))
    return pl.pallas_call(
        functools.partial(_swa_sample_kernel, nl=nl, nprev=nprev),
        grid=(b // nb,),
        in_specs=[
            pl.BlockSpec((nb, SAMPLE_PAD, nq * w), c3),
            pl.BlockSpec((nb, SAMPLE_PAD, w), lambda i: (i, 0, nq)),
            pl.BlockSpec((nb, SAMPLE_PAD, w), lambda i: (i, 0, nq + 1)),
            pl.BlockSpec((None, nb, w, WINDOW), c4),
            pl.BlockSpec((None, nb, w, WINDOW), c4),
            pl.BlockSpec((1, w), lambda i: (0, 0)),
            pl.BlockSpec((1, w), lambda i: (0, 0)),
            pl.BlockSpec(memory_space=pltpu.SMEM),
            pl.BlockSpec((w, w), lambda i: (0, 0)),
        ] + ([stack(nprev)] * 2 if nprev else []),
        out_specs=[
            pl.BlockSpec((nb, SAMPLE_PAD, nq * w), c3),
            stack(nprev + 1),
            stack(nprev + 1),
        ],
        out_shape=[jax.ShapeDtypeStruct((b, SAMPLE_PAD, nq * w), BF16), cache, cache],
        compiler_params=_params(("parallel",)),
        name="swa_sample",
    )(p, p, p, ckt, cvt, qn, kn, sinks, bd, *(() if prev is None else prev))


def _hgrn_consts(rows, group):
    t = np.arange(rows)
    loc, run = t % group, t // group
    same = run[:, None] == run[None, :]
    r = t[None, :]
    mats = [same & (r <= t[:, None]), same & (r > t[:, None])]
    masks = []
    h = group // 2
    while h >= 1:
        par = loc // (2 * h)
        right = (loc % (2 * h)) >= h
        bnd = (run * group + par * 2 * h + h - 1)[:, None]
        q_side = right[:, None] & (r > bnd) & (r <= t[:, None])
        k_side = (~right)[:, None] & (r > t[:, None]) & (r <= bnd)
        mats.append(q_side | k_side)
        masks.append(same & (par[:, None] == par[None, :]) & right[:, None] & (~right)[None, :])
        h //= 2
    masks.append(np.eye(rows, dtype=bool))
    w = np.concatenate(mats, axis=0).astype(np.float32)
    w2 = np.concatenate([w, w], axis=1)
    return jnp.asarray(w2, BF16), jnp.asarray(np.stack(masks).astype(np.float32)), len(masks) - 1


def _hgrn_gates(q_raw, z, lb):
    sp, sn = _sigmoid_pair(z)
    log2f = jnp.log2(lb + (1.0 - lb) * sp)
    kf = (1.0 - lb) * sn
    q = q_raw * _sigmoid(q_raw) * HG_HEAD ** -0.5
    return q, kf, log2f


def _hgrn_decays(log2f, w2_ref):
    hi = log2f.astype(BF16)
    lo = (log2f - hi.astype(F32)).astype(BF16)
    return jnp.exp2(_dot(w2_ref[...], jnp.concatenate([hi, lo], axis=0)))


def _hgrn_intra(q, kf, v, e, m_ref, nlev):
    rows = q.shape[0]
    att = m_ref[nlev] * _dot_nt(q.astype(BF16), kf.astype(BF16))
    for l in range(nlev):
        el = e[(2 + l) * rows:(3 + l) * rows]
        att += m_ref[l] * _dot_nt((q * el).astype(BF16), (kf * el).astype(BF16))
    return _dot(att.astype(BF16), v.astype(BF16))


def _hgrn_out(o, g, gn):
    ms = jnp.mean(o * o, axis=-1, keepdims=True)
    return (o * lax.rsqrt(ms + EPS) * gn * (g * _sigmoid(g))).astype(BF16)


def _hgrn_prompt_kernel(q_ref, f_ref, i_ref, g_ref, lb_ref, gn_ref, w2_ref, m_ref,
                        cq_ref, mkb_ref, mvt_ref, mqn_ref, mbd_ref,
                        tok_ref, st_ref, mo_ref, s_scr, *, nlev):
    t = pl.program_id(1)

    @pl.when(t == 0)
    def _():
        s_scr[...] = jnp.zeros_like(s_scr)

    mo_ref[...] = _mem_attend_rows(cq_ref[...], mkb_ref[...], mvt_ref[...], mqn_ref[...],
                                   mbd_ref[...]).astype(mo_ref.dtype)

    lb = lb_ref[...]
    gn = gn_ref[...]
    rows = HG_ROWS
    nh = q_ref.shape[1] // HG_HEAD

    def chunk(c, st):
        sl = pl.ds(pl.multiple_of(c * rows, rows), rows)
        v = i_ref[sl, :]
        g = g_ref[sl, :]
        q, kf, log2f = _hgrn_gates(q_ref[sl, :], f_ref[sl, :], lb)
        e = _hgrn_decays(log2f, w2_ref)
        qt = (q * e[0:rows]).astype(BF16)
        kh = (kf * e[rows:2 * rows]).astype(BF16)
        vb = v.astype(BF16)
        stb = st.astype(BF16)
        qb = q.astype(BF16)
        kb = kf.astype(BF16)
        ql, kl = [qb], [kb]
        rowi = lax.broadcasted_iota(jnp.int32, (rows, 1), 0)
        for l in range(nlev):
            half = rows >> (l + 1)
            if half >= 8:
                base = jnp.concatenate([(q if (r0 // half) & 1 else kf)[r0:r0 + half]
                                        for r0 in range(0, rows, half)], axis=0)
            else:
                base = jnp.where((rowi & (2 * half - 1)) >= half, q, kf)
            x = (base * e[(2 + l) * rows:(3 + l) * rows]).astype(BF16)
            ql.append(x)
            kl.append(x)
        mk = [m_ref[nlev]] + [m_ref[l] for l in range(nlev)]
        zr = jnp.zeros((rows, HG_HEAD), BF16)
        zs = jnp.zeros((HG_HEAD, HG_HEAD), BF16)

        def bdiag(a, b, z):
            return jnp.concatenate([jnp.concatenate([a, z], axis=1),
                                    jnp.concatenate([z, b], axis=1)], axis=0)

        upd = []
        for p in range(nh // 2):
            h0 = slice(2 * p * HG_HEAD, (2 * p + 1) * HG_HEAD)
            h1 = slice((2 * p + 1) * HG_HEAD, (2 * p + 2) * HG_HEAD)
            pr = slice(2 * p * HG_HEAD, (2 * p + 2) * HG_HEAD)
            att = jnp.zeros((rows, 2 * rows), F32)
            for a, b, m in zip(ql, kl, mk):
                att += m * _dot_nt(a[:, pr], bdiag(b[:, h0], b[:, h1], zr))
            o = _dot(att.astype(BF16), bdiag(vb[:, h0], vb[:, h1], zr))
            o += _dot_nt(qt[:, pr], bdiag(stb[:, h0], stb[:, h1], zs))
            for hs in (h0, h1):
                lo = hs.start - pr.start
                tok_ref[sl, hs] = _hgrn_out(o[:, lo:lo + HG_HEAD], g[:, hs], gn[:, hs])
                upd.append(_dot_tn(vb[:, hs], kh[:, hs]))
        return st * e[rows - 1:rows, :] + jnp.concatenate(upd, axis=1)

    st = lax.fori_loop(0, q_ref.shape[0] // rows, chunk, s_scr[...], unroll=True)
    s_scr[...] = st

    @pl.when(t == pl.num_programs(1) - 1)
    def _():
        for h in range(nh):
            st_ref[h] = st[:, h * HG_HEAD:(h + 1) * HG_HEAD].T


def _hgrn_prompt(p, lb, gn, tb, mkb, mvt, layer, mqn, bd):
    b, l, _ = p.shape
    tw = lb.shape[1]
    nh = tw // HG_HEAD
    mw = mvt.shape[2]
    w2, masks, nlev = _hgrn_consts(HG_ROWS, HG_ROWS)
    masks = jnp.tile(masks, (1, 1, 2))
    sec = lambda s: pl.BlockSpec((None, tb, tw), lambda i, t: (i, t, s))
    mem_in, mem_out = _mem_prompt_specs(mkb, mvt, layer, 4 * tw // mw, tb)
    return pl.pallas_call(
        functools.partial(_hgrn_prompt_kernel, nlev=nlev),
        grid=(b, l // tb),
        in_specs=[
            sec(0), sec(1), sec(2), sec(3),
            pl.BlockSpec((1, tw), lambda i, t: (0, 0)),
            pl.BlockSpec((1, tw), lambda i, t: (0, 0)),
            pl.BlockSpec(w2.shape, lambda i, t: (0, 0)),
            pl.BlockSpec(masks.shape, lambda i, t: (0, 0, 0)),
        ] + mem_in,
        out_specs=[
            pl.BlockSpec((None, tb, tw), lambda i, t: (i, t, 0)),
            pl.BlockSpec((None, nh, HG_HEAD, HG_HEAD), lambda i, t: (i, 0, 0, 0)),
            mem_out,
        ],
        out_shape=[jax.ShapeDtypeStruct((b, l, tw), BF16),
                   jax.ShapeDtypeStruct((b, nh, HG_HEAD, HG_HEAD), F32),
                   jax.ShapeDtypeStruct((b, l, mw), BF16)],
        scratch_shapes=[pltpu.VMEM((HG_HEAD, tw), F32)],
        compiler_params=_params(("parallel", "arbitrary")),
        name="hgrn_prompt",
    )(p, p, p, p, lb, gn, w2, masks, p, mkb, mvt, mqn, bd)


def _hgrn_sample_kernel(q_ref, f_ref, i_ref, g_ref, s0_ref, lb_ref, gn_ref, w2_ref, m_ref,
                        *rest, nlev, nl, nprev):
    if nprev:
        prev_ref, tok_ref, stack_ref = rest
        stack_ref[0:nprev] = prev_ref[...]
    else:
        tok_ref, stack_ref = rest
    st_ref = stack_ref.at[nprev]
    rows = q_ref.shape[0]
    nb = rows // nl
    shift = nl.bit_length() - 1
    rowb = lax.broadcasted_iota(jnp.int32, (rows, 1), 0) >> shift
    colb = lax.broadcasted_iota(jnp.int32, (rows, nb * HG_HEAD), 1) >> 7
    v_all = i_ref[...]
    q_all, kf_all, log2f = _hgrn_gates(q_ref[...], f_ref[...], lb_ref[...])
    e_all = _hgrn_decays(log2f, w2_ref)
    g_all = g_ref[...]
    gn_all = gn_ref[...]
    for hh in range(q_ref.shape[1] // HG_HEAD):
        hs = slice(hh * HG_HEAD, (hh + 1) * HG_HEAD)
        q, kf, v, e = q_all[:, hs], kf_all[:, hs], v_all[:, hs], e_all[:, hs]
        o_intra = _hgrn_intra(q, kf, v, e, m_ref, nlev)
        e_cum = e[0:rows]
        qt = q * e_cum
        kh = kf * e[rows:2 * rows]
        s0cat = jnp.concatenate([s0_ref[b, hh].astype(BF16) for b in range(nb)], axis=1)
        full = _dot(qt.astype(BF16), s0cat)
        o = o_intra
        for b in range(nb):
            o += jnp.where(rowb == b, full[:, b * HG_HEAD:(b + 1) * HG_HEAD], 0.0)
        tok_ref[:, hs] = _hgrn_out(o, g_all[:, hs], gn_all[:, hs])
        xt = jnp.concatenate([kh, e_cum], axis=0).T
        vbd = jnp.where(rowb == colb, jnp.concatenate([v] * nb, axis=1), 0.0).astype(BF16)
        upd = _dot(xt[:, 0:rows].astype(BF16), vbd)
        for b in range(nb):
            last = rows + b * nl + nl - 1
            st_ref[b, hh] = (s0_ref[b, hh] * xt[:, last:last + 1]
                             + upd[:, b * HG_HEAD:(b + 1) * HG_HEAD])


def _hgrn_sample(p, s0, layer, lb, gn, nl, prev):
    rows = HG_ROWS
    nb = rows // nl
    b, nh = s0.shape[1], s0.shape[2]
    hp = 2
    hw = hp * HG_HEAD
    nprev = 0 if prev is None else prev.shape[0]
    w2, masks, nlev = _hgrn_consts(rows, nl)
    sec = lambda s: pl.BlockSpec((rows, hw), lambda i, h: (i, s * (nh // hp) + h))
    stack = lambda n: pl.BlockSpec((n, nb, hp, HG_HEAD, HG_HEAD), lambda i, h: (0, i, h, 0, 0))
    s0_spec = pl.BlockSpec((None, nb, hp, HG_HEAD, HG_HEAD), lambda i, h: (layer, i, h, 0, 0))
    return pl.pallas_call(
        functools.partial(_hgrn_sample_kernel, nlev=nlev, nl=nl, nprev=nprev),
        grid=(b // nb, nh // hp),
        in_specs=[
            sec(0), sec(1), sec(2), sec(3), s0_spec,
            pl.BlockSpec((1, hw), lambda i, h: (0, h)),
            pl.BlockSpec((1, hw), lambda i, h: (0, h)),
            pl.BlockSpec(w2.shape, lambda i, h: (0, 0)),
            pl.BlockSpec(masks.shape, lambda i, h: (0, 0, 0)),
        ] + ([stack(nprev)] if nprev else []),
        out_specs=[pl.BlockSpec((rows, hw), lambda i, h: (i, h)), stack(nprev + 1)],
        out_shape=[jax.ShapeDtypeStruct((b * nl, nh * HG_HEAD), BF16),
                   jax.ShapeDtypeStruct((nprev + 1,) + s0.shape[1:], F32)],
        compiler_params=_params(("parallel", "parallel")),
        name="hgrn_sample",
    )(p, p, p, p, s0, lb, gn, w2, masks, *(() if prev is None else (prev,)))


def _swa_regroup(w, axis):
    axis %= w.ndim
    shape = w.shape[:axis] + (SW_KV_HEADS, SW_GROUP, HEAD64) + w.shape[axis + 1:]
    return jnp.swapaxes(w.reshape(shape), axis, axis + 1).reshape(w.shape)


def kernel(x_prompt, x_sample, cache_mem_k, cache_mem_v, state_hgrn, cache_swa_k, cache_swa_v,
           mem_prompt, norm_mix, norm_ffn, norm_mem, w_in_hgrn, hgrn_lb_raw, hgrn_out_norm,
           w_in_swa, swa_q_norm, swa_k_norm, swa_sinks, w_mem_kv, mem_q_norm, mem_k_norm,
           w_out, w_ffn_in, w_ffn_out):
    depth, d = norm_mix.shape
    bp, lp, _ = x_prompt.shape
    bs, ls, _ = x_sample.shape
    nm = mem_prompt.shape[1]
    mw = MEM_HEADS * HEAD64
    kvw = SW_KV_HEADS * HEAD64
    tokw = d - mw
    tm = 512

    seg = np.arange(mw) // HEAD64
    bd = jnp.asarray((seg[:, None] == seg[None, :]).astype(np.float32) / HEAD64, BF16)
    tile4 = lambda a: jnp.tile(a.astype(F32), (1, mw // HEAD64))[:, None, :]

    sm = jax.nn.softmax(hgrn_lb_raw.astype(F32), axis=0)
    lbs = jnp.clip(jnp.cumsum(sm, axis=0) - sm[0], 0.0, LB_MAX)

    w_in_swa_b = w_in_swa.astype(BF16)
    w_in_swa_p = jnp.concatenate([_swa_regroup(w_in_swa_b[..., :tokw], -1),
                                  w_in_swa_b[..., tokw:]], axis=-1)
    w_in_hgrn_b = w_in_hgrn.astype(BF16)
    w_out_b = w_out.astype(BF16)
    w_out_b = jnp.stack([w_out_b[i] if i % 2 == 0 else
                         jnp.concatenate([_swa_regroup(w_out_b[i, :tokw], 0), w_out_b[i, tokw:]],
                                         axis=0)
                         for i in range(depth)])
    w1_b = w_ffn_in.astype(BF16)
    w2_b = w_ffn_out.astype(BF16)
    sinks_p = swa_sinks.astype(F32).reshape(-1, SW_KV_HEADS, SW_GROUP).transpose(0, 2, 1)
    sinks_p = sinks_p.reshape(-1, SW_KV_HEADS * SW_GROUP)
    mem_qn = tile4(mem_q_norm)
    swa_qn = tile4(swa_q_norm)
    swa_kn = tile4(swa_k_norm)

    mk_p, mv_p, mk_b = _mem_kv(mem_prompt, norm_mem[:, None, :], w_mem_kv.astype(BF16),
                               tile4(mem_k_norm), bd)

    to_t = lambda a: a.transpose(0, 1, 3, 4, 2).reshape(a.shape[0], a.shape[1], -1, a.shape[2])
    from_t = lambda a, heads: a.reshape(a.shape[:-2] + (heads, HEAD64, a.shape[-1])).transpose(
        *range(a.ndim - 2), a.ndim, a.ndim - 2, a.ndim - 1)
    cmk_t, cmv_t = to_t(cache_mem_k), to_t(cache_mem_v)
    csk_t, csv_t = to_t(cache_swa_k), to_t(cache_swa_v)

    xp = x_prompt.reshape(bp * lp, d)
    xs = x_sample.reshape(bs * ls, d)
    pad_rows = lambda a: jnp.pad(a.reshape(bs, ls, a.shape[-1]),
                                 ((0, 0), (0, SAMPLE_PAD - ls), (0, 0)))
    g_mix = norm_mix[:, None, :]
    g_ffn = norm_ffn[:, None, :]
    hg_p, swk_p, swv_p = [], [], []
    hg_s = sw_s = None
    for i in range(depth):
        j = i // 2
        if i % 2 == 0:
            lb = lbs[j][None, :]
            gn = hgrn_out_norm[j][None, :].astype(F32)
            pp = _norm_matmul(xp, g_mix, i, w_in_hgrn_b, j, tm).reshape(bp, lp, -1)
            tok_p, st_p, mo_p = _hgrn_prompt(pp, lb, gn, 512, mk_b, mv_p, i, mem_qn[i], bd)
            ps = _norm_matmul(xs, g_mix, i, w_in_hgrn_b, j, tm)
            tok_s, hg_s = _hgrn_sample(ps, state_hgrn, j, lb, gn, ls, hg_s)
            hg_p.append(st_p)
            cq_s = pad_rows(ps[:, 4 * tokw:])
        else:
            pp = _norm_matmul(xp, g_mix, i, w_in_swa_p, j, tm).reshape(bp, lp, -1)
            tok_p, kb, vb, mo_p = _swa_prompt(pp, swa_qn[j], swa_kn[j], sinks_p[j], bd, 4,
                                              mk_b, mv_p, i, mem_qn[i])
            swk_p.append(kb)
            swv_p.append(vb)
            ps = pad_rows(_norm_matmul(xs, g_mix, i, w_in_swa_p, j, tm))
            tok_s, *sw_s = _swa_sample(ps, csk_t, csv_t, j, swa_qn[j], swa_kn[j], sinks_p[j],
                                       bd, ls, 8, sw_s)
            tok_s = tok_s[:, :ls].reshape(bs * ls, tokw)
            cq_s = ps[:, :, tokw + 2 * kvw:]
        mo_s = _mem_attend(cq_s, 0, cmk_t, cmv_t, i, mem_qn[i], bd, 8, SAMPLE_PAD)
        mo_s = mo_s[:, :ls].reshape(bs * ls, mw)
        xp = _mix_ffn(xp, tok_p.reshape(bp * lp, tokw), mo_p.reshape(bp * lp, mw),
                      w_out_b, g_ffn, w1_b, w2_b, i, tm)
        xs = _mix_ffn(xs, tok_s, mo_s, w_out_b, g_ffn, w1_b, w2_b, i, tm)

    return (xp.reshape(bp, lp, d), xs.reshape(bs, ls, d),
            from_t(mk_p, MEM_HEADS), from_t(mv_p, MEM_HEADS),
            jnp.stack(hg_p), hg_s,
            from_t(jnp.stack(swk_p), SW_KV_HEADS), from_t(jnp.stack(swv_p), SW_KV_HEADS),
            from_t(sw_s[0], SW_KV_HEADS), from_t(sw_s[1], SW_KV_HEADS))
```

```python
import functools

import numpy as np
import jax
import jax.numpy as jnp
from jax import lax
from jax.experimental import pallas as pl
from jax.experimental.pallas import tpu as pltpu

F32 = jnp.float32
BF16 = jnp.bfloat16
EPS = 1e-6
NEG = -1e30
LB_MAX = 0.999
LOG2E = 1.4426950408889634

HEAD64 = 64
MEM_HEADS = 4
SW_KV_HEADS = 4
SW_GROUP = 3
WINDOW = 128
HG_HEAD = 128
HG_ROWS = 64
SAMPLE_PAD = 8
LANES = 128

V7X_VMEM_BYTES = 64 * 1024 * 1024
VMEM_LIMIT = V7X_VMEM_BYTES - 8 * 1024 * 1024


def _params(sem):
    return pltpu.CompilerParams(dimension_semantics=sem, vmem_limit_bytes=VMEM_LIMIT)


def _dot(a, b):
    return jnp.dot(a, b, preferred_element_type=F32)


def _dot_nt(a, b):
    return lax.dot_general(a, b, (((1,), (1,)), ((), ())), preferred_element_type=F32)


def _dot_tn(a, b):
    return lax.dot_general(a, b, (((0,), (0,)), ((), ())), preferred_element_type=F32)


def _sigmoid(x):
    return 1.0 / (1.0 + jnp.exp(-x))


def _sigmoid_pair(z):
    e = jnp.exp(-jnp.abs(z))
    r = 1.0 / (1.0 + e)
    er = e * r
    pos = z >= 0
    return jnp.where(pos, r, er), jnp.where(pos, er, r)


def _rms_rows(x, g):
    ms = jnp.mean(x * x, axis=-1, keepdims=True)
    return x * lax.rsqrt(ms + EPS) * g


def _head_rms64(x, g, bd):
    ms = _dot((x * x).astype(BF16), bd)
    return x * lax.rsqrt(ms + EPS) * g


def _head_mask(width, h):
    lane = lax.broadcasted_iota(jnp.int32, (1, width), 1)
    return ((lane >> 6) == h).astype(F32)


def _norm_matmul_kernel(x_ref, g_ref, w_ref, o_ref):
    h = _rms_rows(x_ref[...], g_ref[...]).astype(BF16)
    o_ref[...] = _dot(h, w_ref[...])


def _norm_matmul(x, g, g_idx, w, w_idx, tm):
    m, d = x.shape
    n = w.shape[2]
    tm = min(tm, m)
    return pl.pallas_call(
        _norm_matmul_kernel,
        grid=(m // tm,),
        in_specs=[
            pl.BlockSpec((tm, d), lambda i: (i, 0)),
            pl.BlockSpec((None, 1, d), lambda i: (g_idx, 0, 0)),
            pl.BlockSpec((None, d, n), lambda i: (w_idx, 0, 0)),
        ],
        out_specs=pl.BlockSpec((tm, n), lambda i: (i, 0)),
        out_shape=jax.ShapeDtypeStruct((m, n), F32),
        compiler_params=_params(("parallel",)),
        name="norm_matmul",
    )(x, g, w)


def _mix_ffn_kernel(x_ref, tok_ref, mo_ref, wo_ref, g_ref, w1_ref, w2_ref, o_ref, act_ref,
                    *, ff, chunk):
    tw = tok_ref.shape[1]
    x1 = (x_ref[...] + _dot(tok_ref[...], wo_ref[0:tw, :])
          + _dot(mo_ref[...], wo_ref[tw:, :]))
    o_ref[...] = x1
    h = _rms_rows(x1, g_ref[...]).astype(BF16)
    for c0 in range(0, ff, chunk):
        g = _dot(h, w1_ref[:, c0:c0 + chunk])
        u = _dot(h, w1_ref[:, ff + c0:ff + c0 + chunk])
        act_ref[:, c0:c0 + chunk] = (g * _sigmoid(g) * u).astype(BF16)
    o_ref[...] += _dot(act_ref[...], w2_ref[...])


def _mix_ffn(x, tok, mo, wo, g, w1, w2, layer, tm):
    m, d = x.shape
    ff = w2.shape[1]
    tm = min(tm, m)
    tw, mw = tok.shape[1], mo.shape[1]
    const = lambda i: (layer, 0, 0)
    return pl.pallas_call(
        functools.partial(_mix_ffn_kernel, ff=ff, chunk=256),
        grid=(m // tm,),
        in_specs=[
            pl.BlockSpec((tm, d), lambda i: (i, 0)),
            pl.BlockSpec((tm, tw), lambda i: (i, 0)),
            pl.BlockSpec((tm, mw), lambda i: (i, 0)),
            pl.BlockSpec((None, tw + mw, d), const, pipeline_mode=pl.Buffered(1)),
            pl.BlockSpec((None, 1, d), const),
            pl.BlockSpec((None, d, 2 * ff), const, pipeline_mode=pl.Buffered(1)),
            pl.BlockSpec((None, ff, d), const, pipeline_mode=pl.Buffered(1)),
        ],
        out_specs=pl.BlockSpec((tm, d), lambda i: (i, 0)),
        out_shape=jax.ShapeDtypeStruct((m, d), F32),
        scratch_shapes=[pltpu.VMEM((tm, ff), BF16)],
        compiler_params=_params(("parallel",)),
        name="mix_ffn",
    )(x, tok, mo, wo, g, w1, w2)


def _mem_kv_kernel(mem_ref, g_ref, w_ref, kn_ref, bd_ref, k_ref, v_ref, kb_ref):
    nb, nm, d = mem_ref.shape
    kw = k_ref.shape[1]
    h = _rms_rows(mem_ref[...].reshape(nb * nm, d), g_ref[...]).astype(BF16)
    kv = _dot(h, w_ref[...])
    k_all = _head_rms64(kv[:, :kw], kn_ref[...], bd_ref[...])
    for b in range(nb):
        k = k_all[b * nm:(b + 1) * nm]
        k_ref[b] = k.T
        v_ref[b] = kv[b * nm:(b + 1) * nm, kw:].T
        kb_ref[b] = jnp.concatenate([(k * _head_mask(kw, hd)).astype(BF16)
                                     for hd in range(MEM_HEADS)], axis=0)


def _mem_kv(mem, g, w, kn, bd):
    depth = w.shape[0]
    b, nm, d = mem.shape
    kw = w.shape[2] // 2
    out = jax.ShapeDtypeStruct((depth, b, kw, nm), F32)
    out_b = jax.ShapeDtypeStruct((depth, b, MEM_HEADS * nm, kw), BF16)
    return pl.pallas_call(
        _mem_kv_kernel,
        grid=(depth,),
        in_specs=[
            pl.BlockSpec((b, nm, d), lambda i: (0, 0, 0)),
            pl.BlockSpec((None, 1, d), lambda i: (i, 0, 0)),
            pl.BlockSpec((None, d, 2 * kw), lambda i: (i, 0, 0)),
            pl.BlockSpec((None, 1, kw), lambda i: (i, 0, 0)),
            pl.BlockSpec((kw, kw), lambda i: (0, 0)),
        ],
        out_specs=[pl.BlockSpec((None, b, kw, nm), lambda i: (i, 0, 0, 0))] * 2
        + [pl.BlockSpec((None, b, MEM_HEADS * nm, kw), lambda i: (i, 0, 0, 0))],
        out_shape=[out, out, out_b],
        compiler_params=_params(("parallel",)),
        name="mem_kv",
    )(mem, g, w, kn, bd)


def _mem_attend_kernel(q_ref, k_ref, v_ref, qn_ref, bd_ref, o_ref):
    nb, tq, w = q_ref.shape
    masks = [_head_mask(w, h) for h in range(MEM_HEADS)]
    for b in range(nb):
        qn = _head_rms64(q_ref[b], qn_ref[...], bd_ref[...]) * HEAD64 ** -0.5
        kt = k_ref[b].astype(BF16)
        vt = v_ref[b].astype(BF16)
        qbd = jnp.concatenate([(qn * m).astype(BF16) for m in masks], axis=0)
        s = _dot(qbd, kt)
        p = jnp.exp(s - jnp.max(s, axis=-1, keepdims=True))
        o = _dot_nt(p.astype(BF16), vt) / jnp.sum(p, axis=-1, keepdims=True)
        acc = o[0:tq] * masks[0]
        for h in range(1, MEM_HEADS):
            acc += o[h * tq:(h + 1) * tq] * masks[h]
        o_ref[b] = acc.astype(o_ref.dtype)


def _mem_attend_rows(q, kb, vt, qn, bd):
    nm = vt.shape[1]
    qn = _head_rms64(q, qn, bd) * (HEAD64 ** -0.5 * LOG2E)
    st = _dot_nt(kb, qn.astype(BF16))
    vt = vt.astype(BF16)
    outs = []
    for h in range(MEM_HEADS):
        s = st[h * nm:(h + 1) * nm]
        p = jnp.exp2(s - jnp.max(s, axis=0, keepdims=True))
        den = jnp.sum(p, axis=0, keepdims=True)
        outs.append(_dot(vt[h * HEAD64:(h + 1) * HEAD64, :], p.astype(BF16)) / den)
    return jnp.concatenate(outs, axis=0).T


def _mem_prompt_specs(mkb, mvt, layer, col, tq):
    w, nm = mvt.shape[2], mvt.shape[3]
    ins = [
        pl.BlockSpec((None, tq, w), lambda i, t: (i, t, col)),
        pl.BlockSpec((None, None, MEM_HEADS * nm, w), lambda i, t: (layer, i, 0, 0)),
        pl.BlockSpec((None, None, w, nm), lambda i, t: (layer, i, 0, 0)),
        pl.BlockSpec((1, w), lambda i, t: (0, 0)),
        pl.BlockSpec((w, w), lambda i, t: (0, 0)),
    ]
    return ins, pl.BlockSpec((None, tq, w), lambda i, t: (i, t, 0))


def _mem_attend(q, col, mkt, mvt, layer, qn, bd, nb, tq):
    bsz, l, _ = q.shape
    w, nm = mkt.shape[2], mkt.shape[3]
    return pl.pallas_call(
        _mem_attend_kernel,
        grid=(bsz // nb, l // tq),
        in_specs=[
            pl.BlockSpec((nb, tq, w), lambda i, t: (i, t, col)),
            pl.BlockSpec((None, nb, w, nm), lambda i, t: (layer, i, 0, 0)),
            pl.BlockSpec((None, nb, w, nm), lambda i, t: (layer, i, 0, 0)),
            pl.BlockSpec((1, w), lambda i, t: (0, 0)),
            pl.BlockSpec((w, w), lambda i, t: (0, 0)),
        ],
        out_specs=pl.BlockSpec((nb, tq, w), lambda i, t: (i, t, 0)),
        out_shape=jax.ShapeDtypeStruct((bsz, l, w), BF16),
        compiler_params=_params(("parallel", "parallel")),
        name="mem_attend",
    )(q, mkt, mvt, qn, bd)


def _swa_prompt_kernel(q_ref, kc_ref, vc_ref, kp_ref, vp_ref, qn_ref, kn_ref, sink_ref, bd_ref,
                       cq_ref, mkb_ref, mvt_ref, mqn_ref, mbd_ref,
                       tok_ref, kout_ref, vout_ref, mo_ref):
    n = pl.program_id(1)
    bd = bd_ref[...]
    mo_ref[...] = _mem_attend_rows(cq_ref[...], mkb_ref[...], mvt_ref[...], mqn_ref[...],
                                   mbd_ref[...]).astype(mo_ref.dtype)
    w = kc_ref.shape[-1]
    nwin = kc_ref.shape[0] // WINDOW
    kk = _head_rms64(jnp.concatenate([kp_ref[...], kc_ref[...]], axis=0), kn_ref[...], bd)
    vv = jnp.concatenate([vp_ref[...], vc_ref[...]], axis=0)
    kb = kk.astype(BF16)
    vvt = vv.T.astype(BF16)
    r = lax.broadcasted_iota(jnp.int32, (2 * WINDOW, WINDOW), 0)
    qi = lax.broadcasted_iota(jnp.int32, (2 * WINDOW, WINDOW), 1)
    band = (r > qi) & (r <= qi + WINDOW)
    masks = [_head_mask(w, k) for k in range(SW_KV_HEADS)]
    for i in range(nwin):
        rows = slice(i * WINDOW, (i + 1) * WINDOW)
        keys = slice(i * WINDOW, (i + 2) * WINDOW)
        valid = band if i else band & ((r >= WINDOW) | (n > 0))
        pieces = []
        for g in range(SW_GROUP):
            qg = _head_rms64(q_ref[rows, g * w:(g + 1) * w], qn_ref[...], bd)
            qg = qg * (HEAD64 ** -0.5 * LOG2E)
            pieces += [(qg * m).astype(BF16) for m in masks]
        st = _dot_nt(kb[keys], jnp.concatenate(pieces, axis=0))
        outs = []
        for h in range(SW_GROUP * SW_KV_HEADS):
            k = h % SW_KV_HEADS
            s = jnp.where(valid, st[:, h * WINDOW:(h + 1) * WINDOW], NEG)
            sink = sink_ref[h] * LOG2E
            m = jnp.maximum(jnp.max(s, axis=0, keepdims=True), sink)
            p = jnp.exp2(s - m)
            den = jnp.sum(p, axis=0, keepdims=True) + jnp.exp2(sink - m)
            outs.append(_dot(vvt[k * HEAD64:(k + 1) * HEAD64, keys], p.astype(BF16)) / den)
        tok_ref[rows, :] = jnp.concatenate(outs, axis=0).T.astype(tok_ref.dtype)

    @pl.when(n == pl.num_programs(1) - 1)
    def _():
        kout_ref[...] = kk[nwin * WINDOW:].T
        vout_ref[...] = vv[nwin * WINDOW:].T


def _swa_prompt(p, qn, kn, sinks, bd, nwin, mkb, mvt, layer, mqn):
    b, l, _ = p.shape
    w = SW_KV_HEADS * HEAD64
    nq = SW_GROUP
    tq = nwin * WINDOW
    cache = jax.ShapeDtypeStruct((b, w, WINDOW), F32)
    prev = lambda c: (lambda i, n: (i, jnp.maximum(n * nwin - 1, 0), c))
    mem_in, mem_out = _mem_prompt_specs(mkb, mvt, layer, nq + 2, tq)
    return pl.pallas_call(
        _swa_prompt_kernel,
        grid=(b, l // tq),
        in_specs=[
            pl.BlockSpec((None, tq, nq * w), lambda i, n: (i, n, 0)),
            pl.BlockSpec((None, tq, w), lambda i, n: (i, n, nq)),
            pl.BlockSpec((None, tq, w), lambda i, n: (i, n, nq + 1)),
            pl.BlockSpec((None, WINDOW, w), prev(nq)),
            pl.BlockSpec((None, WINDOW, w), prev(nq + 1)),
            pl.BlockSpec((1, w), lambda i, n: (0, 0)),
            pl.BlockSpec((1, w), lambda i, n: (0, 0)),
            pl.BlockSpec(memory_space=pltpu.SMEM),
            pl.BlockSpec((w, w), lambda i, n: (0, 0)),
        ] + mem_in,
        out_specs=[
            pl.BlockSpec((None, tq, nq * w), lambda i, n: (i, n, 0)),
            pl.BlockSpec((None, w, WINDOW), lambda i, n: (i, 0, 0)),
            pl.BlockSpec((None, w, WINDOW), lambda i, n: (i, 0, 0)),
            mem_out,
        ],
        out_shape=[jax.ShapeDtypeStruct((b, l, nq * w), BF16), cache, cache,
                   jax.ShapeDtypeStruct((b, l, mvt.shape[2]), BF16)],
        compiler_params=_params(("parallel", "arbitrary")),
        name="swa_prompt",
    )(p, p, p, p, p, qn, kn, sinks, bd, p, mkb, mvt, mqn, bd)


def _swa_sample_kernel(q_ref, kn_ref_in, vn_ref_in, ck_ref, cv_ref, qn_ref, kn_ref, sink_ref,
                       bd_ref, *rest, nl, nprev):
    if nprev:
        pk_ref, pv_ref, tok_ref, kstack_ref, vstack_ref = rest
        kstack_ref[0:nprev] = pk_ref[...]
        vstack_ref[0:nprev] = pv_ref[...]
    else:
        tok_ref, kstack_ref, vstack_ref = rest
    kout_ref = kstack_ref.at[nprev]
    vout_ref = vstack_ref.at[nprev]
    nb = q_ref.shape[0]
    w = ck_ref.shape[1]
    rows = SAMPLE_PAD
    nkeys = 2 * WINDOW
    bd = bd_ref[...]
    nh = SW_GROUP * SW_KV_HEADS
    masks = [_head_mask(w, k) for k in range(SW_KV_HEADS)]
    row = lax.broadcasted_iota(jnp.int32, (nh * rows, nkeys), 0)
    r = lax.broadcasted_iota(jnp.int32, (nh * rows, nkeys), 1)
    j = row & (rows - 1)
    valid = (r > j) & (r <= j + WINDOW)
    hrow = lax.broadcasted_iota(jnp.int32, (nh * rows, 1), 0) >> 3
    sink = jnp.zeros((nh * rows, 1), F32)
    for h in range(nh):
        sink = jnp.where(hrow == h, sink_ref[h], sink)
    zpad = jnp.zeros((WINDOW - rows, w), F32)
    is_new = lax.broadcasted_iota(jnp.int32, (1, WINDOW), 1) < nl
    for b in range(nb):
        knew = _head_rms64(kn_ref_in[b], kn_ref[...], bd)
        ck, cv = ck_ref[b], cv_ref[b]
        knt = jnp.concatenate([knew, zpad], axis=0).T
        vnt = jnp.concatenate([vn_ref_in[b], zpad], axis=0).T
        kt = jnp.concatenate([ck, knt], axis=1)
        vt = jnp.concatenate([cv, vnt], axis=1)
        kout_ref[b] = pltpu.roll(jnp.where(is_new, knt, ck), WINDOW - nl, axis=1)
        vout_ref[b] = pltpu.roll(jnp.where(is_new, vnt, cv), WINDOW - nl, axis=1)
        pieces = []
        for g in range(SW_GROUP):
            qg = _head_rms64(q_ref[b, :, g * w:(g + 1) * w], qn_ref[...], bd) * HEAD64 ** -0.5
            pieces += [(qg * masks[k]).astype(BF16) for k in range(SW_KV_HEADS)]
        s = _dot(jnp.concatenate(pieces, axis=0), kt.astype(BF16))
        s = jnp.where(valid, s, NEG)
        m = jnp.maximum(jnp.max(s, axis=-1, keepdims=True), sink)
        p = jnp.exp(s - m)
        den = jnp.sum(p, axis=-1, keepdims=True) + jnp.exp(sink - m)
        o = _dot_nt(p.astype(BF16), vt.astype(BF16)) / den
        for g in range(SW_GROUP):
            acc = jnp.zeros((rows, w), F32)
            for k in range(SW_KV_HEADS):
                h = g * SW_KV_HEADS + k
                acc += o[h * rows:(h + 1) * rows] * masks[k]
            tok_ref[b, :, g * w:(g + 1) * w] = acc.astype(tok_ref.dtype)


def _swa_sample(p, ckt, cvt, layer, qn, kn, sinks, bd, nl, nb, prev):
    b = p.shape[0]
    w = ckt.shape[2]
    nq = SW_GROUP
    nprev = 0 if prev is None else prev[0].shape[0]
    cache = jax.ShapeDtypeStruct((nprev + 1, b, w, WINDOW), F32)
    c3 = lambda i: (i, 0, 0)
    c4 = lambda i: (layer, i, 0, 0)
    stack = lambda n: pl.BlockSpec((n, nb, w, WINDOW), lambda i: (0, i, 0, 0))
    return pl.pallas_call(
        functools.partial(_swa_sample_kernel, nl=nl, nprev=nprev),
        grid=(b // nb,),
        in_specs=[
            pl.BlockSpec((nb, SAMPLE_PAD, nq * w), c3),
            pl.BlockSpec((nb, SAMPLE_PAD, w), lambda i: (i, 0, nq)),
            pl.BlockSpec((nb, SAMPLE_PAD, w), lambda i: (i, 0, nq + 1)),
            pl.BlockSpec((None, nb, w, WINDOW), c4),
            pl.BlockSpec((None, nb, w, WINDOW), c4),
            pl.BlockSpec((1, w), lambda i: (0, 0)),
            pl.BlockSpec((1, w), lambda i: (0, 0)),
            pl.BlockSpec(memory_space=pltpu.SMEM),
            pl.BlockSpec((w, w), lambda i: (0, 0)),
        ] + ([stack(nprev)] * 2 if nprev else []),
        out_specs=[
            pl.BlockSpec((nb, SAMPLE_PAD, nq * w), c3),
            stack(nprev + 1),
            stack(nprev + 1),
        ],
        out_shape=[jax.ShapeDtypeStruct((b, SAMPLE_PAD, nq * w), BF16), cache, cache],
        compiler_params=_params(("parallel",)),
        name="swa_sample",
    )(p, p, p, ckt, cvt, qn, kn, sinks, bd, *(() if prev is None else prev))


def _hgrn_consts(rows, group):
    t = np.arange(rows)
    loc, run = t % group, t // group
    same = run[:, None] == run[None, :]
    r = t[None, :]
    mats = [same & (r <= t[:, None]), same & (r > t[:, None])]
    masks = []
    h = group // 2
    while h >= 1:
        par = loc // (2 * h)
        right = (loc % (2 * h)) >= h
        bnd = (run * group + par * 2 * h + h - 1)[:, None]
        q_side = right[:, None] & (r > bnd) & (r <= t[:, None])
        k_side = (~right)[:, None] & (r > t[:, None]) & (r <= bnd)
        mats.append(q_side | k_side)
        masks.append(same & (par[:, None] == par[None, :]) & right[:, None] & (~right)[None, :])
        h //= 2
    masks.append(np.eye(rows, dtype=bool))
    w = np.concatenate(mats, axis=0).astype(np.float32)
    w2 = np.concatenate([w, w], axis=1)
    return jnp.asarray(w2, BF16), jnp.asarray(np.stack(masks).astype(np.float32)), len(masks) - 1


def _hgrn_gates(q_raw, z, lb):
    sp, sn = _sigmoid_pair(z)
    log2f = jnp.log2(lb + (1.0 - lb) * sp)
    kf = (1.0 - lb) * sn
    q = q_raw * _sigmoid(q_raw) * HG_HEAD ** -0.5
    return q, kf, log2f


def _hgrn_decays(log2f, w2_ref):
    hi = log2f.astype(BF16)
    lo = (log2f - hi.astype(F32)).astype(BF16)
    return jnp.exp2(_dot(w2_ref[...], jnp.concatenate([hi, lo], axis=0)))


def _hgrn_intra(q, kf, v, e, m_ref, nlev):
    rows = q.shape[0]
    att = m_ref[nlev] * _dot_nt(q.astype(BF16), kf.astype(BF16))
    for l in range(nlev):
        el = e[(2 + l) * rows:(3 + l) * rows]
        att += m_ref[l] * _dot_nt((q * el).astype(BF16), (kf * el).astype(BF16))
    return _dot(att.astype(BF16), v.astype(BF16))


def _hgrn_out(o, g, gn):
    ms = jnp.mean(o * o, axis=-1, keepdims=True)
    return (o * lax.rsqrt(ms + EPS) * gn * (g * _sigmoid(g))).astype(BF16)


def _hgrn_prompt_kernel(q_ref, f_ref, i_ref, g_ref, lb_ref, gn_ref, w2_ref, m_ref,
                        cq_ref, mkb_ref, mvt_ref, mqn_ref, mbd_ref,
                        tok_ref, st_ref, mo_ref, s_scr, *, nlev):
    t = pl.program_id(1)

    @pl.when(t == 0)
    def _():
        s_scr[...] = jnp.zeros_like(s_scr)

    mo_ref[...] = _mem_attend_rows(cq_ref[...], mkb_ref[...], mvt_ref[...], mqn_ref[...],
                                   mbd_ref[...]).astype(mo_ref.dtype)

    lb = lb_ref[...]
    gn = gn_ref[...]
    rows = HG_ROWS
    nh = q_ref.shape[1] // HG_HEAD

    def chunk(c, st):
        sl = pl.ds(pl.multiple_of(c * rows, rows), rows)
        v = i_ref[sl, :]
        g = g_ref[sl, :]
        q, kf, log2f = _hgrn_gates(q_ref[sl, :], f_ref[sl, :], lb)
        e = _hgrn_decays(log2f, w2_ref)
        qt = (q * e[0:rows]).astype(BF16)
        kh = (kf * e[rows:2 * rows]).astype(BF16)
        vb = v.astype(BF16)
        stb = st.astype(BF16)
        qb = q.astype(BF16)
        kb = kf.astype(BF16)
        ql, kl = [qb], [kb]
        rowi = lax.broadcasted_iota(jnp.int32, (rows, 1), 0)
        for l in range(nlev):
            half = rows >> (l + 1)
            if half >= 8:
                base = jnp.concatenate([(q if (r0 // half) & 1 else kf)[r0:r0 + half]
                                        for r0 in range(0, rows, half)], axis=0)
            else:
                base = jnp.where((rowi & (2 * half - 1)) >= half, q, kf)
            x = (base * e[(2 + l) * rows:(3 + l) * rows]).astype(BF16)
            ql.append(x)
            kl.append(x)
        mk = [m_ref[nlev]] + [m_ref[l] for l in range(nlev)]
        zr = jnp.zeros((rows, HG_HEAD), BF16)
        zs = jnp.zeros((HG_HEAD, HG_HEAD), BF16)

        def bdiag(a, b, z):
            return jnp.concatenate([jnp.concatenate([a, z], axis=1),
                                    jnp.concatenate([z, b], axis=1)], axis=0)

        upd = []
        for p in range(nh // 2):
            h0 = slice(2 * p * HG_HEAD, (2 * p + 1) * HG_HEAD)
            h1 = slice((2 * p + 1) * HG_HEAD, (2 * p + 2) * HG_HEAD)
            pr = slice(2 * p * HG_HEAD, (2 * p + 2) * HG_HEAD)
            att = jnp.zeros((rows, 2 * rows), F32)
            for a, b, m in zip(ql, kl, mk):
                att += m * _dot_nt(a[:, pr], bdiag(b[:, h0], b[:, h1], zr))
            o = _dot(att.astype(BF16), bdiag(vb[:, h0], vb[:, h1], zr))
            o += _dot_nt(qt[:, pr], bdiag(stb[:, h0], stb[:, h1], zs))
            for hs in (h0, h1):
                lo = hs.start - pr.start
                tok_ref[sl, hs] = _hgrn_out(o[:, lo:lo + HG_HEAD], g[:, hs], gn[:, hs])
                upd.append(_dot_tn(vb[:, hs], kh[:, hs]))
        return st * e[rows - 1:rows, :] + jnp.concatenate(upd, axis=1)

    st = lax.fori_loop(0, q_ref.shape[0] // rows, chunk, s_scr[...], unroll=True)
    s_scr[...] = st

    @pl.when(t == pl.num_programs(1) - 1)
    def _():
        for h in range(nh):
            st_ref[h] = st[:, h * HG_HEAD:(h + 1) * HG_HEAD].T


def _hgrn_prompt(p, lb, gn, tb, mkb, mvt, layer, mqn, bd):
    b, l, _ = p.shape
    tw = lb.shape[1]
    nh = tw // HG_HEAD
    mw = mvt.shape[2]
    w2, masks, nlev = _hgrn_consts(HG_ROWS, HG_ROWS)
    masks = jnp.tile(masks, (1, 1, 2))
    sec = lambda s: pl.BlockSpec((None, tb, tw), lambda i, t: (i, t, s))
    mem_in, mem_out = _mem_prompt_specs(mkb, mvt, layer, 4 * tw // mw, tb)
    return pl.pallas_call(
        functools.partial(_hgrn_prompt_kernel, nlev=nlev),
        grid=(b, l // tb),
        in_specs=[
            sec(0), sec(1), sec(2), sec(3),
            pl.BlockSpec((1, tw), lambda i, t: (0, 0)),
            pl.BlockSpec((1, tw), lambda i, t: (0, 0)),
            pl.BlockSpec(w2.shape, lambda i, t: (0, 0)),
            pl.BlockSpec(masks.shape, lambda i, t: (0, 0, 0)),
        ] + mem_in,
        out_specs=[
            pl.BlockSpec((None, tb, tw), lambda i, t: (i, t, 0)),
            pl.BlockSpec((None, nh, HG_HEAD, HG_HEAD), lambda i, t: (i, 0, 0, 0)),
            mem_out,
        ],
        out_shape=[jax.ShapeDtypeStruct((b, l, tw), BF16),
                   jax.ShapeDtypeStruct((b, nh, HG_HEAD, HG_HEAD), F32),
                   jax.ShapeDtypeStruct((b, l, mw), BF16)],
        scratch_shapes=[pltpu.VMEM((HG_HEAD, tw), F32)],
        compiler_params=_params(("parallel", "arbitrary")),
        name="hgrn_prompt",
    )(p, p, p, p, lb, gn, w2, masks, p, mkb, mvt, mqn, bd)


def _hgrn_sample_kernel(q_ref, f_ref, i_ref, g_ref, s0_ref, lb_ref, gn_ref, w2_ref, m_ref,
                        *rest, nlev, nl, nprev):
    if nprev:
        prev_ref, tok_ref, stack_ref = rest
        stack_ref[0:nprev] = prev_ref[...]
    else:
        tok_ref, stack_ref = rest
    st_ref = stack_ref.at[nprev]
    rows = q_ref.shape[0]
    nb = rows // nl
    shift = nl.bit_length() - 1
    rowb = lax.broadcasted_iota(jnp.int32, (rows, 1), 0) >> shift
    colb = lax.broadcasted_iota(jnp.int32, (rows, nb * HG_HEAD), 1) >> 7
    v_all = i_ref[...]
    q_all, kf_all, log2f = _hgrn_gates(q_ref[...], f_ref[...], lb_ref[...])
    e_all = _hgrn_decays(log2f, w2_ref)
    g_all = g_ref[...]
    gn_all = gn_ref[...]
    for hh in range(q_ref.shape[1] // HG_HEAD):
        hs = slice(hh * HG_HEAD, (hh + 1) * HG_HEAD)
        q, kf, v, e = q_all[:, hs], kf_all[:, hs], v_all[:, hs], e_all[:, hs]
        o_intra = _hgrn_intra(q, kf, v, e, m_ref, nlev)
        e_cum = e[0:rows]
        qt = q * e_cum
        kh = kf * e[rows:2 * rows]
        s0cat = jnp.concatenate([s0_ref[b, hh].astype(BF16) for b in range(nb)], axis=1)
        full = _dot(qt.astype(BF16), s0cat)
        o = o_intra
        for b in range(nb):
            o += jnp.where(rowb == b, full[:, b * HG_HEAD:(b + 1) * HG_HEAD], 0.0)
        tok_ref[:, hs] = _hgrn_out(o, g_all[:, hs], gn_all[:, hs])
        xt = jnp.concatenate([kh, e_cum], axis=0).T
        vbd = jnp.where(rowb == colb, jnp.concatenate([v] * nb, axis=1), 0.0).astype(BF16)
        upd = _dot(xt[:, 0:rows].astype(BF16), vbd)
        for b in range(nb):
            last = rows + b * nl + nl - 1
            st_ref[b, hh] = (s0_ref[b, hh] * xt[:, last:last + 1]
                             + upd[:, b * HG_HEAD:(b + 1) * HG_HEAD])


def _hgrn_sample(p, s0, layer, lb, gn, nl, prev):
    rows = HG_ROWS
    nb = rows // nl
    b, nh = s0.shape[1], s0.shape[2]
    hp = 2
    hw = hp * HG_HEAD
    nprev = 0 if prev is None else prev.shape[0]
    w2, masks, nlev = _hgrn_consts(rows, nl)
    sec = lambda s: pl.BlockSpec((rows, hw), lambda i, h: (i, s * (nh // hp) + h))
    stack = lambda n: pl.BlockSpec((n, nb, hp, HG_HEAD, HG_HEAD), lambda i, h: (0, i, h, 0, 0))
    s0_spec = pl.BlockSpec((None, nb, hp, HG_HEAD, HG_HEAD), lambda i, h: (layer, i, h, 0, 0))
    return pl.pallas_call(
        functools.partial(_hgrn_sample_kernel, nlev=nlev, nl=nl, nprev=nprev),
        grid=(b // nb, nh // hp),
        in_specs=[
            sec(0), sec(1), sec(2), sec(3), s0_spec,
            pl.BlockSpec((1, hw), lambda i, h: (0, h)),
            pl.BlockSpec((1, hw), lambda i, h: (0, h)),
            pl.BlockSpec(w2.shape, lambda i, h: (0, 0)),
            pl.BlockSpec(masks.shape, lambda i, h: (0, 0, 0)),
        ] + ([stack(nprev)] if nprev else []),
        out_specs=[pl.BlockSpec((rows, hw), lambda i, h: (i, h)), stack(nprev + 1)],
        out_shape=[jax.ShapeDtypeStruct((b * nl, nh * HG_HEAD), BF16),
                   jax.ShapeDtypeStruct((nprev + 1,) + s0.shape[1:], F32)],
        compiler_params=_params(("parallel", "parallel")),
        name="hgrn_sample",
    )(p, p, p, p, s0, lb, gn, w2, masks, *(() if prev is None else (prev,)))


def _swa_regroup(w, axis):
    axis %= w.ndim
    shape = w.shape[:axis] + (SW_KV_HEADS, SW_GROUP, HEAD64) + w.shape[axis + 1:]
    return jnp.swapaxes(w.reshape(shape), axis, axis + 1).reshape(w.shape)


def kernel(x_prompt, x_sample, cache_mem_k, cache_mem_v, state_hgrn, cache_swa_k, cache_swa_v,
           mem_prompt, norm_mix, norm_ffn, norm_mem, w_in_hgrn, hgrn_lb_raw, hgrn_out_norm,
           w_in_swa, swa_q_norm, swa_k_norm, swa_sinks, w_mem_kv, mem_q_norm, mem_k_norm,
           w_out, w_ffn_in, w_ffn_out):
    depth, d = norm_mix.shape
    bp, lp, _ = x_prompt.shape
    bs, ls, _ = x_sample.shape
    nm = mem_prompt.shape[1]
    mw = MEM_HEADS * HEAD64
    kvw = SW_KV_HEADS * HEAD64
    tokw = d - mw
    tm = 512

    seg = np.arange(mw) // HEAD64
    bd = jnp.asarray((seg[:, None] == seg[None, :]).astype(np.float32) / HEAD64, BF16)
    tile4 = lambda a: jnp.tile(a.astype(F32), (1, mw // HEAD64))[:, None, :]

    sm = jax.nn.softmax(hgrn_lb_raw.astype(F32), axis=0)
    lbs = jnp.clip(jnp.cumsum(sm, axis=0) - sm[0], 0.0, LB_MAX)

    w_in_swa_b = w_in_swa.astype(BF16)
    w_in_swa_p = jnp.concatenate([_swa_regroup(w_in_swa_b[..., :tokw], -1),
                                  w_in_swa_b[..., tokw:]], axis=-1)
    w_in_hgrn_b = w_in_hgrn.astype(BF16)
    w_out_b = w_out.astype(BF16)
    w_out_b = jnp.stack([w_out_b[i] if i % 2 == 0 else
                         jnp.concatenate([_swa_regroup(w_out_b[i, :tokw], 0), w_out_b[i, tokw:]],
                                         axis=0)
                         for i in range(depth)])
    w1_b = w_ffn_in.astype(BF16)
    w2_b = w_ffn_out.astype(BF16)
    sinks_p = swa_sinks.astype(F32).reshape(-1, SW_KV_HEADS, SW_GROUP).transpose(0, 2, 1)
    sinks_p = sinks_p.reshape(-1, SW_KV_HEADS * SW_GROUP)
    mem_qn = tile4(mem_q_norm)
    swa_qn = tile4(swa_q_norm)
    swa_kn = tile4(swa_k_norm)

    mk_p, mv_p, mk_b = _mem_kv(mem_prompt, norm_mem[:, None, :], w_mem_kv.astype(BF16),
                               tile4(mem_k_norm), bd)

    to_t = lambda a: a.transpose(0, 1, 3, 4, 2).reshape(a.shape[0], a.shape[1], -1, a.shape[2])
    from_t = lambda a, heads: a.reshape(a.shape[:-2] + (heads, HEAD64, a.shape[-1])).transpose(
        *range(a.ndim - 2), a.ndim, a.ndim - 2, a.ndim - 1)
    cmk_t, cmv_t = to_t(cache_mem_k), to_t(cache_mem_v)
    csk_t, csv_t = to_t(cache_swa_k), to_t(cache_swa_v)

    xp = x_prompt.reshape(bp * lp, d)
    xs = x_sample.reshape(bs * ls, d)
    pad_rows = lambda a: jnp.pad(a.reshape(bs, ls, a.shape[-1]),
                                 ((0, 0), (0, SAMPLE_PAD - ls), (0, 0)))
    g_mix = norm_mix[:, None, :]
    g_ffn = norm_ffn[:, None, :]
    hg_p, swk_p, swv_p = [], [], []
    hg_s = sw_s = None
    for i in range(depth):
        j = i // 2
        if i % 2 == 0:
            lb = lbs[j][None, :]
            gn = hgrn_out_norm[j][None, :].astype(F32)
            pp = _norm_matmul(xp, g_mix, i, w_in_hgrn_b, j, tm).reshape(bp, lp, -1)
            tok_p, st_p, mo_p = _hgrn_prompt(pp, lb, gn, 512, mk_b, mv_p, i, mem_qn[i], bd)
            ps = _norm_matmul(xs, g_mix, i, w_in_hgrn_b, j, tm)
            tok_s, hg_s = _hgrn_sample(ps, state_hgrn, j, lb, gn, ls, hg_s)
            hg_p.append(st_p)
            cq_s = pad_rows(ps[:, 4 * tokw:])
        else:
            pp = _norm_matmul(xp, g_mix, i, w_in_swa_p, j, tm).reshape(bp, lp, -1)
            tok_p, kb, vb, mo_p = _swa_prompt(pp, swa_qn[j], swa_kn[j], sinks_p[j], bd, 4,
                                              mk_b, mv_p, i, mem_qn[i])
            swk_p.append(kb)
            swv_p.append(vb)
            ps = pad_rows(_norm_matmul(xs, g_mix, i, w_in_swa_p, j, tm))
            tok_s, *sw_s = _swa_sample(ps, csk_t, csv_t, j, swa_qn[j], swa_kn[j], sinks_p[j],
                                       bd, ls, 8, sw_s)
            tok_s = tok_s[:, :ls].reshape(bs * ls, tokw)
            cq_s = ps[:, :, tokw + 2 * kvw:]
        mo_s = _mem_attend(cq_s, 0, cmk_t, cmv_t, i, mem_qn[i], bd, 8, SAMPLE_PAD)
        mo_s = mo_s[:, :ls].reshape(bs * ls, mw)
        xp = _mix_ffn(xp, tok_p.reshape(bp * lp, tokw), mo_p.reshape(bp * lp, mw),
                      w_out_b, g_ffn, w1_b, w2_b, i, tm)
        xs = _mix_ffn(xs, tok_s, mo_s, w_out_b, g_ffn, w1_b, w2_b, i, tm)

    return (xp.reshape(bp, lp, d), xs.reshape(bs, ls, d),
            from_t(mk_p, MEM_HEADS), from_t(mv_p, MEM_HEADS),
            jnp.stack(hg_p), hg_s,
            from_t(jnp.stack(swk_p), SW_KV_HEADS), from_t(jnp.stack(swv_p), SW_KV_HEADS),
            from_t(sw_s[0], SW_KV_HEADS), from_t(sw_s[1], SW_KV_HEADS))
```

```python
import functools

import numpy as np
import jax
import jax.numpy as jnp
from jax import lax
from jax.experimental import pallas as pl
from jax.experimental.pallas import tpu as pltpu

F32 = jnp.float32
BF16 = jnp.bfloat16
EPS = 1e-6
NEG = -1e30
LB_MAX = 0.999
LOG2E = 1.4426950408889634

HEAD64 = 64
MEM_HEADS = 4
SW_KV_HEADS = 4
SW_GROUP = 3
WINDOW = 128
HG_HEAD = 128
HG_ROWS = 64
SAMPLE_PAD = 8
LANES = 128

V7X_VMEM_BYTES = 64 * 1024 * 1024
VMEM_LIMIT = V7X_VMEM_BYTES - 8 * 1024 * 1024


def _params(sem):
    return pltpu.CompilerParams(dimension_semantics=sem, vmem_limit_bytes=VMEM_LIMIT)


def _dot(a, b):
    return jnp.dot(a, b, preferred_element_type=F32)


def _dot_nt(a, b):
    return lax.dot_general(a, b, (((1,), (1,)), ((), ())), preferred_element_type=F32)


def _dot_tn(a, b):
    return lax.dot_general(a, b, (((0,), (0,)), ((), ())), preferred_element_type=F32)


def _sigmoid(x):
    return 1.0 / (1.0 + jnp.exp(-x))


def _sigmoid_pair(z):
    e = jnp.exp(-jnp.abs(z))
    r = 1.0 / (1.0 + e)
    er = e * r
    pos = z >= 0
    return jnp.where(pos, r, er), jnp.where(pos, er, r)


def _rms_rows(x, g):
    ms = jnp.mean(x * x, axis=-1, keepdims=True)
    return x * lax.rsqrt(ms + EPS) * g


def _head_rms64(x, g, bd):
    ms = _dot((x * x).astype(BF16), bd)
    return x * lax.rsqrt(ms + EPS) * g


def _head_mask(width, h):
    lane = lax.broadcasted_iota(jnp.int32, (1, width), 1)
    return ((lane >> 6) == h).astype(F32)


def _norm_matmul_kernel(x_ref, g_ref, w_ref, o_ref):
    h = _rms_rows(x_ref[...], g_ref[...]).astype(BF16)
    o_ref[...] = _dot(h, w_ref[...])


def _norm_matmul(x, g, g_idx, w, w_idx, tm):
    m, d = x.shape
    n = w.shape[2]
    tm = min(tm, m)
    return pl.pallas_call(
        _norm_matmul_kernel,
        grid=(m // tm,),
        in_specs=[
            pl.BlockSpec((tm, d), lambda i: (i, 0)),
            pl.BlockSpec((None, 1, d), lambda i: (g_idx, 0, 0)),
            pl.BlockSpec((None, d, n), lambda i: (w_idx, 0, 0)),
        ],
        out_specs=pl.BlockSpec((tm, n), lambda i: (i, 0)),
        out_shape=jax.ShapeDtypeStruct((m, n), F32),
        compiler_params=_params(("parallel",)),
        name="norm_matmul",
    )(x, g, w)


def _mix_ffn_kernel(x_ref, tok_ref, mo_ref, wo_ref, g_ref, w1_ref, w2_ref, o_ref, act_ref,
                    *, ff, chunk):
    tw = tok_ref.shape[1]
    x1 = (x_ref[...] + _dot(tok_ref[...], wo_ref[0:tw, :])
          + _dot(mo_ref[...], wo_ref[tw:, :]))
    o_ref[...] = x1
    h = _rms_rows(x1, g_ref[...]).astype(BF16)
    for c0 in range(0, ff, chunk):
        g = _dot(h, w1_ref[:, c0:c0 + chunk])
        u = _dot(h, w1_ref[:, ff + c0:ff + c0 + chunk])
        act_ref[:, c0:c0 + chunk] = (g * _sigmoid(g) * u).astype(BF16)
    o_ref[...] += _dot(act_ref[...], w2_ref[...])


def _mix_ffn(x, tok, mo, wo, g, w1, w2, layer, tm):
    m, d = x.shape
    ff = w2.shape[1]
    tm = min(tm, m)
    tw, mw = tok.shape[1], mo.shape[1]
    const = lambda i: (layer, 0, 0)
    return pl.pallas_call(
        functools.partial(_mix_ffn_kernel, ff=ff, chunk=256),
        grid=(m // tm,),
        in_specs=[
            pl.BlockSpec((tm, d), lambda i: (i, 0)),
            pl.BlockSpec((tm, tw), lambda i: (i, 0)),
            pl.BlockSpec((tm, mw), lambda i: (i, 0)),
            pl.BlockSpec((None, tw + mw, d), const, pipeline_mode=pl.Buffered(1)),
            pl.BlockSpec((None, 1, d), const),
            pl.BlockSpec((None, d, 2 * ff), const, pipeline_mode=pl.Buffered(1)),
            pl.BlockSpec((None, ff, d), const, pipeline_mode=pl.Buffered(1)),
        ],
        out_specs=pl.BlockSpec((tm, d), lambda i: (i, 0)),
        out_shape=jax.ShapeDtypeStruct((m, d), F32),
        scratch_shapes=[pltpu.VMEM((tm, ff), BF16)],
        compiler_params=_params(("parallel",)),
        name="mix_ffn",
    )(x, tok, mo, wo, g, w1, w2)


def _mem_kv_kernel(mem_ref, g_ref, w_ref, kn_ref, bd_ref, k_ref, v_ref, kb_ref):
    nb, nm, d = mem_ref.shape
    kw = k_ref.shape[1]
    h = _rms_rows(mem_ref[...].reshape(nb * nm, d), g_ref[...]).astype(BF16)
    kv = _dot(h, w_ref[...])
    k_all = _head_rms64(kv[:, :kw], kn_ref[...], bd_ref[...])
    for b in range(nb):
        k = k_all[b * nm:(b + 1) * nm]
        k_ref[b] = k.T
        v_ref[b] = kv[b * nm:(b + 1) * nm, kw:].T
        kb_ref[b] = jnp.concatenate([(k * _head_mask(kw, hd)).astype(BF16)
                                     for hd in range(MEM_HEADS)], axis=0)


def _mem_kv(mem, g, w, kn, bd):
    depth = w.shape[0]
    b, nm, d = mem.shape
    kw = w.shape[2] // 2
    out = jax.ShapeDtypeStruct((depth, b, kw, nm), F32)
    out_b = jax.ShapeDtypeStruct((depth, b, MEM_HEADS * nm, kw), BF16)
    return pl.pallas_call(
        _mem_kv_kernel,
        grid=(depth,),
        in_specs=[
            pl.BlockSpec((b, nm, d), lambda i: (0, 0, 0)),
            pl.BlockSpec((None, 1, d), lambda i: (i, 0, 0)),
            pl.BlockSpec((None, d, 2 * kw), lambda i: (i, 0, 0)),
            pl.BlockSpec((None, 1, kw), lambda i: (i, 0, 0)),
            pl.BlockSpec((kw, kw), lambda i: (0, 0)),
        ],
        out_specs=[pl.BlockSpec((None, b, kw, nm), lambda i: (i, 0, 0, 0))] * 2
        + [pl.BlockSpec((None, b, MEM_HEADS * nm, kw), lambda i: (i, 0, 0, 0))],
        out_shape=[out, out, out_b],
        compiler_params=_params(("parallel",)),
        name="mem_kv",
    )(mem, g, w, kn, bd)


def _mem_attend_kernel(q_ref, k_ref, v_ref, qn_ref, bd_ref, o_ref, *, nl):
    nb = k_ref.shape[0]
    w = q_ref.shape[1]
    tq = SAMPLE_PAD
    masks = [_head_mask(w, h) for h in range(MEM_HEADS)]
    qn_all = _head_rms64(q_ref[...], qn_ref[...], bd_ref[...]) * HEAD64 ** -0.5
    zpad = jnp.zeros((tq - nl, w), F32)
    outs = []
    for b in range(nb):
        qn = jnp.concatenate([qn_all[b * nl:(b + 1) * nl], zpad], axis=0)
        kt = k_ref[b].astype(BF16)
        vt = v_ref[b].astype(BF16)
        qbd = jnp.concatenate([(qn * m).astype(BF16) for m in masks], axis=0)
        s = _dot(qbd, kt)
        p = jnp.exp(s - jnp.max(s, axis=-1, keepdims=True))
        o = _dot_nt(p.astype(BF16), vt) / jnp.sum(p, axis=-1, keepdims=True)
        acc = o[0:tq] * masks[0]
        for h in range(1, MEM_HEADS):
            acc += o[h * tq:(h + 1) * tq] * masks[h]
        outs.append(acc[0:nl])
    o_ref[...] = jnp.concatenate(outs, axis=0).astype(o_ref.dtype)


def _mem_attend_rows(q, kb, vt, qn, bd):
    nm = vt.shape[1]
    qn = _head_rms64(q, qn, bd) * (HEAD64 ** -0.5 * LOG2E)
    st = _dot_nt(kb, qn.astype(BF16))
    vt = vt.astype(BF16)
    outs = []
    for h in range(MEM_HEADS):
        s = st[h * nm:(h + 1) * nm]
        p = jnp.exp2(s - jnp.max(s, axis=0, keepdims=True))
        den = jnp.sum(p, axis=0, keepdims=True)
        outs.append(_dot(vt[h * HEAD64:(h + 1) * HEAD64, :], p.astype(BF16)) / den)
    return jnp.concatenate(outs, axis=0).T


def _mem_prompt_specs(mkb, mvt, layer, col, tq):
    w, nm = mvt.shape[2], mvt.shape[3]
    ins = [
        pl.BlockSpec((None, tq, w), lambda i, t: (i, t, col)),
        pl.BlockSpec((None, None, MEM_HEADS * nm, w), lambda i, t: (layer, i, 0, 0)),
        pl.BlockSpec((None, None, w, nm), lambda i, t: (layer, i, 0, 0)),
        pl.BlockSpec((1, w), lambda i, t: (0, 0)),
        pl.BlockSpec((w, w), lambda i, t: (0, 0)),
    ]
    return ins, pl.BlockSpec((None, tq, w), lambda i, t: (i, t, 0))


def _mem_attend(q, col, mkt, mvt, layer, qn, bd, nb, nl):
    bsz = mkt.shape[1]
    w, nm = mkt.shape[2], mkt.shape[3]
    return pl.pallas_call(
        functools.partial(_mem_attend_kernel, nl=nl),
        grid=(bsz // nb,),
        in_specs=[
            pl.BlockSpec((nb * nl, w), lambda i: (i, col)),
            pl.BlockSpec((None, nb, w, nm), lambda i: (layer, i, 0, 0)),
            pl.BlockSpec((None, nb, w, nm), lambda i: (layer, i, 0, 0)),
            pl.BlockSpec((1, w), lambda i: (0, 0)),
            pl.BlockSpec((w, w), lambda i: (0, 0)),
        ],
        out_specs=pl.BlockSpec((nb * nl, w), lambda i: (i, 0)),
        out_shape=jax.ShapeDtypeStruct((bsz * nl, w), BF16),
        compiler_params=_params(("parallel",)),
        name="mem_attend",
    )(q, mkt, mvt, qn, bd)


def _swa_prompt_kernel(q_ref, kc_ref, vc_ref, kp_ref, vp_ref, qn_ref, kn_ref, sink_ref, bd_ref,
                       cq_ref, mkb_ref, mvt_ref, mqn_ref, mbd_ref,
                       tok_ref, kout_ref, vout_ref, mo_ref):
    n = pl.program_id(1)
    bd = bd_ref[...]
    mo_ref[...] = _mem_attend_rows(cq_ref[...], mkb_ref[...], mvt_ref[...], mqn_ref[...],
                                   mbd_ref[...]).astype(mo_ref.dtype)
    w = kc_ref.shape[-1]
    nwin = kc_ref.shape[0] // WINDOW
    kk = _head_rms64(jnp.concatenate([kp_ref[...], kc_ref[...]], axis=0), kn_ref[...], bd)
    vv = jnp.concatenate([vp_ref[...], vc_ref[...]], axis=0)
    kb = kk.astype(BF16)
    vvt = vv.T.astype(BF16)
    r = lax.broadcasted_iota(jnp.int32, (2 * WINDOW, WINDOW), 0)
    qi = lax.broadcasted_iota(jnp.int32, (2 * WINDOW, WINDOW), 1)
    band = (r > qi) & (r <= qi + WINDOW)
    masks = [_head_mask(w, k) for k in range(SW_KV_HEADS)]
    for i in range(nwin):
        rows = slice(i * WINDOW, (i + 1) * WINDOW)
        keys = slice(i * WINDOW, (i + 2) * WINDOW)
        valid = band if i else band & ((r >= WINDOW) | (n > 0))
        pieces = []
        for g in range(SW_GROUP):
            qg = _head_rms64(q_ref[rows, g * w:(g + 1) * w], qn_ref[...], bd)
            qg = qg * (HEAD64 ** -0.5 * LOG2E)
            pieces += [(qg * m).astype(BF16) for m in masks]
        st = _dot_nt(kb[keys], jnp.concatenate(pieces, axis=0))
        outs = []
        for h in range(SW_GROUP * SW_KV_HEADS):
            k = h % SW_KV_HEADS
            s = jnp.where(valid, st[:, h * WINDOW:(h + 1) * WINDOW], NEG)
            sink = sink_ref[h] * LOG2E
            m = jnp.maximum(jnp.max(s, axis=0, keepdims=True), sink)
            p = jnp.exp2(s - m)
            den = jnp.sum(p, axis=0, keepdims=True) + jnp.exp2(sink - m)
            outs.append(_dot(vvt[k * HEAD64:(k + 1) * HEAD64, keys], p.astype(BF16)) / den)
        tok_ref[rows, :] = jnp.concatenate(outs, axis=0).T.astype(tok_ref.dtype)

    @pl.when(n == pl.num_programs(1) - 1)
    def _():
        kout_ref[...] = kk[nwin * WINDOW:].T
        vout_ref[...] = vv[nwin * WINDOW:].T


def _swa_prompt(p, qn, kn, sinks, bd, nwin, mkb, mvt, layer, mqn):
    b, l, _ = p.shape
    w = SW_KV_HEADS * HEAD64
    nq = SW_GROUP
    tq = nwin * WINDOW
    cache = jax.ShapeDtypeStruct((b, w, WINDOW), F32)
    prev = lambda c: (lambda i, n: (i, jnp.maximum(n * nwin - 1, 0), c))
    mem_in, mem_out = _mem_prompt_specs(mkb, mvt, layer, nq + 2, tq)
    return pl.pallas_call(
        _swa_prompt_kernel,
        grid=(b, l // tq),
        in_specs=[
            pl.BlockSpec((None, tq, nq * w), lambda i, n: (i, n, 0)),
            pl.BlockSpec((None, tq, w), lambda i, n: (i, n, nq)),
            pl.BlockSpec((None, tq, w), lambda i, n: (i, n, nq + 1)),
            pl.BlockSpec((None, WINDOW, w), prev(nq)),
            pl.BlockSpec((None, WINDOW, w), prev(nq + 1)),
            pl.BlockSpec((1, w), lambda i, n: (0, 0)),
            pl.BlockSpec((1, w), lambda i, n: (0, 0)),
            pl.BlockSpec(memory_space=pltpu.SMEM),
            pl.BlockSpec((w, w), lambda i, n: (0, 0)),
        ] + mem_in,
        out_specs=[
            pl.BlockSpec((None, tq, nq * w), lambda i, n: (i, n, 0)),
            pl.BlockSpec((None, w, WINDOW), lambda i, n: (i, 0, 0)),
            pl.BlockSpec((None, w, WINDOW), lambda i, n: (i, 0, 0)),
            mem_out,
        ],
        out_shape=[jax.ShapeDtypeStruct((b, l, nq * w), BF16), cache, cache,
                   jax.ShapeDtypeStruct((b, l, mvt.shape[2]), BF16)],
        compiler_params=_params(("parallel", "arbitrary")),
        name="swa_prompt",
    )(p, p, p, p, p, qn, kn, sinks, bd, p, mkb, mvt, mqn, bd)


def _swa_sample_kernel(q_ref, kn_ref_in, vn_ref_in, ck_ref, cv_ref, qn_ref, kn_ref, sink_ref,
                       bd_ref, *rest, nl, nprev):
    if nprev:
        pk_ref, pv_ref, tok_ref, kstack_ref, vstack_ref = rest
        kstack_ref[0:nprev] = pk_ref[...]
        vstack_ref[0:nprev] = pv_ref[...]
    else:
        tok_ref, kstack_ref, vstack_ref = rest
    kout_ref = kstack_ref.at[nprev]
    vout_ref = vstack_ref.at[nprev]
    nb = ck_ref.shape[0]
    w = ck_ref.shape[1]
    rows = SAMPLE_PAD
    nkeys = 2 * WINDOW
    bd = bd_ref[...]
    nh = SW_GROUP * SW_KV_HEADS
    masks = [_head_mask(w, k) for k in range(SW_KV_HEADS)]
    row = lax.broadcasted_iota(jnp.int32, (nh * rows, nkeys), 0)
    r = lax.broadcasted_iota(jnp.int32, (nh * rows, nkeys), 1)
    j = row & (rows - 1)
    valid = (r > j) & (r <= j + WINDOW)
    hrow = lax.broadcasted_iota(jnp.int32, (nh * rows, 1), 0) >> 3
    sink = jnp.zeros((nh * rows, 1), F32)
    for h in range(nh):
        sink = jnp.where(hrow == h, sink_ref[h], sink)
    zpad = jnp.zeros((WINDOW - nl, w), F32)
    zq = jnp.zeros((rows - nl, w), F32)
    is_new = lax.broadcasted_iota(jnp.int32, (1, WINDOW), 1) < nl
    knew_all = _head_rms64(kn_ref_in[...], kn_ref[...], bd)
    vnew_all = vn_ref_in[...]
    q_all = [_head_rms64(q_ref[:, g * w:(g + 1) * w], qn_ref[...], bd) * HEAD64 ** -0.5
             for g in range(SW_GROUP)]
    toks = []
    for b in range(nb):
        mine = slice(b * nl, (b + 1) * nl)
        ck, cv = ck_ref[b], cv_ref[b]
        knt = jnp.concatenate([knew_all[mine], zpad], axis=0).T
        vnt = jnp.concatenate([vnew_all[mine], zpad], axis=0).T
        kt = jnp.concatenate([ck, knt], axis=1)
        vt = jnp.concatenate([cv, vnt], axis=1)
        kout_ref[b] = pltpu.roll(jnp.where(is_new, knt, ck), WINDOW - nl, axis=1)
        vout_ref[b] = pltpu.roll(jnp.where(is_new, vnt, cv), WINDOW - nl, axis=1)
        pieces = []
        for g in range(SW_GROUP):
            qg = jnp.concatenate([q_all[g][mine], zq], axis=0)
            pieces += [(qg * masks[k]).astype(BF16) for k in range(SW_KV_HEADS)]
        s = _dot(jnp.concatenate(pieces, axis=0), kt.astype(BF16))
        s = jnp.where(valid, s, NEG)
        m = jnp.maximum(jnp.max(s, axis=-1, keepdims=True), sink)
        p = jnp.exp(s - m)
        den = jnp.sum(p, axis=-1, keepdims=True) + jnp.exp(sink - m)
        o = _dot_nt(p.astype(BF16), vt.astype(BF16)) / den
        accs = []
        for g in range(SW_GROUP):
            acc = jnp.zeros((rows, w), F32)
            for k in range(SW_KV_HEADS):
                h = g * SW_KV_HEADS + k
                acc += o[h * rows:(h + 1) * rows] * masks[k]
            accs.append(acc[0:nl])
        toks.append(jnp.concatenate(accs, axis=1))
    tok_ref[...] = jnp.concatenate(toks, axis=0).astype(tok_ref.dtype)


def _swa_sample(p, ckt, cvt, layer, qn, kn, sinks, bd, nl, nb, prev):
    b = ckt.shape[1]
    w = ckt.shape[2]
    nq = SW_GROUP
    nprev = 0 if prev is None else prev[0].shape[0]
    cache = jax.ShapeDtypeStruct((nprev + 1, b, w, WINDOW), F32)
    c4 = lambda i: (layer, i, 0, 0)
    stack = lambda n: pl.BlockSpec((n, nb, w, WINDOW), lambda i: (0, i, 0, 0))
    return pl.pallas_call(
        functools.partial(_swa_sample_kernel, nl=nl, nprev=nprev),
        grid=(b // nb,),
        in_specs=[
            pl.BlockSpec((nb * nl, nq * w), lambda i: (i, 0)),
            pl.BlockSpec((nb * nl, w), lambda i: (i, nq)),
            pl.BlockSpec((nb * nl, w), lambda i: (i, nq + 1)),
            pl.BlockSpec((None, nb, w, WINDOW), c4),
            pl.BlockSpec((None, nb, w, WINDOW), c4),
            pl.BlockSpec((1, w), lambda i: (0, 0)),
            pl.BlockSpec((1, w), lambda i: (0, 0)),
            pl.BlockSpec(memory_space=pltpu.SMEM),
            pl.BlockSpec((w, w), lambda i: (0, 0)),
        ] + ([stack(nprev)] * 2 if nprev else []),
        out_specs=[
            pl.BlockSpec((nb * nl, nq * w), lambda i: (i, 0)),
            stack(nprev + 1),
            stack(nprev + 1),
        ],
        out_shape=[jax.ShapeDtypeStruct((b * nl, nq * w), BF16), cache, cache],
        compiler_params=_params(("parallel",)),
        name="swa_sample",
    )(p, p, p, ckt, cvt, qn, kn, sinks, bd, *(() if prev is None else prev))


def _hgrn_consts(rows, group):
    t = np.arange(rows)
    loc, run = t % group, t // group
    same = run[:, None] == run[None, :]
    r = t[None, :]
    mats = [same & (r <= t[:, None]), same & (r > t[:, None])]
    masks = []
    h = group // 2
    while h >= 1:
        par = loc // (2 * h)
        right = (loc % (2 * h)) >= h
        bnd = (run * group + par * 2 * h + h - 1)[:, None]
        q_side = right[:, None] & (r > bnd) & (r <= t[:, None])
        k_side = (~right)[:, None] & (r > t[:, None]) & (r <= bnd)
        mats.append(q_side | k_side)
        masks.append(same & (par[:, None] == par[None, :]) & right[:, None] & (~right)[None, :])
        h //= 2
    masks.append(np.eye(rows, dtype=bool))
    w = np.concatenate(mats, axis=0).astype(np.float32)
    w2 = np.concatenate([w, w], axis=1)
    return jnp.asarray(w2, BF16), jnp.asarray(np.stack(masks).astype(np.float32)), len(masks) - 1


def _hgrn_gates(q_raw, z, lb):
    sp, sn = _sigmoid_pair(z)
    log2f = jnp.log2(lb + (1.0 - lb) * sp)
    kf = (1.0 - lb) * sn
    q = q_raw * _sigmoid(q_raw) * HG_HEAD ** -0.5
    return q, kf, log2f


def _hgrn_decays(log2f, w2_ref):
    hi = log2f.astype(BF16)
    lo = (log2f - hi.astype(F32)).astype(BF16)
    return jnp.exp2(_dot(w2_ref[...], jnp.concatenate([hi, lo], axis=0)))


def _hgrn_intra(q, kf, v, e, m_ref, nlev):
    rows = q.shape[0]
    att = m_ref[nlev] * _dot_nt(q.astype(BF16), kf.astype(BF16))
    for l in range(nlev):
        el = e[(2 + l) * rows:(3 + l) * rows]
        att += m_ref[l] * _dot_nt((q * el).astype(BF16), (kf * el).astype(BF16))
    return _dot(att.astype(BF16), v.astype(BF16))


def _hgrn_out(o, g, gn):
    ms = jnp.mean(o * o, axis=-1, keepdims=True)
    return (o * lax.rsqrt(ms + EPS) * gn * (g * _sigmoid(g))).astype(BF16)


def _hgrn_prompt_kernel(q_ref, f_ref, i_ref, g_ref, lb_ref, gn_ref, w2_ref, m_ref,
                        cq_ref, mkb_ref, mvt_ref, mqn_ref, mbd_ref,
                        tok_ref, st_ref, mo_ref, s_scr, *, nlev):
    t = pl.program_id(1)

    @pl.when(t == 0)
    def _():
        s_scr[...] = jnp.zeros_like(s_scr)

    mo_ref[...] = _mem_attend_rows(cq_ref[...], mkb_ref[...], mvt_ref[...], mqn_ref[...],
                                   mbd_ref[...]).astype(mo_ref.dtype)

    lb = lb_ref[...]
    gn = gn_ref[...]
    rows = HG_ROWS
    nh = q_ref.shape[1] // HG_HEAD

    def chunk(c, st):
        sl = pl.ds(pl.multiple_of(c * rows, rows), rows)
        v = i_ref[sl, :]
        g = g_ref[sl, :]
        q, kf, log2f = _hgrn_gates(q_ref[sl, :], f_ref[sl, :], lb)
        e = _hgrn_decays(log2f, w2_ref)
        qt = (q * e[0:rows]).astype(BF16)
        kh = (kf * e[rows:2 * rows]).astype(BF16)
        vb = v.astype(BF16)
        stb = st.astype(BF16)
        qb = q.astype(BF16)
        kb = kf.astype(BF16)
        ql, kl = [qb], [kb]
        rowi = lax.broadcasted_iota(jnp.int32, (rows, 1), 0)
        for l in range(nlev):
            half = rows >> (l + 1)
            if half >= 8:
                base = jnp.concatenate([(q if (r0 // half) & 1 else kf)[r0:r0 + half]
                                        for r0 in range(0, rows, half)], axis=0)
            else:
                base = jnp.where((rowi & (2 * half - 1)) >= half, q, kf)
            x = (base * e[(2 + l) * rows:(3 + l) * rows]).astype(BF16)
            ql.append(x)
            kl.append(x)
        mk = [m_ref[nlev]] + [m_ref[l] for l in range(nlev)]
        zr = jnp.zeros((rows, HG_HEAD), BF16)
        zs = jnp.zeros((HG_HEAD, HG_HEAD), BF16)

        def bdiag(a, b, z):
            return jnp.concatenate([jnp.concatenate([a, z], axis=1),
                                    jnp.concatenate([z, b], axis=1)], axis=0)

        upd = []
        for p in range(nh // 2):
            h0 = slice(2 * p * HG_HEAD, (2 * p + 1) * HG_HEAD)
            h1 = slice((2 * p + 1) * HG_HEAD, (2 * p + 2) * HG_HEAD)
            pr = slice(2 * p * HG_HEAD, (2 * p + 2) * HG_HEAD)
            att = jnp.zeros((rows, 2 * rows), F32)
            for a, b, m in zip(ql, kl, mk):
                att += m * _dot_nt(a[:, pr], bdiag(b[:, h0], b[:, h1], zr))
            o = _dot(att.astype(BF16), bdiag(vb[:, h0], vb[:, h1], zr))
            o += _dot_nt(qt[:, pr], bdiag(stb[:, h0], stb[:, h1], zs))
            for hs in (h0, h1):
                lo = hs.start - pr.start
                tok_ref[sl, hs] = _hgrn_out(o[:, lo:lo + HG_HEAD], g[:, hs], gn[:, hs])
                upd.append(_dot_tn(vb[:, hs], kh[:, hs]))
        return st * e[rows - 1:rows, :] + jnp.concatenate(upd, axis=1)

    st = lax.fori_loop(0, q_ref.shape[0] // rows, chunk, s_scr[...], unroll=True)
    s_scr[...] = st

    @pl.when(t == pl.num_programs(1) - 1)
    def _():
        for h in range(nh):
            st_ref[h] = st[:, h * HG_HEAD:(h + 1) * HG_HEAD].T


def _hgrn_prompt(p, lb, gn, tb, mkb, mvt, layer, mqn, bd):
    b, l, _ = p.shape
    tw = lb.shape[1]
    nh = tw // HG_HEAD
    mw = mvt.shape[2]
    w2, masks, nlev = _hgrn_consts(HG_ROWS, HG_ROWS)
    masks = jnp.tile(masks, (1, 1, 2))
    sec = lambda s: pl.BlockSpec((None, tb, tw), lambda i, t: (i, t, s))
    mem_in, mem_out = _mem_prompt_specs(mkb, mvt, layer, 4 * tw // mw, tb)
    return pl.pallas_call(
        functools.partial(_hgrn_prompt_kernel, nlev=nlev),
        grid=(b, l // tb),
        in_specs=[
            sec(0), sec(1), sec(2), sec(3),
            pl.BlockSpec((1, tw), lambda i, t: (0, 0)),
            pl.BlockSpec((1, tw), lambda i, t: (0, 0)),
            pl.BlockSpec(w2.shape, lambda i, t: (0, 0)),
            pl.BlockSpec(masks.shape, lambda i, t: (0, 0, 0)),
        ] + mem_in,
        out_specs=[
            pl.BlockSpec((None, tb, tw), lambda i, t: (i, t, 0)),
            pl.BlockSpec((None, nh, HG_HEAD, HG_HEAD), lambda i, t: (i, 0, 0, 0)),
            mem_out,
        ],
        out_shape=[jax.ShapeDtypeStruct((b, l, tw), BF16),
                   jax.ShapeDtypeStruct((b, nh, HG_HEAD, HG_HEAD), F32),
                   jax.ShapeDtypeStruct((b, l, mw), BF16)],
        scratch_shapes=[pltpu.VMEM((HG_HEAD, tw), F32)],
        compiler_params=_params(("parallel", "arbitrary")),
        name="hgrn_prompt",
    )(p, p, p, p, lb, gn, w2, masks, p, mkb, mvt, mqn, bd)


def _hgrn_sample_kernel(q_ref, f_ref, i_ref, g_ref, s0_ref, lb_ref, gn_ref, w2_ref, m_ref,
                        *rest, nlev, nl, nprev):
    if nprev:
        prev_ref, tok_ref, stack_ref = rest
        stack_ref[0:nprev] = prev_ref[...]
    else:
        tok_ref, stack_ref = rest
    st_ref = stack_ref.at[nprev]
    rows = q_ref.shape[0]
    nb = rows // nl
    shift = nl.bit_length() - 1
    rowb = lax.broadcasted_iota(jnp.int32, (rows, 1), 0) >> shift
    colb = lax.broadcasted_iota(jnp.int32, (rows, nb * HG_HEAD), 1) >> 7
    v_all = i_ref[...]
    q_all, kf_all, log2f = _hgrn_gates(q_ref[...], f_ref[...], lb_ref[...])
    e_all = _hgrn_decays(log2f, w2_ref)
    g_all = g_ref[...]
    gn_all = gn_ref[...]
    for hh in range(q_ref.shape[1] // HG_HEAD):
        hs = slice(hh * HG_HEAD, (hh + 1) * HG_HEAD)
        q, kf, v, e = q_all[:, hs], kf_all[:, hs], v_all[:, hs], e_all[:, hs]
        o_intra = _hgrn_intra(q, kf, v, e, m_ref, nlev)
        e_cum = e[0:rows]
        qt = q * e_cum
        kh = kf * e[rows:2 * rows]
        s0cat = jnp.concatenate([s0_ref[b, hh].astype(BF16) for b in range(nb)], axis=1)
        full = _dot(qt.astype(BF16), s0cat)
        o = o_intra
        for b in range(nb):
            o += jnp.where(rowb == b, full[:, b * HG_HEAD:(b + 1) * HG_HEAD], 0.0)
        tok_ref[:, hs] = _hgrn_out(o, g_all[:, hs], gn_all[:, hs])
        xt = jnp.concatenate([kh, e_cum], axis=0).T
        vbd = jnp.where(rowb == colb, jnp.concatenate([v] * nb, axis=1), 0.0).astype(BF16)
        upd = _dot(xt[:, 0:rows].astype(BF16), vbd)
        for b in range(nb):
            last = rows + b * nl + nl - 1
            st_ref[b, hh] = (s0_ref[b, hh] * xt[:, last:last + 1]
                             + upd[:, b * HG_HEAD:(b + 1) * HG_HEAD])


def _hgrn_sample(p, s0, layer, lb, gn, nl, prev):
    rows = HG_ROWS
    nb = rows // nl
    b, nh = s0.shape[1], s0.shape[2]
    hp = 2
    hw = hp * HG_HEAD
    nprev = 0 if prev is None else prev.shape[0]
    w2, masks, nlev = _hgrn_consts(rows, nl)
    sec = lambda s: pl.BlockSpec((rows, hw), lambda i, h: (i, s * (nh // hp) + h))
    stack = lambda n: pl.BlockSpec((n, nb, hp, HG_HEAD, HG_HEAD), lambda i, h: (0, i, h, 0, 0))
    s0_spec = pl.BlockSpec((None, nb, hp, HG_HEAD, HG_HEAD), lambda i, h: (layer, i, h, 0, 0))
    return pl.pallas_call(
        functools.partial(_hgrn_sample_kernel, nlev=nlev, nl=nl, nprev=nprev),
        grid=(b // nb, nh // hp),
        in_specs=[
            sec(0), sec(1), sec(2), sec(3), s0_spec,
            pl.BlockSpec((1, hw), lambda i, h: (0, h)),
            pl.BlockSpec((1, hw), lambda i, h: (0, h)),
            pl.BlockSpec(w2.shape, lambda i, h: (0, 0)),
            pl.BlockSpec(masks.shape, lambda i, h: (0, 0, 0)),
        ] + ([stack(nprev)] if nprev else []),
        out_specs=[pl.BlockSpec((rows, hw), lambda i, h: (i, h)), stack(nprev + 1)],
        out_shape=[jax.ShapeDtypeStruct((b * nl, nh * HG_HEAD), BF16),
                   jax.ShapeDtypeStruct((nprev + 1,) + s0.shape[1:], F32)],
        compiler_params=_params(("parallel", "parallel")),
        name="hgrn_sample",
    )(p, p, p, p, s0, lb, gn, w2, masks, *(() if prev is None else (prev,)))


def _swa_regroup(w, axis):
    axis %= w.ndim
    shape = w.shape[:axis] + (SW_KV_HEADS, SW_GROUP, HEAD64) + w.shape[axis + 1:]
    return jnp.swapaxes(w.reshape(shape), axis, axis + 1).reshape(w.shape)


def kernel(x_prompt, x_sample, cache_mem_k, cache_mem_v, state_hgrn, cache_swa_k, cache_swa_v,
           mem_prompt, norm_mix, norm_ffn, norm_mem, w_in_hgrn, hgrn_lb_raw, hgrn_out_norm,
           w_in_swa, swa_q_norm, swa_k_norm, swa_sinks, w_mem_kv, mem_q_norm, mem_k_norm,
           w_out, w_ffn_in, w_ffn_out):
    depth, d = norm_mix.shape
    bp, lp, _ = x_prompt.shape
    bs, ls, _ = x_sample.shape
    nm = mem_prompt.shape[1]
    mw = MEM_HEADS * HEAD64
    kvw = SW_KV_HEADS * HEAD64
    tokw = d - mw
    tm = 512

    seg = np.arange(mw) // HEAD64
    bd = jnp.asarray((seg[:, None] == seg[None, :]).astype(np.float32) / HEAD64, BF16)
    tile4 = lambda a: jnp.tile(a.astype(F32), (1, mw // HEAD64))[:, None, :]

    sm = jax.nn.softmax(hgrn_lb_raw.astype(F32), axis=0)
    lbs = jnp.clip(jnp.cumsum(sm, axis=0) - sm[0], 0.0, LB_MAX)

    w_in_swa_b = w_in_swa.astype(BF16)
    w_in_swa_p = jnp.concatenate([_swa_regroup(w_in_swa_b[..., :tokw], -1),
                                  w_in_swa_b[..., tokw:]], axis=-1)
    w_in_hgrn_b = w_in_hgrn.astype(BF16)
    w_out_b = w_out.astype(BF16)
    w_out_b = jnp.stack([w_out_b[i] if i % 2 == 0 else
                         jnp.concatenate([_swa_regroup(w_out_b[i, :tokw], 0), w_out_b[i, tokw:]],
                                         axis=0)
                         for i in range(depth)])
    w1_b = w_ffn_in.astype(BF16)
    w2_b = w_ffn_out.astype(BF16)
    sinks_p = swa_sinks.astype(F32).reshape(-1, SW_KV_HEADS, SW_GROUP).transpose(0, 2, 1)
    sinks_p = sinks_p.reshape(-1, SW_KV_HEADS * SW_GROUP)
    mem_qn = tile4(mem_q_norm)
    swa_qn = tile4(swa_q_norm)
    swa_kn = tile4(swa_k_norm)

    mk_p, mv_p, mk_b = _mem_kv(mem_prompt, norm_mem[:, None, :], w_mem_kv.astype(BF16),
                               tile4(mem_k_norm), bd)

    to_t = lambda a: a.transpose(0, 1, 3, 4, 2).reshape(a.shape[0], a.shape[1], -1, a.shape[2])
    from_t = lambda a, heads: a.reshape(a.shape[:-2] + (heads, HEAD64, a.shape[-1])).transpose(
        *range(a.ndim - 2), a.ndim, a.ndim - 2, a.ndim - 1)
    cmk_t, cmv_t = to_t(cache_mem_k), to_t(cache_mem_v)
    csk_t, csv_t = to_t(cache_swa_k), to_t(cache_swa_v)

    xp = x_prompt.reshape(bp * lp, d)
    xs = x_sample.reshape(bs * ls, d)
    g_mix = norm_mix[:, None, :]
    g_ffn = norm_ffn[:, None, :]
    hg_p, swk_p, swv_p = [], [], []
    hg_s = sw_s = None
    for i in range(depth):
        j = i // 2
        if i % 2 == 0:
            lb = lbs[j][None, :]
            gn = hgrn_out_norm[j][None, :].astype(F32)
            pp = _norm_matmul(xp, g_mix, i, w_in_hgrn_b, j, tm).reshape(bp, lp, -1)
            tok_p, st_p, mo_p = _hgrn_prompt(pp, lb, gn, 512, mk_b, mv_p, i, mem_qn[i], bd)
            ps = _norm_matmul(xs, g_mix, i, w_in_hgrn_b, j, tm)
            tok_s, hg_s = _hgrn_sample(ps, state_hgrn, j, lb, gn, ls, hg_s)
            hg_p.append(st_p)
            cq_col = 4 * tokw // mw
        else:
            pp = _norm_matmul(xp, g_mix, i, w_in_swa_p, j, tm).reshape(bp, lp, -1)
            tok_p, kb, vb, mo_p = _swa_prompt(pp, swa_qn[j], swa_kn[j], sinks_p[j], bd, 4,
                                              mk_b, mv_p, i, mem_qn[i])
            swk_p.append(kb)
            swv_p.append(vb)
            ps = _norm_matmul(xs, g_mix, i, w_in_swa_p, j, tm)
            tok_s, *sw_s = _swa_sample(ps, csk_t, csv_t, j, swa_qn[j], swa_kn[j],
                                       sinks_p[j], bd, ls, 8, sw_s)
            cq_col = (tokw + 2 * kvw) // mw
        mo_s = _mem_attend(ps, cq_col, cmk_t, cmv_t, i, mem_qn[i], bd, 8, ls)
        xp = _mix_ffn(xp, tok_p.reshape(bp * lp, tokw), mo_p.reshape(bp * lp, mw),
                      w_out_b, g_ffn, w1_b, w2_b, i, tm)
        xs = _mix_ffn(xs, tok_s, mo_s, w_out_b, g_ffn, w1_b, w2_b, i, tm)

    return (xp.reshape(bp, lp, d), xs.reshape(bs, ls, d),
            from_t(mk_p, MEM_HEADS), from_t(mv_p, MEM_HEADS),
            jnp.stack(hg_p), hg_s,
            from_t(jnp.stack(swk_p), SW_KV_HEADS), from_t(jnp.stack(swv_p), SW_KV_HEADS),
            from_t(sw_s[0], SW_KV_HEADS), from_t(sw_s[1], SW_KV_HEADS))
```

```python
import functools

import numpy as np
import jax
import jax.numpy as jnp
from jax import lax
from jax.experimental import pallas as pl
from jax.experimental.pallas import tpu as pltpu

F32 = jnp.float32
BF16 = jnp.bfloat16
EPS = 1e-6
NEG = -1e30
LB_MAX = 0.999
LOG2E = 1.4426950408889634

HEAD64 = 64
MEM_HEADS = 4
SW_KV_HEADS = 4
SW_GROUP = 3
WINDOW = 128
HG_HEAD = 128
HG_ROWS = 64
SAMPLE_PAD = 8

V7X_VMEM_BYTES = 64 * 1024 * 1024
VMEM_LIMIT = V7X_VMEM_BYTES - 8 * 1024 * 1024

DENSE_ROWS = 512
HGRN_ROWS = 512
SWA_WINDOWS = 4
SWA_DECODE_SEQS = 8
MEM_DECODE_SEQS = 16


def _params(sem):
    return pltpu.CompilerParams(dimension_semantics=sem, vmem_limit_bytes=VMEM_LIMIT)


def _dot(a, b):
    return jnp.dot(a, b, preferred_element_type=F32)


def _dot_nt(a, b):
    return lax.dot_general(a, b, (((1,), (1,)), ((), ())), preferred_element_type=F32)


def _dot_tn(a, b):
    return lax.dot_general(a, b, (((0,), (0,)), ((), ())), preferred_element_type=F32)


def _sigmoid(x):
    return 1.0 / (1.0 + jnp.exp(-x))


def _sigmoid_pair(z):
    e = jnp.exp(-jnp.abs(z))
    r = 1.0 / (1.0 + e)
    er = e * r
    pos = z >= 0
    return jnp.where(pos, r, er), jnp.where(pos, er, r)


def _rms_rows(x, g):
    ms = jnp.mean(x * x, axis=-1, keepdims=True)
    return x * lax.rsqrt(ms + EPS) * g


def _head_rms64(x, g, bd):
    ms = _dot((x * x).astype(BF16), bd)
    return x * lax.rsqrt(ms + EPS) * g


def _head_mask(width, h):
    lane = lax.broadcasted_iota(jnp.int32, (1, width), 1)
    return ((lane >> 6) == h).astype(F32)


def _norm_matmul_kernel(x_ref, g_ref, w_ref, o_ref):
    h = _rms_rows(x_ref[...], g_ref[...]).astype(BF16)
    o_ref[...] = _dot(h, w_ref[...])


def _norm_matmul(x, g, g_idx, w, w_idx, tm):
    m, d = x.shape
    n = w.shape[2]
    tm = min(tm, m)
    return pl.pallas_call(
        _norm_matmul_kernel,
        grid=(m // tm,),
        in_specs=[
            pl.BlockSpec((tm, d), lambda i: (i, 0)),
            pl.BlockSpec((None, 1, d), lambda i: (g_idx, 0, 0)),
            pl.BlockSpec((None, d, n), lambda i: (w_idx, 0, 0)),
        ],
        out_specs=pl.BlockSpec((tm, n), lambda i: (i, 0)),
        out_shape=jax.ShapeDtypeStruct((m, n), F32),
        compiler_params=_params(("parallel",)),
        name="norm_matmul",
    )(x, g, w)


def _mix_ffn_kernel(x_ref, tok_ref, mo_ref, wo_ref, g_ref, w1_ref, w2_ref, o_ref, act_ref,
                    *, ff, chunk):
    tw = tok_ref.shape[1]
    x1 = (x_ref[...] + _dot(tok_ref[...], wo_ref[0:tw, :])
          + _dot(mo_ref[...], wo_ref[tw:, :]))
    o_ref[...] = x1
    h = _rms_rows(x1, g_ref[...]).astype(BF16)
    for c0 in range(0, ff, chunk):
        g = _dot(h, w1_ref[:, c0:c0 + chunk])
        u = _dot(h, w1_ref[:, ff + c0:ff + c0 + chunk])
        act_ref[:, c0:c0 + chunk] = (g * _sigmoid(g) * u).astype(BF16)
    o_ref[...] += _dot(act_ref[...], w2_ref[...])


def _mix_ffn(x, tok, mo, wo, g, w1, w2, layer, tm):
    m, d = x.shape
    ff = w2.shape[1]
    tm = min(tm, m)
    tw, mw = tok.shape[1], mo.shape[1]
    const = lambda i: (layer, 0, 0)
    return pl.pallas_call(
        functools.partial(_mix_ffn_kernel, ff=ff, chunk=256),
        grid=(m // tm,),
        in_specs=[
            pl.BlockSpec((tm, d), lambda i: (i, 0)),
            pl.BlockSpec((tm, tw), lambda i: (i, 0)),
            pl.BlockSpec((tm, mw), lambda i: (i, 0)),
            pl.BlockSpec((None, tw + mw, d), const, pipeline_mode=pl.Buffered(1)),
            pl.BlockSpec((None, 1, d), const),
            pl.BlockSpec((None, d, 2 * ff), const, pipeline_mode=pl.Buffered(1)),
            pl.BlockSpec((None, ff, d), const, pipeline_mode=pl.Buffered(1)),
        ],
        out_specs=pl.BlockSpec((tm, d), lambda i: (i, 0)),
        out_shape=jax.ShapeDtypeStruct((m, d), F32),
        scratch_shapes=[pltpu.VMEM((tm, ff), BF16)],
        compiler_params=_params(("parallel",)),
        name="mix_ffn",
    )(x, tok, mo, wo, g, w1, w2)


def _mem_kv_kernel(mem_ref, g_ref, w_ref, kn_ref, bd_ref, k_ref, v_ref, kb_ref):
    nb, nm, d = mem_ref.shape
    kw = k_ref.shape[1]
    h = _rms_rows(mem_ref[...].reshape(nb * nm, d), g_ref[...]).astype(BF16)
    kv = _dot(h, w_ref[...])
    k_all = _head_rms64(kv[:, :kw], kn_ref[...], bd_ref[...])
    for b in range(nb):
        k = k_all[b * nm:(b + 1) * nm]
        k_ref[b] = k.T
        v_ref[b] = kv[b * nm:(b + 1) * nm, kw:].T
        kb_ref[b] = jnp.concatenate([(k * _head_mask(kw, hd)).astype(BF16)
                                     for hd in range(MEM_HEADS)], axis=0)


def _mem_kv(mem, g, w, kn, bd):
    depth = w.shape[0]
    b, nm, d = mem.shape
    kw = w.shape[2] // 2
    out = jax.ShapeDtypeStruct((depth, b, kw, nm), F32)
    out_b = jax.ShapeDtypeStruct((depth, b, MEM_HEADS * nm, kw), BF16)
    return pl.pallas_call(
        _mem_kv_kernel,
        grid=(depth,),
        in_specs=[
            pl.BlockSpec((b, nm, d), lambda i: (0, 0, 0)),
            pl.BlockSpec((None, 1, d), lambda i: (i, 0, 0)),
            pl.BlockSpec((None, d, 2 * kw), lambda i: (i, 0, 0)),
            pl.BlockSpec((None, 1, kw), lambda i: (i, 0, 0)),
            pl.BlockSpec((kw, kw), lambda i: (0, 0)),
        ],
        out_specs=[pl.BlockSpec((None, b, kw, nm), lambda i: (i, 0, 0, 0))] * 2
        + [pl.BlockSpec((None, b, MEM_HEADS * nm, kw), lambda i: (i, 0, 0, 0))],
        out_shape=[out, out, out_b],
        compiler_params=_params(("parallel",)),
        name="mem_kv",
    )(mem, g, w, kn, bd)


def _mem_attend_kernel(q_ref, k_ref, v_ref, qn_ref, bd_ref, o_ref, *, nl):
    nb = k_ref.shape[0]
    w = q_ref.shape[1]
    tq = SAMPLE_PAD
    masks = [_head_mask(w, h) for h in range(MEM_HEADS)]
    qn_all = _head_rms64(q_ref[...], qn_ref[...], bd_ref[...]) * HEAD64 ** -0.5
    zpad = jnp.zeros((tq - nl, w), F32)
    outs = []
    for b in range(nb):
        qn = jnp.concatenate([qn_all[b * nl:(b + 1) * nl], zpad], axis=0)
        kt = k_ref[b].astype(BF16)
        vt = v_ref[b].astype(BF16)
        qbd = jnp.concatenate([(qn * m).astype(BF16) for m in masks], axis=0)
        s = _dot(qbd, kt)
        p = jnp.exp(s - jnp.max(s, axis=-1, keepdims=True))
        o = _dot_nt(p.astype(BF16), vt) / jnp.sum(p, axis=-1, keepdims=True)
        acc = o[0:tq] * masks[0]
        for h in range(1, MEM_HEADS):
            acc += o[h * tq:(h + 1) * tq] * masks[h]
        outs.append(acc[0:nl])
    o_ref[...] = jnp.concatenate(outs, axis=0).astype(o_ref.dtype)


def _mem_attend_rows(q, kb, vt, qn, bd):
    nm = vt.shape[1]
    qn = _head_rms64(q, qn, bd) * (HEAD64 ** -0.5 * LOG2E)
    st = _dot_nt(kb, qn.astype(BF16))
    vt = vt.astype(BF16)
    outs = []
    for h in range(MEM_HEADS):
        s = st[h * nm:(h + 1) * nm]
        p = jnp.exp2(s - jnp.max(s, axis=0, keepdims=True))
        den = jnp.sum(p, axis=0, keepdims=True)
        outs.append(_dot(vt[h * HEAD64:(h + 1) * HEAD64, :], p.astype(BF16)) / den)
    return jnp.concatenate(outs, axis=0).T


def _mem_prompt_specs(mkb, mvt, layer, col, tq):
    w, nm = mvt.shape[2], mvt.shape[3]
    ins = [
        pl.BlockSpec((None, tq, w), lambda i, t: (i, t, col)),
        pl.BlockSpec((None, None, MEM_HEADS * nm, w), lambda i, t: (layer, i, 0, 0)),
        pl.BlockSpec((None, None, w, nm), lambda i, t: (layer, i, 0, 0)),
        pl.BlockSpec((1, w), lambda i, t: (0, 0)),
        pl.BlockSpec((w, w), lambda i, t: (0, 0)),
    ]
    return ins, pl.BlockSpec((None, tq, w), lambda i, t: (i, t, 0))


def _mem_attend(q, col, mkt, mvt, layer, qn, bd, nb, nl):
    bsz = mkt.shape[1]
    w, nm = mkt.shape[2], mkt.shape[3]
    return pl.pallas_call(
        functools.partial(_mem_attend_kernel, nl=nl),
        grid=(bsz // nb,),
        in_specs=[
            pl.BlockSpec((nb * nl, w), lambda i: (i, col)),
            pl.BlockSpec((None, nb, w, nm), lambda i: (layer, i, 0, 0)),
            pl.BlockSpec((None, nb, w, nm), lambda i: (layer, i, 0, 0)),
            pl.BlockSpec((1, w), lambda i: (0, 0)),
            pl.BlockSpec((w, w), lambda i: (0, 0)),
        ],
        out_specs=pl.BlockSpec((nb * nl, w), lambda i: (i, 0)),
        out_shape=jax.ShapeDtypeStruct((bsz * nl, w), BF16),
        compiler_params=_params(("parallel",)),
        name="mem_attend",
    )(q, mkt, mvt, qn, bd)


def _swa_prompt_kernel(q_ref, kc_ref, vc_ref, kp_ref, vp_ref, qn_ref, kn_ref, sink_ref, bd_ref,
                       cq_ref, mkb_ref, mvt_ref, mqn_ref, mbd_ref,
                       tok_ref, kout_ref, vout_ref, mo_ref):
    n = pl.program_id(1)
    bd = bd_ref[...]
    mo_ref[...] = _mem_attend_rows(cq_ref[...], mkb_ref[...], mvt_ref[...], mqn_ref[...],
                                   mbd_ref[...]).astype(mo_ref.dtype)
    w = kc_ref.shape[-1]
    nwin = kc_ref.shape[0] // WINDOW
    kk = _head_rms64(jnp.concatenate([kp_ref[...], kc_ref[...]], axis=0), kn_ref[...], bd)
    vv = jnp.concatenate([vp_ref[...], vc_ref[...]], axis=0)
    kb = kk.astype(BF16)
    vvt = vv.T.astype(BF16)
    r = lax.broadcasted_iota(jnp.int32, (2 * WINDOW, WINDOW), 0)
    qi = lax.broadcasted_iota(jnp.int32, (2 * WINDOW, WINDOW), 1)
    band = (r > qi) & (r <= qi + WINDOW)
    masks = [_head_mask(w, k).astype(BF16) for k in range(SW_KV_HEADS)]
    for i in range(nwin):
        rows = slice(i * WINDOW, (i + 1) * WINDOW)
        keys = slice(i * WINDOW, (i + 2) * WINDOW)
        valid = band if i else band & ((r >= WINDOW) | (n > 0))
        pieces = []
        for g in range(SW_GROUP):
            qg = _head_rms64(q_ref[rows, g * w:(g + 1) * w], qn_ref[...], bd)
            qg = (qg * (HEAD64 ** -0.5 * LOG2E)).astype(BF16)
            pieces += [qg * m for m in masks]
        st = _dot_nt(kb[keys], jnp.concatenate(pieces, axis=0))
        outs = []
        for h in range(SW_GROUP * SW_KV_HEADS):
            k = h % SW_KV_HEADS
            s = jnp.where(valid, st[:, h * WINDOW:(h + 1) * WINDOW], NEG)
            sink = sink_ref[h] * LOG2E
            m = jnp.maximum(jnp.max(s, axis=0, keepdims=True), sink)
            p = jnp.exp2(s - m)
            den = jnp.sum(p, axis=0, keepdims=True) + jnp.exp2(sink - m)
            outs.append(_dot(vvt[k * HEAD64:(k + 1) * HEAD64, keys], p.astype(BF16)) / den)
        tok_ref[rows, :] = jnp.concatenate(outs, axis=0).T.astype(tok_ref.dtype)

    @pl.when(n == pl.num_programs(1) - 1)
    def _():
        kout_ref[...] = kk[nwin * WINDOW:].T
        vout_ref[...] = vv[nwin * WINDOW:].T


def _swa_prompt(p, qn, kn, sinks, bd, nwin, mkb, mvt, layer, mqn):
    b, l, _ = p.shape
    w = SW_KV_HEADS * HEAD64
    nq = SW_GROUP
    tq = nwin * WINDOW
    cache = jax.ShapeDtypeStruct((b, w, WINDOW), F32)
    prev = lambda c: (lambda i, n: (i, jnp.maximum(n * nwin - 1, 0), c))
    mem_in, mem_out = _mem_prompt_specs(mkb, mvt, layer, nq + 2, tq)
    return pl.pallas_call(
        _swa_prompt_kernel,
        grid=(b, l // tq),
        in_specs=[
            pl.BlockSpec((None, tq, nq * w), lambda i, n: (i, n, 0)),
            pl.BlockSpec((None, tq, w), lambda i, n: (i, n, nq)),
            pl.BlockSpec((None, tq, w), lambda i, n: (i, n, nq + 1)),
            pl.BlockSpec((None, WINDOW, w), prev(nq)),
            pl.BlockSpec((None, WINDOW, w), prev(nq + 1)),
            pl.BlockSpec((1, w), lambda i, n: (0, 0)),
            pl.BlockSpec((1, w), lambda i, n: (0, 0)),
            pl.BlockSpec(memory_space=pltpu.SMEM),
            pl.BlockSpec((w, w), lambda i, n: (0, 0)),
        ] + mem_in,
        out_specs=[
            pl.BlockSpec((None, tq, nq * w), lambda i, n: (i, n, 0)),
            pl.BlockSpec((None, w, WINDOW), lambda i, n: (i, 0, 0)),
            pl.BlockSpec((None, w, WINDOW), lambda i, n: (i, 0, 0)),
            mem_out,
        ],
        out_shape=[jax.ShapeDtypeStruct((b, l, nq * w), BF16), cache, cache,
                   jax.ShapeDtypeStruct((b, l, mvt.shape[2]), BF16)],
        compiler_params=_params(("parallel", "arbitrary")),
        name="swa_prompt",
    )(p, p, p, p, p, qn, kn, sinks, bd, p, mkb, mvt, mqn, bd)


def _swa_sample_kernel(q_ref, kn_ref_in, vn_ref_in, ck_ref, cv_ref, qn_ref, kn_ref, sink_ref,
                       bd_ref, *rest, nl, nprev):
    if nprev:
        pk_ref, pv_ref, tok_ref, kstack_ref, vstack_ref = rest
        kstack_ref[0:nprev] = pk_ref[...]
        vstack_ref[0:nprev] = pv_ref[...]
    else:
        tok_ref, kstack_ref, vstack_ref = rest
    kout_ref = kstack_ref.at[nprev]
    vout_ref = vstack_ref.at[nprev]
    nb = ck_ref.shape[0]
    w = ck_ref.shape[1]
    rows = SAMPLE_PAD
    nkeys = 2 * WINDOW
    bd = bd_ref[...]
    nh = SW_GROUP * SW_KV_HEADS
    masks = [_head_mask(w, k) for k in range(SW_KV_HEADS)]
    row = lax.broadcasted_iota(jnp.int32, (nh * rows, nkeys), 0)
    r = lax.broadcasted_iota(jnp.int32, (nh * rows, nkeys), 1)
    j = row & (rows - 1)
    valid = (r > j) & (r <= j + WINDOW)
    hrow = lax.broadcasted_iota(jnp.int32, (nh * rows, 1), 0) >> 3
    sink = jnp.zeros((nh * rows, 1), F32)
    for h in range(nh):
        sink = jnp.where(hrow == h, sink_ref[h], sink)
    zpad = jnp.zeros((WINDOW - nl, w), F32)
    zq = jnp.zeros((rows - nl, w), F32)
    is_new = lax.broadcasted_iota(jnp.int32, (1, WINDOW), 1) < nl
    knew_all = _head_rms64(kn_ref_in[...], kn_ref[...], bd)
    vnew_all = vn_ref_in[...]
    q_all = [_head_rms64(q_ref[:, g * w:(g + 1) * w], qn_ref[...], bd) * HEAD64 ** -0.5
             for g in range(SW_GROUP)]
    toks = []
    for b in range(nb):
        mine = slice(b * nl, (b + 1) * nl)
        ck, cv = ck_ref[b], cv_ref[b]
        knt = jnp.concatenate([knew_all[mine], zpad], axis=0).T
        vnt = jnp.concatenate([vnew_all[mine], zpad], axis=0).T
        kt = jnp.concatenate([ck, knt], axis=1)
        vt = jnp.concatenate([cv, vnt], axis=1)
        kout_ref[b] = pltpu.roll(jnp.where(is_new, knt, ck), WINDOW - nl, axis=1)
        vout_ref[b] = pltpu.roll(jnp.where(is_new, vnt, cv), WINDOW - nl, axis=1)
        pieces = []
        for g in range(SW_GROUP):
            qg = jnp.concatenate([q_all[g][mine], zq], axis=0)
            pieces += [(qg * masks[k]).astype(BF16) for k in range(SW_KV_HEADS)]
        s = _dot(jnp.concatenate(pieces, axis=0), kt.astype(BF16))
        s = jnp.where(valid, s, NEG)
        m = jnp.maximum(jnp.max(s, axis=-1, keepdims=True), sink)
        p = jnp.exp(s - m)
        den = jnp.sum(p, axis=-1, keepdims=True) + jnp.exp(sink - m)
        o = _dot_nt(p.astype(BF16), vt.astype(BF16)) / den
        accs = []
        for g in range(SW_GROUP):
            acc = jnp.zeros((rows, w), F32)
            for k in range(SW_KV_HEADS):
                h = g * SW_KV_HEADS + k
                acc += o[h * rows:(h + 1) * rows] * masks[k]
            accs.append(acc[0:nl])
        toks.append(jnp.concatenate(accs, axis=1))
    tok_ref[...] = jnp.concatenate(toks, axis=0).astype(tok_ref.dtype)


def _swa_sample(p, ckt, cvt, layer, qn, kn, sinks, bd, nl, nb, prev):
    b = ckt.shape[1]
    w = ckt.shape[2]
    nq = SW_GROUP
    nprev = 0 if prev is None else prev[0].shape[0]
    cache = jax.ShapeDtypeStruct((nprev + 1, b, w, WINDOW), F32)
    c4 = lambda i: (layer, i, 0, 0)
    stack = lambda n: pl.BlockSpec((n, nb, w, WINDOW), lambda i: (0, i, 0, 0))
    return pl.pallas_call(
        functools.partial(_swa_sample_kernel, nl=nl, nprev=nprev),
        grid=(b // nb,),
        in_specs=[
            pl.BlockSpec((nb * nl, nq * w), lambda i: (i, 0)),
            pl.BlockSpec((nb * nl, w), lambda i: (i, nq)),
            pl.BlockSpec((nb * nl, w), lambda i: (i, nq + 1)),
            pl.BlockSpec((None, nb, w, WINDOW), c4),
            pl.BlockSpec((None, nb, w, WINDOW), c4),
            pl.BlockSpec((1, w), lambda i: (0, 0)),
            pl.BlockSpec((1, w), lambda i: (0, 0)),
            pl.BlockSpec(memory_space=pltpu.SMEM),
            pl.BlockSpec((w, w), lambda i: (0, 0)),
        ] + ([stack(nprev)] * 2 if nprev else []),
        out_specs=[
            pl.BlockSpec((nb * nl, nq * w), lambda i: (i, 0)),
            stack(nprev + 1),
            stack(nprev + 1),
        ],
        out_shape=[jax.ShapeDtypeStruct((b * nl, nq * w), BF16), cache, cache],
        compiler_params=_params(("parallel",)),
        name="swa_sample",
    )(p, p, p, ckt, cvt, qn, kn, sinks, bd, *(() if prev is None else prev))


def _hgrn_consts(rows, group):
    t = np.arange(rows)
    loc, run = t % group, t // group
    same = run[:, None] == run[None, :]
    r = t[None, :]
    mats = [same & (r <= t[:, None]), same & (r > t[:, None])]
    masks = []
    h = group // 2
    while h >= 1:
        par = loc // (2 * h)
        right = (loc % (2 * h)) >= h
        bnd = (run * group + par * 2 * h + h - 1)[:, None]
        q_side = right[:, None] & (r > bnd) & (r <= t[:, None])
        k_side = (~right)[:, None] & (r > t[:, None]) & (r <= bnd)
        mats.append(q_side | k_side)
        masks.append(same & (par[:, None] == par[None, :]) & right[:, None] & (~right)[None, :])
        h //= 2
    masks.append(np.eye(rows, dtype=bool))
    w = np.concatenate(mats, axis=0).astype(np.float32)
    w2 = np.concatenate([w, w], axis=1)
    return jnp.asarray(w2, BF16), jnp.asarray(np.stack(masks).astype(np.float32)), len(masks) - 1


def _hgrn_gates(q_raw, z, lb):
    sp, sn = _sigmoid_pair(z)
    log2f = jnp.log2(lb + (1.0 - lb) * sp)
    kf = (1.0 - lb) * sn
    q = q_raw * _sigmoid(q_raw) * HG_HEAD ** -0.5
    return q, kf, log2f


def _hgrn_decays(log2f, w2_ref):
    hi = log2f.astype(BF16)
    lo = (log2f - hi.astype(F32)).astype(BF16)
    return jnp.exp2(_dot(w2_ref[...], jnp.concatenate([hi, lo], axis=0)))


def _hgrn_intra(q, kf, v, e, m_ref, nlev):
    rows = q.shape[0]
    att = m_ref[nlev] * _dot_nt(q.astype(BF16), kf.astype(BF16))
    for l in range(nlev):
        el = e[(2 + l) * rows:(3 + l) * rows]
        att += m_ref[l] * _dot_nt((q * el).astype(BF16), (kf * el).astype(BF16))
    return _dot(att.astype(BF16), v.astype(BF16))


def _hgrn_out(o, g, gn):
    ms = jnp.mean(o * o, axis=-1, keepdims=True)
    return (o * lax.rsqrt(ms + EPS) * gn * (g * _sigmoid(g))).astype(BF16)


def _hgrn_prompt_kernel(q_ref, f_ref, i_ref, g_ref, lb_ref, gn_ref, w2_ref, m_ref,
                        cq_ref, mkb_ref, mvt_ref, mqn_ref, mbd_ref,
                        tok_ref, st_ref, mo_ref, s_scr, *, nlev):
    t = pl.program_id(1)

    @pl.when(t == 0)
    def _():
        s_scr[...] = jnp.zeros_like(s_scr)

    mo_ref[...] = _mem_attend_rows(cq_ref[...], mkb_ref[...], mvt_ref[...], mqn_ref[...],
                                   mbd_ref[...]).astype(mo_ref.dtype)

    lb = lb_ref[...]
    gn = gn_ref[...]
    rows = HG_ROWS
    nh = q_ref.shape[1] // HG_HEAD

    def chunk(c, st):
        sl = pl.ds(pl.multiple_of(c * rows, rows), rows)
        v = i_ref[sl, :]
        g = g_ref[sl, :]
        q, kf, log2f = _hgrn_gates(q_ref[sl, :], f_ref[sl, :], lb)
        e = _hgrn_decays(log2f, w2_ref)
        qt = (q * e[0:rows]).astype(BF16)
        kh = (kf * e[rows:2 * rows]).astype(BF16)
        vb = v.astype(BF16)
        stb = st.astype(BF16)
        qb = q.astype(BF16)
        kb = kf.astype(BF16)
        ql, kl = [qb], [kb]
        rowi = lax.broadcasted_iota(jnp.int32, (rows, 1), 0)
        for l in range(nlev):
            half = rows >> (l + 1)
            if half >= 8:
                base = jnp.concatenate([(q if (r0 // half) & 1 else kf)[r0:r0 + half]
                                        for r0 in range(0, rows, half)], axis=0)
            else:
                base = jnp.where((rowi & (2 * half - 1)) >= half, q, kf)
            x = (base * e[(2 + l) * rows:(3 + l) * rows]).astype(BF16)
            ql.append(x)
            kl.append(x)
        mk = [m_ref[nlev]] + [m_ref[l] for l in range(nlev)]
        zr = jnp.zeros((rows, HG_HEAD), BF16)
        zs = jnp.zeros((HG_HEAD, HG_HEAD), BF16)

        def bdiag(a, b, z):
            return jnp.concatenate([jnp.concatenate([a, z], axis=1),
                                    jnp.concatenate([z, b], axis=1)], axis=0)

        upd = []
        for p in range(nh // 2):
            h0 = slice(2 * p * HG_HEAD, (2 * p + 1) * HG_HEAD)
            h1 = slice((2 * p + 1) * HG_HEAD, (2 * p + 2) * HG_HEAD)
            pr = slice(2 * p * HG_HEAD, (2 * p + 2) * HG_HEAD)
            att = jnp.zeros((rows, 2 * rows), F32)
            for a, b, m in zip(ql, kl, mk):
                att += m * _dot_nt(a[:, pr], bdiag(b[:, h0], b[:, h1], zr))
            o = _dot(att.astype(BF16), bdiag(vb[:, h0], vb[:, h1], zr))
            o += _dot_nt(qt[:, pr], bdiag(stb[:, h0], stb[:, h1], zs))
            for hs in (h0, h1):
                lo = hs.start - pr.start
                tok_ref[sl, hs] = _hgrn_out(o[:, lo:lo + HG_HEAD], g[:, hs], gn[:, hs])
                upd.append(_dot_tn(vb[:, hs], kh[:, hs]))
        return st * e[rows - 1:rows, :] + jnp.concatenate(upd, axis=1)

    st = lax.fori_loop(0, q_ref.shape[0] // rows, chunk, s_scr[...], unroll=True)
    s_scr[...] = st

    @pl.when(t == pl.num_programs(1) - 1)
    def _():
        for h in range(nh):
            st_ref[h] = st[:, h * HG_HEAD:(h + 1) * HG_HEAD].T


def _hgrn_prompt(p, lb, gn, tb, mkb, mvt, layer, mqn, bd):
    b, l, _ = p.shape
    tw = lb.shape[1]
    nh = tw // HG_HEAD
    mw = mvt.shape[2]
    w2, masks, nlev = _hgrn_consts(HG_ROWS, HG_ROWS)
    masks = jnp.tile(masks, (1, 1, 2))
    sec = lambda s: pl.BlockSpec((None, tb, tw), lambda i, t: (i, t, s))
    mem_in, mem_out = _mem_prompt_specs(mkb, mvt, layer, 4 * tw // mw, tb)
    return pl.pallas_call(
        functools.partial(_hgrn_prompt_kernel, nlev=nlev),
        grid=(b, l // tb),
        in_specs=[
            sec(0), sec(1), sec(2), sec(3),
            pl.BlockSpec((1, tw), lambda i, t: (0, 0)),
            pl.BlockSpec((1, tw), lambda i, t: (0, 0)),
            pl.BlockSpec(w2.shape, lambda i, t: (0, 0)),
            pl.BlockSpec(masks.shape, lambda i, t: (0, 0, 0)),
        ] + mem_in,
        out_specs=[
            pl.BlockSpec((None, tb, tw), lambda i, t: (i, t, 0)),
            pl.BlockSpec((None, nh, HG_HEAD, HG_HEAD), lambda i, t: (i, 0, 0, 0)),
            mem_out,
        ],
        out_shape=[jax.ShapeDtypeStruct((b, l, tw), BF16),
                   jax.ShapeDtypeStruct((b, nh, HG_HEAD, HG_HEAD), F32),
                   jax.ShapeDtypeStruct((b, l, mw), BF16)],
        scratch_shapes=[pltpu.VMEM((HG_HEAD, tw), F32)],
        compiler_params=_params(("parallel", "arbitrary")),
        name="hgrn_prompt",
    )(p, p, p, p, lb, gn, w2, masks, p, mkb, mvt, mqn, bd)


def _hgrn_sample_kernel(q_ref, f_ref, i_ref, g_ref, s0_ref, lb_ref, gn_ref, w2_ref, m_ref,
                        *rest, nlev, nl, nprev):
    if nprev:
        prev_ref, tok_ref, stack_ref = rest
        stack_ref[0:nprev] = prev_ref[...]
    else:
        tok_ref, stack_ref = rest
    st_ref = stack_ref.at[nprev]
    rows = q_ref.shape[0]
    nb = rows // nl
    shift = nl.bit_length() - 1
    rowb = lax.broadcasted_iota(jnp.int32, (rows, 1), 0) >> shift
    colb = lax.broadcasted_iota(jnp.int32, (rows, nb * HG_HEAD), 1) >> 7
    v_all = i_ref[...]
    q_all, kf_all, log2f = _hgrn_gates(q_ref[...], f_ref[...], lb_ref[...])
    e_all = _hgrn_decays(log2f, w2_ref)
    g_all = g_ref[...]
    gn_all = gn_ref[...]
    for hh in range(q_ref.shape[1] // HG_HEAD):
        hs = slice(hh * HG_HEAD, (hh + 1) * HG_HEAD)
        q, kf, v, e = q_all[:, hs], kf_all[:, hs], v_all[:, hs], e_all[:, hs]
        o_intra = _hgrn_intra(q, kf, v, e, m_ref, nlev)
        e_cum = e[0:rows]
        qt = q * e_cum
        kh = kf * e[rows:2 * rows]
        s0cat = jnp.concatenate([s0_ref[b, hh].astype(BF16) for b in range(nb)], axis=1)
        full = _dot(qt.astype(BF16), s0cat)
        o = o_intra
        for b in range(nb):
            o += jnp.where(rowb == b, full[:, b * HG_HEAD:(b + 1) * HG_HEAD], 0.0)
        tok_ref[:, hs] = _hgrn_out(o, g_all[:, hs], gn_all[:, hs])
        xt = jnp.concatenate([kh, e_cum], axis=0).T
        vbd = jnp.where(rowb == colb, jnp.concatenate([v] * nb, axis=1), 0.0).astype(BF16)
        upd = _dot(xt[:, 0:rows].astype(BF16), vbd)
        for b in range(nb):
            last = rows + b * nl + nl - 1
            st_ref[b, hh] = (s0_ref[b, hh] * xt[:, last:last + 1]
                             + upd[:, b * HG_HEAD:(b + 1) * HG_HEAD])


def _hgrn_sample(p, s0, layer, lb, gn, nl, prev):
    rows = HG_ROWS
    nb = rows // nl
    b, nh = s0.shape[1], s0.shape[2]
    hp = 2
    hw = hp * HG_HEAD
    nprev = 0 if prev is None else prev.shape[0]
    w2, masks, nlev = _hgrn_consts(rows, nl)
    sec = lambda s: pl.BlockSpec((rows, hw), lambda i, h: (i, s * (nh // hp) + h))
    stack = lambda n: pl.BlockSpec((n, nb, hp, HG_HEAD, HG_HEAD), lambda i, h: (0, i, h, 0, 0))
    s0_spec = pl.BlockSpec((None, nb, hp, HG_HEAD, HG_HEAD), lambda i, h: (layer, i, h, 0, 0))
    return pl.pallas_call(
        functools.partial(_hgrn_sample_kernel, nlev=nlev, nl=nl, nprev=nprev),
        grid=(b // nb, nh // hp),
        in_specs=[
            sec(0), sec(1), sec(2), sec(3), s0_spec,
            pl.BlockSpec((1, hw), lambda i, h: (0, h)),
            pl.BlockSpec((1, hw), lambda i, h: (0, h)),
            pl.BlockSpec(w2.shape, lambda i, h: (0, 0)),
            pl.BlockSpec(masks.shape, lambda i, h: (0, 0, 0)),
        ] + ([stack(nprev)] if nprev else []),
        out_specs=[pl.BlockSpec((rows, hw), lambda i, h: (i, h)), stack(nprev + 1)],
        out_shape=[jax.ShapeDtypeStruct((b * nl, nh * HG_HEAD), BF16),
                   jax.ShapeDtypeStruct((nprev + 1,) + s0.shape[1:], F32)],
        compiler_params=_params(("parallel", "parallel")),
        name="hgrn_sample",
    )(p, p, p, p, s0, lb, gn, w2, masks, *(() if prev is None else (prev,)))


def _swa_regroup(w, axis):
    axis %= w.ndim
    shape = w.shape[:axis] + (SW_KV_HEADS, SW_GROUP, HEAD64) + w.shape[axis + 1:]
    return jnp.swapaxes(w.reshape(shape), axis, axis + 1).reshape(w.shape)


def kernel(x_prompt, x_sample, cache_mem_k, cache_mem_v, state_hgrn, cache_swa_k, cache_swa_v,
           mem_prompt, norm_mix, norm_ffn, norm_mem, w_in_hgrn, hgrn_lb_raw, hgrn_out_norm,
           w_in_swa, swa_q_norm, swa_k_norm, swa_sinks, w_mem_kv, mem_q_norm, mem_k_norm,
           w_out, w_ffn_in, w_ffn_out):
    depth, d = norm_mix.shape
    bp, lp, _ = x_prompt.shape
    bs, ls, _ = x_sample.shape
    nm = mem_prompt.shape[1]
    mw = MEM_HEADS * HEAD64
    kvw = SW_KV_HEADS * HEAD64
    tokw = d - mw
    tm = DENSE_ROWS

    seg = np.arange(mw) // HEAD64
    bd = jnp.asarray((seg[:, None] == seg[None, :]).astype(np.float32) / HEAD64, BF16)
    tile4 = lambda a: jnp.tile(a.astype(F32), (1, mw // HEAD64))[:, None, :]

    sm = jax.nn.softmax(hgrn_lb_raw.astype(F32), axis=0)
    lbs = jnp.clip(jnp.cumsum(sm, axis=0) - sm[0], 0.0, LB_MAX)

    w_in_swa_b = w_in_swa.astype(BF16)
    w_in_swa_p = jnp.concatenate([_swa_regroup(w_in_swa_b[..., :tokw], -1),
                                  w_in_swa_b[..., tokw:]], axis=-1)
    w_in_hgrn_b = w_in_hgrn.astype(BF16)
    w_out_b = w_out.astype(BF16)
    w_out_b = jnp.stack([w_out_b[i] if i % 2 == 0 else
                         jnp.concatenate([_swa_regroup(w_out_b[i, :tokw], 0), w_out_b[i, tokw:]],
                                         axis=0)
                         for i in range(depth)])
    w1_b = w_ffn_in.astype(BF16)
    w2_b = w_ffn_out.astype(BF16)
    sinks_p = swa_sinks.astype(F32).reshape(-1, SW_KV_HEADS, SW_GROUP).transpose(0, 2, 1)
    sinks_p = sinks_p.reshape(-1, SW_KV_HEADS * SW_GROUP)
    mem_qn = tile4(mem_q_norm)
    swa_qn = tile4(swa_q_norm)
    swa_kn = tile4(swa_k_norm)

    mk_p, mv_p, mk_b = _mem_kv(mem_prompt, norm_mem[:, None, :], w_mem_kv.astype(BF16),
                               tile4(mem_k_norm), bd)

    to_t = lambda a: a.transpose(0, 1, 3, 4, 2).reshape(a.shape[0], a.shape[1], -1, a.shape[2])
    from_t = lambda a, heads: a.reshape(a.shape[:-2] + (heads, HEAD64, a.shape[-1])).transpose(
        *range(a.ndim - 2), a.ndim, a.ndim - 2, a.ndim - 1)
    cmk_t, cmv_t = to_t(cache_mem_k), to_t(cache_mem_v)
    csk_t, csv_t = to_t(cache_swa_k), to_t(cache_swa_v)

    xp = x_prompt.reshape(bp * lp, d)
    xs = x_sample.reshape(bs * ls, d)
    g_mix = norm_mix[:, None, :]
    g_ffn = norm_ffn[:, None, :]
    hg_p, swk_p, swv_p = [], [], []
    hg_s = sw_s = None
    for i in range(depth):
        j = i // 2
        if i % 2 == 0:
            lb = lbs[j][None, :]
            gn = hgrn_out_norm[j][None, :].astype(F32)
            pp = _norm_matmul(xp, g_mix, i, w_in_hgrn_b, j, tm).reshape(bp, lp, -1)
            tok_p, st_p, mo_p = _hgrn_prompt(pp, lb, gn, HGRN_ROWS, mk_b, mv_p, i, mem_qn[i], bd)
            ps = _norm_matmul(xs, g_mix, i, w_in_hgrn_b, j, tm)
            tok_s, hg_s = _hgrn_sample(ps, state_hgrn, j, lb, gn, ls, hg_s)
            hg_p.append(st_p)
            cq_col = 4 * tokw // mw
        else:
            pp = _norm_matmul(xp, g_mix, i, w_in_swa_p, j, tm).reshape(bp, lp, -1)
            tok_p, kb, vb, mo_p = _swa_prompt(pp, swa_qn[j], swa_kn[j], sinks_p[j], bd,
                                              SWA_WINDOWS, mk_b, mv_p, i, mem_qn[i])
            swk_p.append(kb)
            swv_p.append(vb)
            ps = _norm_matmul(xs, g_mix, i, w_in_swa_p, j, tm)
            tok_s, *sw_s = _swa_sample(ps, csk_t, csv_t, j, swa_qn[j], swa_kn[j],
                                       sinks_p[j], bd, ls, SWA_DECODE_SEQS, sw_s)
            cq_col = (tokw + 2 * kvw) // mw
        mo_s = _mem_attend(ps, cq_col, cmk_t, cmv_t, i, mem_qn[i], bd, MEM_DECODE_SEQS, ls)
        xp = _mix_ffn(xp, tok_p.reshape(bp * lp, tokw), mo_p.reshape(bp * lp, mw),
                      w_out_b, g_ffn, w1_b, w2_b, i, tm)
        xs = _mix_ffn(xs, tok_s, mo_s, w_out_b, g_ffn, w1_b, w2_b, i, tm)

    return (xp.reshape(bp, lp, d), xs.reshape(bs, ls, d),
            from_t(mk_p, MEM_HEADS), from_t(mv_p, MEM_HEADS),
            jnp.stack(hg_p), hg_s,
            from_t(jnp.stack(swk_p), SW_KV_HEADS), from_t(jnp.stack(swv_p), SW_KV_HEADS),
            from_t(sw_s[0], SW_KV_HEADS), from_t(sw_s[1], SW_KV_HEADS))
```

```python
import functools

import numpy as np
import jax
import jax.numpy as jnp
from jax import lax
from jax.experimental import pallas as pl
from jax.experimental.pallas import tpu as pltpu

F32 = jnp.float32
BF16 = jnp.bfloat16
EPS = 1e-6
NEG = -1e30
LB_MAX = 0.999
LOG2E = 1.4426950408889634

HEAD64 = 64
MEM_HEADS = 4
SW_KV_HEADS = 4
SW_GROUP = 3
WINDOW = 128
HG_HEAD = 128
HG_ROWS = 64
SAMPLE_PAD = 8

V7X_VMEM_BYTES = 64 * 1024 * 1024
VMEM_LIMIT = V7X_VMEM_BYTES - 8 * 1024 * 1024

DENSE_ROWS = 512
HGRN_ROWS = 1024
SWA_WINDOWS = 8
SWA_DECODE_SEQS = 8
MEM_DECODE_SEQS = 16


def _params(sem):
    return pltpu.CompilerParams(dimension_semantics=sem, vmem_limit_bytes=VMEM_LIMIT)


def _dot(a, b):
    return jnp.dot(a, b, preferred_element_type=F32)


def _dot_nt(a, b):
    return lax.dot_general(a, b, (((1,), (1,)), ((), ())), preferred_element_type=F32)


def _dot_tn(a, b):
    return lax.dot_general(a, b, (((0,), (0,)), ((), ())), preferred_element_type=F32)


def _sigmoid(x):
    return 1.0 / (1.0 + jnp.exp(-x))


def _sigmoid_pair(z):
    e = jnp.exp(-jnp.abs(z))
    r = 1.0 / (1.0 + e)
    er = e * r
    pos = z >= 0
    return jnp.where(pos, r, er), jnp.where(pos, er, r)


def _rms_rows(x, g):
    ms = jnp.mean(x * x, axis=-1, keepdims=True)
    return x * lax.rsqrt(ms + EPS) * g


def _head_rms64(x, g, bd):
    ms = _dot((x * x).astype(BF16), bd)
    return x * lax.rsqrt(ms + EPS) * g


def _head_mask(width, h):
    lane = lax.broadcasted_iota(jnp.int32, (1, width), 1)
    return ((lane >> 6) == h).astype(F32)


def _norm_matmul_kernel(x_ref, g_ref, w_ref, o_ref):
    h = _rms_rows(x_ref[...], g_ref[...]).astype(BF16)
    o_ref[...] = _dot(h, w_ref[...])


def _norm_matmul(x, g, g_idx, w, w_idx, tm):
    m, d = x.shape
    n = w.shape[2]
    tm = min(tm, m)
    return pl.pallas_call(
        _norm_matmul_kernel,
        grid=(m // tm,),
        in_specs=[
            pl.BlockSpec((tm, d), lambda i: (i, 0)),
            pl.BlockSpec((None, 1, d), lambda i: (g_idx, 0, 0)),
            pl.BlockSpec((None, d, n), lambda i: (w_idx, 0, 0)),
        ],
        out_specs=pl.BlockSpec((tm, n), lambda i: (i, 0)),
        out_shape=jax.ShapeDtypeStruct((m, n), F32),
        compiler_params=_params(("parallel",)),
        name="norm_matmul",
    )(x, g, w)


def _mix_ffn_kernel(x_ref, tok_ref, mo_ref, wo_ref, g_ref, w1_ref, w2_ref, o_ref, act_ref,
                    *, ff, chunk):
    tw = tok_ref.shape[1]
    x1 = (x_ref[...] + _dot(tok_ref[...], wo_ref[0:tw, :])
          + _dot(mo_ref[...], wo_ref[tw:, :]))
    o_ref[...] = x1
    h = _rms_rows(x1, g_ref[...]).astype(BF16)
    for c0 in range(0, ff, chunk):
        g = _dot(h, w1_ref[:, c0:c0 + chunk])
        u = _dot(h, w1_ref[:, ff + c0:ff + c0 + chunk])
        act_ref[:, c0:c0 + chunk] = (g * _sigmoid(g) * u).astype(BF16)
    o_ref[...] += _dot(act_ref[...], w2_ref[...])


def _mix_ffn(x, tok, mo, wo, g, w1, w2, layer, tm):
    m, d = x.shape
    ff = w2.shape[1]
    tm = min(tm, m)
    tw, mw = tok.shape[1], mo.shape[1]
    const = lambda i: (layer, 0, 0)
    return pl.pallas_call(
        functools.partial(_mix_ffn_kernel, ff=ff, chunk=256),
        grid=(m // tm,),
        in_specs=[
            pl.BlockSpec((tm, d), lambda i: (i, 0)),
            pl.BlockSpec((tm, tw), lambda i: (i, 0)),
            pl.BlockSpec((tm, mw), lambda i: (i, 0)),
            pl.BlockSpec((None, tw + mw, d), const, pipeline_mode=pl.Buffered(1)),
            pl.BlockSpec((None, 1, d), const),
            pl.BlockSpec((None, d, 2 * ff), const, pipeline_mode=pl.Buffered(1)),
            pl.BlockSpec((None, ff, d), const, pipeline_mode=pl.Buffered(1)),
        ],
        out_specs=pl.BlockSpec((tm, d), lambda i: (i, 0)),
        out_shape=jax.ShapeDtypeStruct((m, d), F32),
        scratch_shapes=[pltpu.VMEM((tm, ff), BF16)],
        compiler_params=_params(("parallel",)),
        name="mix_ffn",
    )(x, tok, mo, wo, g, w1, w2)


def _mem_kv_kernel(mem_ref, g_ref, w_ref, kn_ref, bd_ref, k_ref, v_ref, kb_ref):
    nb, nm, d = mem_ref.shape
    kw = k_ref.shape[1]
    h = _rms_rows(mem_ref[...].reshape(nb * nm, d), g_ref[...]).astype(BF16)
    kv = _dot(h, w_ref[...])
    k_all = _head_rms64(kv[:, :kw], kn_ref[...], bd_ref[...])
    for b in range(nb):
        k = k_all[b * nm:(b + 1) * nm]
        k_ref[b] = k.T
        v_ref[b] = kv[b * nm:(b + 1) * nm, kw:].T
        kb_ref[b] = jnp.concatenate([(k * _head_mask(kw, hd)).astype(BF16)
                                     for hd in range(MEM_HEADS)], axis=0)


def _mem_kv(mem, g, w, kn, bd):
    depth = w.shape[0]
    b, nm, d = mem.shape
    kw = w.shape[2] // 2
    out = jax.ShapeDtypeStruct((depth, b, kw, nm), F32)
    out_b = jax.ShapeDtypeStruct((depth, b, MEM_HEADS * nm, kw), BF16)
    return pl.pallas_call(
        _mem_kv_kernel,
        grid=(depth,),
        in_specs=[
            pl.BlockSpec((b, nm, d), lambda i: (0, 0, 0)),
            pl.BlockSpec((None, 1, d), lambda i: (i, 0, 0)),
            pl.BlockSpec((None, d, 2 * kw), lambda i: (i, 0, 0)),
            pl.BlockSpec((None, 1, kw), lambda i: (i, 0, 0)),
            pl.BlockSpec((kw, kw), lambda i: (0, 0)),
        ],
        out_specs=[pl.BlockSpec((None, b, kw, nm), lambda i: (i, 0, 0, 0))] * 2
        + [pl.BlockSpec((None, b, MEM_HEADS * nm, kw), lambda i: (i, 0, 0, 0))],
        out_shape=[out, out, out_b],
        compiler_params=_params(("parallel",)),
        name="mem_kv",
    )(mem, g, w, kn, bd)


def _mem_attend_kernel(q_ref, k_ref, v_ref, qn_ref, bd_ref, o_ref, *, nl):
    nb = k_ref.shape[0]
    w = q_ref.shape[1]
    tq = SAMPLE_PAD
    masks = [_head_mask(w, h) for h in range(MEM_HEADS)]
    qn_all = _head_rms64(q_ref[...], qn_ref[...], bd_ref[...]) * HEAD64 ** -0.5
    zpad = jnp.zeros((tq - nl, w), F32)
    outs = []
    for b in range(nb):
        qn = jnp.concatenate([qn_all[b * nl:(b + 1) * nl], zpad], axis=0)
        kt = k_ref[b].astype(BF16)
        vt = v_ref[b].astype(BF16)
        qbd = jnp.concatenate([(qn * m).astype(BF16) for m in masks], axis=0)
        s = _dot(qbd, kt)
        p = jnp.exp(s - jnp.max(s, axis=-1, keepdims=True))
        o = _dot_nt(p.astype(BF16), vt) / jnp.sum(p, axis=-1, keepdims=True)
        acc = o[0:tq] * masks[0]
        for h in range(1, MEM_HEADS):
            acc += o[h * tq:(h + 1) * tq] * masks[h]
        outs.append(acc[0:nl])
    o_ref[...] = jnp.concatenate(outs, axis=0).astype(o_ref.dtype)


def _mem_attend_rows(q, kb, vt, qn, bd):
    nm = vt.shape[1]
    qn = _head_rms64(q, qn, bd) * (HEAD64 ** -0.5 * LOG2E)
    st = _dot_nt(kb, qn.astype(BF16))
    vt = vt.astype(BF16)
    outs = []
    for h in range(MEM_HEADS):
        s = st[h * nm:(h + 1) * nm]
        p = jnp.exp2(s - jnp.max(s, axis=0, keepdims=True))
        den = jnp.sum(p, axis=0, keepdims=True)
        outs.append(_dot(vt[h * HEAD64:(h + 1) * HEAD64, :], p.astype(BF16)) / den)
    return jnp.concatenate(outs, axis=0).T


def _mem_prompt_specs(mkb, mvt, layer, col, tq):
    w, nm = mvt.shape[2], mvt.shape[3]
    ins = [
        pl.BlockSpec((None, tq, w), lambda i, t: (i, t, col)),
        pl.BlockSpec((None, None, MEM_HEADS * nm, w), lambda i, t: (layer, i, 0, 0)),
        pl.BlockSpec((None, None, w, nm), lambda i, t: (layer, i, 0, 0)),
        pl.BlockSpec((1, w), lambda i, t: (0, 0)),
        pl.BlockSpec((w, w), lambda i, t: (0, 0)),
    ]
    return ins, pl.BlockSpec((None, tq, w), lambda i, t: (i, t, 0))


def _mem_attend(q, col, mkt, mvt, layer, qn, bd, nb, nl):
    bsz = mkt.shape[1]
    w, nm = mkt.shape[2], mkt.shape[3]
    return pl.pallas_call(
        functools.partial(_mem_attend_kernel, nl=nl),
        grid=(bsz // nb,),
        in_specs=[
            pl.BlockSpec((nb * nl, w), lambda i: (i, col)),
            pl.BlockSpec((None, nb, w, nm), lambda i: (layer, i, 0, 0)),
            pl.BlockSpec((None, nb, w, nm), lambda i: (layer, i, 0, 0)),
            pl.BlockSpec((1, w), lambda i: (0, 0)),
            pl.BlockSpec((w, w), lambda i: (0, 0)),
        ],
        out_specs=pl.BlockSpec((nb * nl, w), lambda i: (i, 0)),
        out_shape=jax.ShapeDtypeStruct((bsz * nl, w), BF16),
        compiler_params=_params(("parallel",)),
        name="mem_attend",
    )(q, mkt, mvt, qn, bd)


def _swa_prompt_kernel(q_ref, kc_ref, vc_ref, kp_ref, vp_ref, qn_ref, kn_ref, sink_ref, bd_ref,
                       cq_ref, mkb_ref, mvt_ref, mqn_ref, mbd_ref,
                       tok_ref, kout_ref, vout_ref, mo_ref):
    n = pl.program_id(1)
    bd = bd_ref[...]
    mo_ref[...] = _mem_attend_rows(cq_ref[...], mkb_ref[...], mvt_ref[...], mqn_ref[...],
                                   mbd_ref[...]).astype(mo_ref.dtype)
    w = kc_ref.shape[-1]
    nwin = kc_ref.shape[0] // WINDOW
    kk = _head_rms64(jnp.concatenate([kp_ref[...], kc_ref[...]], axis=0), kn_ref[...], bd)
    vv = jnp.concatenate([vp_ref[...], vc_ref[...]], axis=0)
    kb = kk.astype(BF16)
    vvt = vv.T.astype(BF16)
    r = lax.broadcasted_iota(jnp.int32, (2 * WINDOW, WINDOW), 0)
    qi = lax.broadcasted_iota(jnp.int32, (2 * WINDOW, WINDOW), 1)
    band = (r > qi) & (r <= qi + WINDOW)
    masks = [_head_mask(w, k).astype(BF16) for k in range(SW_KV_HEADS)]
    for i in range(nwin):
        rows = slice(i * WINDOW, (i + 1) * WINDOW)
        keys = slice(i * WINDOW, (i + 2) * WINDOW)
        valid = band if i else band & ((r >= WINDOW) | (n > 0))
        pieces = []
        for g in range(SW_GROUP):
            qg = _head_rms64(q_ref[rows, g * w:(g + 1) * w], qn_ref[...], bd)
            qg = (qg * (HEAD64 ** -0.5 * LOG2E)).astype(BF16)
            pieces += [qg * m for m in masks]
        st = _dot_nt(kb[keys], jnp.concatenate(pieces, axis=0))
        outs = []
        for h in range(SW_GROUP * SW_KV_HEADS):
            k = h % SW_KV_HEADS
            s = jnp.where(valid, st[:, h * WINDOW:(h + 1) * WINDOW], NEG)
            sink = sink_ref[h] * LOG2E
            m = jnp.maximum(jnp.max(s, axis=0, keepdims=True), sink)
            p = jnp.exp2(s - m)
            den = jnp.sum(p, axis=0, keepdims=True) + jnp.exp2(sink - m)
            outs.append(_dot(vvt[k * HEAD64:(k + 1) * HEAD64, keys], p.astype(BF16)) / den)
        tok_ref[rows, :] = jnp.concatenate(outs, axis=0).T.astype(tok_ref.dtype)

    @pl.when(n == pl.num_programs(1) - 1)
    def _():
        kout_ref[...] = kk[nwin * WINDOW:].T
        vout_ref[...] = vv[nwin * WINDOW:].T


def _swa_prompt(p, qn, kn, sinks, bd, nwin, mkb, mvt, layer, mqn):
    b, l, _ = p.shape
    w = SW_KV_HEADS * HEAD64
    nq = SW_GROUP
    tq = nwin * WINDOW
    cache = jax.ShapeDtypeStruct((b, w, WINDOW), F32)
    prev = lambda c: (lambda i, n: (i, jnp.maximum(n * nwin - 1, 0), c))
    mem_in, mem_out = _mem_prompt_specs(mkb, mvt, layer, nq + 2, tq)
    return pl.pallas_call(
        _swa_prompt_kernel,
        grid=(b, l // tq),
        in_specs=[
            pl.BlockSpec((None, tq, nq * w), lambda i, n: (i, n, 0)),
            pl.BlockSpec((None, tq, w), lambda i, n: (i, n, nq)),
            pl.BlockSpec((None, tq, w), lambda i, n: (i, n, nq + 1)),
            pl.BlockSpec((None, WINDOW, w), prev(nq)),
            pl.BlockSpec((None, WINDOW, w), prev(nq + 1)),
            pl.BlockSpec((1, w), lambda i, n: (0, 0)),
            pl.BlockSpec((1, w), lambda i, n: (0, 0)),
            pl.BlockSpec(memory_space=pltpu.SMEM),
            pl.BlockSpec((w, w), lambda i, n: (0, 0)),
        ] + mem_in,
        out_specs=[
            pl.BlockSpec((None, tq, nq * w), lambda i, n: (i, n, 0)),
            pl.BlockSpec((None, w, WINDOW), lambda i, n: (i, 0, 0)),
            pl.BlockSpec((None, w, WINDOW), lambda i, n: (i, 0, 0)),
            mem_out,
        ],
        out_shape=[jax.ShapeDtypeStruct((b, l, nq * w), BF16), cache, cache,
                   jax.ShapeDtypeStruct((b, l, mvt.shape[2]), BF16)],
        compiler_params=_params(("parallel", "arbitrary")),
        name="swa_prompt",
    )(p, p, p, p, p, qn, kn, sinks, bd, p, mkb, mvt, mqn, bd)


def _swa_sample_kernel(q_ref, kn_ref_in, vn_ref_in, ck_ref, cv_ref, qn_ref, kn_ref, sink_ref,
                       bd_ref, *rest, nl, nprev):
    if nprev:
        pk_ref, pv_ref, tok_ref, kstack_ref, vstack_ref = rest
        kstack_ref[0:nprev] = pk_ref[...]
        vstack_ref[0:nprev] = pv_ref[...]
    else:
        tok_ref, kstack_ref, vstack_ref = rest
    kout_ref = kstack_ref.at[nprev]
    vout_ref = vstack_ref.at[nprev]
    nb = ck_ref.shape[0]
    w = ck_ref.shape[1]
    rows = SAMPLE_PAD
    nkeys = 2 * WINDOW
    bd = bd_ref[...]
    nh = SW_GROUP * SW_KV_HEADS
    masks = [_head_mask(w, k) for k in range(SW_KV_HEADS)]
    row = lax.broadcasted_iota(jnp.int32, (nh * rows, nkeys), 0)
    r = lax.broadcasted_iota(jnp.int32, (nh * rows, nkeys), 1)
    j = row & (rows - 1)
    valid = (r > j) & (r <= j + WINDOW)
    hrow = lax.broadcasted_iota(jnp.int32, (nh * rows, 1), 0) >> 3
    sink = jnp.zeros((nh * rows, 1), F32)
    for h in range(nh):
        sink = jnp.where(hrow == h, sink_ref[h], sink)
    zpad = jnp.zeros((WINDOW - nl, w), F32)
    zq = jnp.zeros((rows - nl, w), F32)
    is_new = lax.broadcasted_iota(jnp.int32, (1, WINDOW), 1) < nl
    knew_all = _head_rms64(kn_ref_in[...], kn_ref[...], bd)
    vnew_all = vn_ref_in[...]
    q_all = [_head_rms64(q_ref[:, g * w:(g + 1) * w], qn_ref[...], bd) * HEAD64 ** -0.5
             for g in range(SW_GROUP)]
    toks = []
    for b in range(nb):
        mine = slice(b * nl, (b + 1) * nl)
        ck, cv = ck_ref[b], cv_ref[b]
        knt = jnp.concatenate([knew_all[mine], zpad], axis=0).T
        vnt = jnp.concatenate([vnew_all[mine], zpad], axis=0).T
        kt = jnp.concatenate([ck, knt], axis=1)
        vt = jnp.concatenate([cv, vnt], axis=1)
        kout_ref[b] = pltpu.roll(jnp.where(is_new, knt, ck), WINDOW - nl, axis=1)
        vout_ref[b] = pltpu.roll(jnp.where(is_new, vnt, cv), WINDOW - nl, axis=1)
        pieces = []
        for g in range(SW_GROUP):
            qg = jnp.concatenate([q_all[g][mine], zq], axis=0)
            pieces += [(qg * masks[k]).astype(BF16) for k in range(SW_KV_HEADS)]
        s = _dot(jnp.concatenate(pieces, axis=0), kt.astype(BF16))
        s = jnp.where(valid, s, NEG)
        m = jnp.maximum(jnp.max(s, axis=-1, keepdims=True), sink)
        p = jnp.exp(s - m)
        den = jnp.sum(p, axis=-1, keepdims=True) + jnp.exp(sink - m)
        o = _dot_nt(p.astype(BF16), vt.astype(BF16)) / den
        accs = []
        for g in range(SW_GROUP):
            acc = jnp.zeros((rows, w), F32)
            for k in range(SW_KV_HEADS):
                h = g * SW_KV_HEADS + k
                acc += o[h * rows:(h + 1) * rows] * masks[k]
            accs.append(acc[0:nl])
        toks.append(jnp.concatenate(accs, axis=1))
    tok_ref[...] = jnp.concatenate(toks, axis=0).astype(tok_ref.dtype)


def _swa_sample(p, ckt, cvt, layer, qn, kn, sinks, bd, nl, nb, prev):
    b = ckt.shape[1]
    w = ckt.shape[2]
    nq = SW_GROUP
    nprev = 0 if prev is None else prev[0].shape[0]
    cache = jax.ShapeDtypeStruct((nprev + 1, b, w, WINDOW), F32)
    c4 = lambda i: (layer, i, 0, 0)
    stack = lambda n: pl.BlockSpec((n, nb, w, WINDOW), lambda i: (0, i, 0, 0))
    return pl.pallas_call(
        functools.partial(_swa_sample_kernel, nl=nl, nprev=nprev),
        grid=(b // nb,),
        in_specs=[
            pl.BlockSpec((nb * nl, nq * w), lambda i: (i, 0)),
            pl.BlockSpec((nb * nl, w), lambda i: (i, nq)),
            pl.BlockSpec((nb * nl, w), lambda i: (i, nq + 1)),
            pl.BlockSpec((None, nb, w, WINDOW), c4),
            pl.BlockSpec((None, nb, w, WINDOW), c4),
            pl.BlockSpec((1, w), lambda i: (0, 0)),
            pl.BlockSpec((1, w), lambda i: (0, 0)),
            pl.BlockSpec(memory_space=pltpu.SMEM),
            pl.BlockSpec((w, w), lambda i: (0, 0)),
        ] + ([stack(nprev)] * 2 if nprev else []),
        out_specs=[
            pl.BlockSpec((nb * nl, nq * w), lambda i: (i, 0)),
            stack(nprev + 1),
            stack(nprev + 1),
        ],
        out_shape=[jax.ShapeDtypeStruct((b * nl, nq * w), BF16), cache, cache],
        compiler_params=_params(("parallel",)),
        name="swa_sample",
    )(p, p, p, ckt, cvt, qn, kn, sinks, bd, *(() if prev is None else prev))


def _hgrn_consts(rows, group):
    t = np.arange(rows)
    loc, run = t % group, t // group
    same = run[:, None] == run[None, :]
    r = t[None, :]
    mats = [same & (r <= t[:, None]), same & (r > t[:, None])]
    masks = []
    h = group // 2
    while h >= 1:
        par = loc // (2 * h)
        right = (loc % (2 * h)) >= h
        bnd = (run * group + par * 2 * h + h - 1)[:, None]
        q_side = right[:, None] & (r > bnd) & (r <= t[:, None])
        k_side = (~right)[:, None] & (r > t[:, None]) & (r <= bnd)
        mats.append(q_side | k_side)
        masks.append(same & (par[:, None] == par[None, :]) & right[:, None] & (~right)[None, :])
        h //= 2
    masks.append(np.eye(rows, dtype=bool))
    w = np.concatenate(mats, axis=0).astype(np.float32)
    w2 = np.concatenate([w, w], axis=1)
    return jnp.asarray(w2, BF16), jnp.asarray(np.stack(masks).astype(np.float32)), len(masks) - 1


def _hgrn_gates(q_raw, z, lb):
    sp, sn = _sigmoid_pair(z)
    log2f = jnp.log2(lb + (1.0 - lb) * sp)
    kf = (1.0 - lb) * sn
    q = q_raw * _sigmoid(q_raw) * HG_HEAD ** -0.5
    return q, kf, log2f


def _hgrn_decays(log2f, w2_ref):
    hi = log2f.astype(BF16)
    lo = (log2f - hi.astype(F32)).astype(BF16)
    return jnp.exp2(_dot(w2_ref[...], jnp.concatenate([hi, lo], axis=0)))


def _hgrn_intra(q, kf, v, e, m_ref, nlev):
    rows = q.shape[0]
    att = m_ref[nlev] * _dot_nt(q.astype(BF16), kf.astype(BF16))
    for l in range(nlev):
        el = e[(2 + l) * rows:(3 + l) * rows]
        att += m_ref[l] * _dot_nt((q * el).astype(BF16), (kf * el).astype(BF16))
    return _dot(att.astype(BF16), v.astype(BF16))


def _hgrn_out(o, g, gn):
    ms = jnp.mean(o * o, axis=-1, keepdims=True)
    return (o * lax.rsqrt(ms + EPS) * gn * (g * _sigmoid(g))).astype(BF16)


def _hgrn_prompt_kernel(q_ref, f_ref, i_ref, g_ref, lb_ref, gn_ref, w2_ref, m_ref,
                        cq_ref, mkb_ref, mvt_ref, mqn_ref, mbd_ref,
                        tok_ref, st_ref, mo_ref, s_scr, *, nlev):
    t = pl.program_id(1)

    @pl.when(t == 0)
    def _():
        s_scr[...] = jnp.zeros_like(s_scr)

    mo_ref[...] = _mem_attend_rows(cq_ref[...], mkb_ref[...], mvt_ref[...], mqn_ref[...],
                                   mbd_ref[...]).astype(mo_ref.dtype)

    lb = lb_ref[...]
    gn = gn_ref[...]
    rows = HG_ROWS
    nh = q_ref.shape[1] // HG_HEAD

    def chunk(c, st):
        sl = pl.ds(pl.multiple_of(c * rows, rows), rows)
        v = i_ref[sl, :]
        g = g_ref[sl, :]
        q, kf, log2f = _hgrn_gates(q_ref[sl, :], f_ref[sl, :], lb)
        e = _hgrn_decays(log2f, w2_ref)
        qt = (q * e[0:rows]).astype(BF16)
        kh = (kf * e[rows:2 * rows]).astype(BF16)
        vb = v.astype(BF16)
        stb = st.astype(BF16)
        qb = q.astype(BF16)
        kb = kf.astype(BF16)
        ql, kl = [qb], [kb]
        rowi = lax.broadcasted_iota(jnp.int32, (rows, 1), 0)
        for l in range(nlev):
            half = rows >> (l + 1)
            if half >= 8:
                base = jnp.concatenate([(q if (r0 // half) & 1 else kf)[r0:r0 + half]
                                        for r0 in range(0, rows, half)], axis=0)
            else:
                base = jnp.where((rowi & (2 * half - 1)) >= half, q, kf)
            x = (base * e[(2 + l) * rows:(3 + l) * rows]).astype(BF16)
            ql.append(x)
            kl.append(x)
        mk = [m_ref[nlev]] + [m_ref[l] for l in range(nlev)]
        zr = jnp.zeros((rows, HG_HEAD), BF16)
        zs = jnp.zeros((HG_HEAD, HG_HEAD), BF16)

        def bdiag(a, b, z):
            return jnp.concatenate([jnp.concatenate([a, z], axis=1),
                                    jnp.concatenate([z, b], axis=1)], axis=0)

        upd = []
        for p in range(nh // 2):
            h0 = slice(2 * p * HG_HEAD, (2 * p + 1) * HG_HEAD)
            h1 = slice((2 * p + 1) * HG_HEAD, (2 * p + 2) * HG_HEAD)
            pr = slice(2 * p * HG_HEAD, (2 * p + 2) * HG_HEAD)
            att = jnp.zeros((rows, 2 * rows), F32)
            for a, b, m in zip(ql, kl, mk):
                att += m * _dot_nt(a[:, pr], bdiag(b[:, h0], b[:, h1], zr))
            o = _dot(att.astype(BF16), bdiag(vb[:, h0], vb[:, h1], zr))
            o += _dot_nt(qt[:, pr], bdiag(stb[:, h0], stb[:, h1], zs))
            for hs in (h0, h1):
                lo = hs.start - pr.start
                tok_ref[sl, hs] = _hgrn_out(o[:, lo:lo + HG_HEAD], g[:, hs], gn[:, hs])
                upd.append(_dot_tn(vb[:, hs], kh[:, hs]))
        return st * e[rows - 1:rows, :] + jnp.concatenate(upd, axis=1)

    st = lax.fori_loop(0, q_ref.shape[0] // rows, chunk, s_scr[...], unroll=True)
    s_scr[...] = st

    @pl.when(t == pl.num_programs(1) - 1)
    def _():
        for h in range(nh):
            st_ref[h] = st[:, h * HG_HEAD:(h + 1) * HG_HEAD].T


def _hgrn_prompt(p, lb, gn, tb, mkb, mvt, layer, mqn, bd):
    b, l, _ = p.shape
    tw = lb.shape[1]
    nh = tw // HG_HEAD
    mw = mvt.shape[2]
    w2, masks, nlev = _hgrn_consts(HG_ROWS, HG_ROWS)
    masks = jnp.tile(masks, (1, 1, 2))
    sec = lambda s: pl.BlockSpec((None, tb, tw), lambda i, t: (i, t, s))
    mem_in, mem_out = _mem_prompt_specs(mkb, mvt, layer, 4 * tw // mw, tb)
    return pl.pallas_call(
        functools.partial(_hgrn_prompt_kernel, nlev=nlev),
        grid=(b, l // tb),
        in_specs=[
            sec(0), sec(1), sec(2), sec(3),
            pl.BlockSpec((1, tw), lambda i, t: (0, 0)),
            pl.BlockSpec((1, tw), lambda i, t: (0, 0)),
            pl.BlockSpec(w2.shape, lambda i, t: (0, 0)),
            pl.BlockSpec(masks.shape, lambda i, t: (0, 0, 0)),
        ] + mem_in,
        out_specs=[
            pl.BlockSpec((None, tb, tw), lambda i, t: (i, t, 0)),
            pl.BlockSpec((None, nh, HG_HEAD, HG_HEAD), lambda i, t: (i, 0, 0, 0)),
            mem_out,
        ],
        out_shape=[jax.ShapeDtypeStruct((b, l, tw), BF16),
                   jax.ShapeDtypeStruct((b, nh, HG_HEAD, HG_HEAD), F32),
                   jax.ShapeDtypeStruct((b, l, mw), BF16)],
        scratch_shapes=[pltpu.VMEM((HG_HEAD, tw), F32)],
        compiler_params=_params(("parallel", "arbitrary")),
        name="hgrn_prompt",
    )(p, p, p, p, lb, gn, w2, masks, p, mkb, mvt, mqn, bd)


def _hgrn_sample_kernel(q_ref, f_ref, i_ref, g_ref, s0_ref, lb_ref, gn_ref, w2_ref, m_ref,
                        *rest, nlev, nl, nprev):
    if nprev:
        prev_ref, tok_ref, stack_ref = rest
        stack_ref[0:nprev] = prev_ref[...]
    else:
        tok_ref, stack_ref = rest
    st_ref = stack_ref.at[nprev]
    rows = q_ref.shape[0]
    nb = rows // nl
    shift = nl.bit_length() - 1
    rowb = lax.broadcasted_iota(jnp.int32, (rows, 1), 0) >> shift
    colb = lax.broadcasted_iota(jnp.int32, (rows, nb * HG_HEAD), 1) >> 7
    v_all = i_ref[...]
    q_all, kf_all, log2f = _hgrn_gates(q_ref[...], f_ref[...], lb_ref[...])
    e_all = _hgrn_decays(log2f, w2_ref)
    g_all = g_ref[...]
    gn_all = gn_ref[...]
    for hh in range(q_ref.shape[1] // HG_HEAD):
        hs = slice(hh * HG_HEAD, (hh + 1) * HG_HEAD)
        q, kf, v, e = q_all[:, hs], kf_all[:, hs], v_all[:, hs], e_all[:, hs]
        o_intra = _hgrn_intra(q, kf, v, e, m_ref, nlev)
        e_cum = e[0:rows]
        qt = q * e_cum
        kh = kf * e[rows:2 * rows]
        s0cat = jnp.concatenate([s0_ref[b, hh].astype(BF16) for b in range(nb)], axis=1)
        full = _dot(qt.astype(BF16), s0cat)
        o = o_intra
        for b in range(nb):
            o += jnp.where(rowb == b, full[:, b * HG_HEAD:(b + 1) * HG_HEAD], 0.0)
        tok_ref[:, hs] = _hgrn_out(o, g_all[:, hs], gn_all[:, hs])
        xt = jnp.concatenate([kh, e_cum], axis=0).T
        vbd = jnp.where(rowb == colb, jnp.concatenate([v] * nb, axis=1), 0.0).astype(BF16)
        upd = _dot(xt[:, 0:rows].astype(BF16), vbd)
        for b in range(nb):
            last = rows + b * nl + nl - 1
            st_ref[b, hh] = (s0_ref[b, hh] * xt[:, last:last + 1]
                             + upd[:, b * HG_HEAD:(b + 1) * HG_HEAD])


def _hgrn_sample(p, s0, layer, lb, gn, nl, prev):
    rows = HG_ROWS
    nb = rows // nl
    b, nh = s0.shape[1], s0.shape[2]
    hp = 2
    hw = hp * HG_HEAD
    nprev = 0 if prev is None else prev.shape[0]
    w2, masks, nlev = _hgrn_consts(rows, nl)
    sec = lambda s: pl.BlockSpec((rows, hw), lambda i, h: (i, s * (nh // hp) + h))
    stack = lambda n: pl.BlockSpec((n, nb, hp, HG_HEAD, HG_HEAD), lambda i, h: (0, i, h, 0, 0))
    s0_spec = pl.BlockSpec((None, nb, hp, HG_HEAD, HG_HEAD), lambda i, h: (layer, i, h, 0, 0))
    return pl.pallas_call(
        functools.partial(_hgrn_sample_kernel, nlev=nlev, nl=nl, nprev=nprev),
        grid=(b // nb, nh // hp),
        in_specs=[
            sec(0), sec(1), sec(2), sec(3), s0_spec,
            pl.BlockSpec((1, hw), lambda i, h: (0, h)),
            pl.BlockSpec((1, hw), lambda i, h: (0, h)),
            pl.BlockSpec(w2.shape, lambda i, h: (0, 0)),
            pl.BlockSpec(masks.shape, lambda i, h: (0, 0, 0)),
        ] + ([stack(nprev)] if nprev else []),
        out_specs=[pl.BlockSpec((rows, hw), lambda i, h: (i, h)), stack(nprev + 1)],
        out_shape=[jax.ShapeDtypeStruct((b * nl, nh * HG_HEAD), BF16),
                   jax.ShapeDtypeStruct((nprev + 1,) + s0.shape[1:], F32)],
        compiler_params=_params(("parallel", "parallel")),
        name="hgrn_sample",
    )(p, p, p, p, s0, lb, gn, w2, masks, *(() if prev is None else (prev,)))


def _swa_regroup(w, axis):
    axis %= w.ndim
    shape = w.shape[:axis] + (SW_KV_HEADS, SW_GROUP, HEAD64) + w.shape[axis + 1:]
    return jnp.swapaxes(w.reshape(shape), axis, axis + 1).reshape(w.shape)


def kernel(x_prompt, x_sample, cache_mem_k, cache_mem_v, state_hgrn, cache_swa_k, cache_swa_v,
           mem_prompt, norm_mix, norm_ffn, norm_mem, w_in_hgrn, hgrn_lb_raw, hgrn_out_norm,
           w_in_swa, swa_q_norm, swa_k_norm, swa_sinks, w_mem_kv, mem_q_norm, mem_k_norm,
           w_out, w_ffn_in, w_ffn_out):
    depth, d = norm_mix.shape
    bp, lp, _ = x_prompt.shape
    bs, ls, _ = x_sample.shape
    nm = mem_prompt.shape[1]
    mw = MEM_HEADS * HEAD64
    kvw = SW_KV_HEADS * HEAD64
    tokw = d - mw
    tm = DENSE_ROWS

    seg = np.arange(mw) // HEAD64
    bd = jnp.asarray((seg[:, None] == seg[None, :]).astype(np.float32) / HEAD64, BF16)
    tile4 = lambda a: jnp.tile(a.astype(F32), (1, mw // HEAD64))[:, None, :]

    sm = jax.nn.softmax(hgrn_lb_raw.astype(F32), axis=0)
    lbs = jnp.clip(jnp.cumsum(sm, axis=0) - sm[0], 0.0, LB_MAX)

    w_in_swa_b = w_in_swa.astype(BF16)
    w_in_swa_p = jnp.concatenate([_swa_regroup(w_in_swa_b[..., :tokw], -1),
                                  w_in_swa_b[..., tokw:]], axis=-1)
    w_in_hgrn_b = w_in_hgrn.astype(BF16)
    w_out_b = w_out.astype(BF16)
    w_out_b = jnp.stack([w_out_b[i] if i % 2 == 0 else
                         jnp.concatenate([_swa_regroup(w_out_b[i, :tokw], 0), w_out_b[i, tokw:]],
                                         axis=0)
                         for i in range(depth)])
    w1_b = w_ffn_in.astype(BF16)
    w2_b = w_ffn_out.astype(BF16)
    sinks_p = swa_sinks.astype(F32).reshape(-1, SW_KV_HEADS, SW_GROUP).transpose(0, 2, 1)
    sinks_p = sinks_p.reshape(-1, SW_KV_HEADS * SW_GROUP)
    mem_qn = tile4(mem_q_norm)
    swa_qn = tile4(swa_q_norm)
    swa_kn = tile4(swa_k_norm)

    mk_p, mv_p, mk_b = _mem_kv(mem_prompt, norm_mem[:, None, :], w_mem_kv.astype(BF16),
                               tile4(mem_k_norm), bd)

    to_t = lambda a: a.transpose(0, 1, 3, 4, 2).reshape(a.shape[0], a.shape[1], -1, a.shape[2])
    from_t = lambda a, heads: a.reshape(a.shape[:-2] + (heads, HEAD64, a.shape[-1])).transpose(
        *range(a.ndim - 2), a.ndim, a.ndim - 2, a.ndim - 1)
    cmk_t, cmv_t = to_t(cache_mem_k), to_t(cache_mem_v)
    csk_t, csv_t = to_t(cache_swa_k), to_t(cache_swa_v)

    xp = x_prompt.reshape(bp * lp, d)
    xs = x_sample.reshape(bs * ls, d)
    g_mix = norm_mix[:, None, :]
    g_ffn = norm_ffn[:, None, :]
    hg_p, swk_p, swv_p = [], [], []
    hg_s = sw_s = None
    for i in range(depth):
        j = i // 2
        if i % 2 == 0:
            lb = lbs[j][None, :]
            gn = hgrn_out_norm[j][None, :].astype(F32)
            pp = _norm_matmul(xp, g_mix, i, w_in_hgrn_b, j, tm).reshape(bp, lp, -1)
            tok_p, st_p, mo_p = _hgrn_prompt(pp, lb, gn, HGRN_ROWS, mk_b, mv_p, i, mem_qn[i], bd)
            ps = _norm_matmul(xs, g_mix, i, w_in_hgrn_b, j, tm)
            tok_s, hg_s = _hgrn_sample(ps, state_hgrn, j, lb, gn, ls, hg_s)
            hg_p.append(st_p)
            cq_col = 4 * tokw // mw
        else:
            pp = _norm_matmul(xp, g_mix, i, w_in_swa_p, j, tm).reshape(bp, lp, -1)
            tok_p, kb, vb, mo_p = _swa_prompt(pp, swa_qn[j], swa_kn[j], sinks_p[j], bd,
                                              SWA_WINDOWS, mk_b, mv_p, i, mem_qn[i])
            swk_p.append(kb)
            swv_p.append(vb)
            ps = _norm_matmul(xs, g_mix, i, w_in_swa_p, j, tm)
            tok_s, *sw_s = _swa_sample(ps, csk_t, csv_t, j, swa_qn[j], swa_kn[j],
                                       sinks_p[j], bd, ls, SWA_DECODE_SEQS, sw_s)
            cq_col = (tokw + 2 * kvw) // mw
        mo_s = _mem_attend(ps, cq_col, cmk_t, cmv_t, i, mem_qn[i], bd, MEM_DECODE_SEQS, ls)
        xp = _mix_ffn(xp, tok_p.reshape(bp * lp, tokw), mo_p.reshape(bp * lp, mw),
                      w_out_b, g_ffn, w1_b, w2_b, i, tm)
        xs = _mix_ffn(xs, tok_s, mo_s, w_out_b, g_ffn, w1_b, w2_b, i, tm)

    return (xp.reshape(bp, lp, d), xs.reshape(bs, ls, d),
            from_t(mk_p, MEM_HEADS), from_t(mv_p, MEM_HEADS),
            jnp.stack(hg_p), hg_s,
            from_t(jnp.stack(swk_p), SW_KV_HEADS), from_t(jnp.stack(swv_p), SW_KV_HEADS),
            from_t(sw_s[0], SW_KV_HEADS), from_t(sw_s[1], SW_KV_HEADS))
```

```python
import functools

import numpy as np
import jax
import jax.numpy as jnp
from jax import lax
from jax.experimental import pallas as pl
from jax.experimental.pallas import tpu as pltpu

F32 = jnp.float32
BF16 = jnp.bfloat16
EPS = 1e-6
NEG = -1e30
LB_MAX = 0.999
LOG2E = 1.4426950408889634

HEAD64 = 64
MEM_HEADS = 4
SW_KV_HEADS = 4
SW_GROUP = 3
WINDOW = 128
HG_HEAD = 128
HG_ROWS = 64
SAMPLE_PAD = 8

V7X_VMEM_BYTES = 64 * 1024 * 1024
VMEM_LIMIT = V7X_VMEM_BYTES - 8 * 1024 * 1024

DENSE_ROWS = 512
HGRN_ROWS = 1024
SWA_WINDOWS = 8
SWA_DECODE_SEQS = 8
MEM_DECODE_SEQS = 16


def _params(sem):
    return pltpu.CompilerParams(dimension_semantics=sem, vmem_limit_bytes=VMEM_LIMIT)


def _dot(a, b):
    return jnp.dot(a, b, preferred_element_type=F32)


def _dot_nt(a, b):
    return lax.dot_general(a, b, (((1,), (1,)), ((), ())), preferred_element_type=F32)


def _dot_tn(a, b):
    return lax.dot_general(a, b, (((0,), (0,)), ((), ())), preferred_element_type=F32)


def _sigmoid(x):
    return 1.0 / (1.0 + jnp.exp(-x))


def _sigmoid_pair(z):
    e = jnp.exp(-jnp.abs(z))
    r = 1.0 / (1.0 + e)
    er = e * r
    pos = z >= 0
    return jnp.where(pos, r, er), jnp.where(pos, er, r)


def _rms_rows(x, g):
    ms = jnp.mean(x * x, axis=-1, keepdims=True)
    return x * lax.rsqrt(ms + EPS) * g


def _head_rms64(x, g, bd):
    ms = _dot((x * x).astype(BF16), bd)
    return x * lax.rsqrt(ms + EPS) * g


def _head_mask(width, h):
    lane = lax.broadcasted_iota(jnp.int32, (1, width), 1)
    return ((lane >> 6) == h).astype(F32)


def _norm_matmul_kernel(x_ref, g_ref, w_ref, o_ref):
    h = _rms_rows(x_ref[...], g_ref[...]).astype(BF16)
    o_ref[...] = _dot(h, w_ref[...])


def _norm_matmul(x, g, g_idx, w, w_idx, tm):
    m, d = x.shape
    n = w.shape[2]
    tm = min(tm, m)
    return pl.pallas_call(
        _norm_matmul_kernel,
        grid=(m // tm,),
        in_specs=[
            pl.BlockSpec((tm, d), lambda i: (i, 0)),
            pl.BlockSpec((None, 1, d), lambda i: (g_idx, 0, 0)),
            pl.BlockSpec((None, d, n), lambda i: (w_idx, 0, 0)),
        ],
        out_specs=pl.BlockSpec((tm, n), lambda i: (i, 0)),
        out_shape=jax.ShapeDtypeStruct((m, n), F32),
        compiler_params=_params(("parallel",)),
        name="norm_matmul",
    )(x, g, w)


def _mix_ffn_kernel(xp_ref, tokp_ref, mop_ref, xs_ref, toks_ref, mos_ref, wo_ref, g_ref, w1_ref,
                    w2_ref, op_ref, os_ref, act_ref, *, ff, chunk, np_steps):
    is_p = pl.program_id(0) < np_steps

    def tile(x_ref, tok_ref, mo_ref, o_ref):
        tw = tok_ref.shape[1]
        x1 = (x_ref[...] + _dot(tok_ref[...], wo_ref[0:tw, :])
              + _dot(mo_ref[...], wo_ref[tw:, :]))
        o_ref[...] = x1
        h = _rms_rows(x1, g_ref[...]).astype(BF16)
        for c0 in range(0, ff, chunk):
            g = _dot(h, w1_ref[:, c0:c0 + chunk])
            u = _dot(h, w1_ref[:, ff + c0:ff + c0 + chunk])
            act_ref[:, c0:c0 + chunk] = (g * _sigmoid(g) * u).astype(BF16)
        o_ref[...] += _dot(act_ref[...], w2_ref[...])

    @pl.when(is_p)
    def _():
        tile(xp_ref, tokp_ref, mop_ref, op_ref)

    @pl.when(jnp.logical_not(is_p))
    def _():
        tile(xs_ref, toks_ref, mos_ref, os_ref)


def _mix_ffn(xp, tokp, mop, xs, toks, mos, wo, g, w1, w2, layer, tm):
    mp, d = xp.shape
    ms = xs.shape[0]
    ff = w2.shape[1]
    tm = min(tm, mp, ms)
    np_steps, ns_steps = mp // tm, ms // tm
    tw, mw = tokp.shape[1], mop.shape[1]
    const = lambda i: (layer, 0, 0)
    pidx = lambda i: (jnp.minimum(i, np_steps - 1), 0)
    sidx = lambda i: (jnp.maximum(i - np_steps, 0), 0)
    rows = lambda width, idx: pl.BlockSpec((tm, width), idx)
    return pl.pallas_call(
        functools.partial(_mix_ffn_kernel, ff=ff, chunk=256, np_steps=np_steps),
        grid=(np_steps + ns_steps,),
        in_specs=[
            rows(d, pidx), rows(tw, pidx), rows(mw, pidx),
            rows(d, sidx), rows(tw, sidx), rows(mw, sidx),
            pl.BlockSpec((None, tw + mw, d), const, pipeline_mode=pl.Buffered(1)),
            pl.BlockSpec((None, 1, d), const),
            pl.BlockSpec((None, d, 2 * ff), const, pipeline_mode=pl.Buffered(1)),
            pl.BlockSpec((None, ff, d), const, pipeline_mode=pl.Buffered(1)),
        ],
        out_specs=[rows(d, pidx), rows(d, sidx)],
        out_shape=[jax.ShapeDtypeStruct((mp, d), F32), jax.ShapeDtypeStruct((ms, d), F32)],
        scratch_shapes=[pltpu.VMEM((tm, ff), BF16)],
        compiler_params=_params(("arbitrary",)),
        name="mix_ffn",
    )(xp, tokp, mop, xs, toks, mos, wo, g, w1, w2)


def _mem_kv_kernel(mem_ref, g_ref, w_ref, kn_ref, bd_ref, k_ref, v_ref, kb_ref):
    nb, nm, d = mem_ref.shape
    kw = k_ref.shape[1]
    h = _rms_rows(mem_ref[...].reshape(nb * nm, d), g_ref[...]).astype(BF16)
    kv = _dot(h, w_ref[...])
    k_all = _head_rms64(kv[:, :kw], kn_ref[...], bd_ref[...])
    for b in range(nb):
        k = k_all[b * nm:(b + 1) * nm]
        k_ref[b] = k.T
        v_ref[b] = kv[b * nm:(b + 1) * nm, kw:].T
        kb_ref[b] = jnp.concatenate([(k * _head_mask(kw, hd)).astype(BF16)
                                     for hd in range(MEM_HEADS)], axis=0)


def _mem_kv(mem, g, w, kn, bd):
    depth = w.shape[0]
    b, nm, d = mem.shape
    kw = w.shape[2] // 2
    out = jax.ShapeDtypeStruct((depth, b, kw, nm), F32)
    out_b = jax.ShapeDtypeStruct((depth, b, MEM_HEADS * nm, kw), BF16)
    return pl.pallas_call(
        _mem_kv_kernel,
        grid=(depth,),
        in_specs=[
            pl.BlockSpec((b, nm, d), lambda i: (0, 0, 0)),
            pl.BlockSpec((None, 1, d), lambda i: (i, 0, 0)),
            pl.BlockSpec((None, d, 2 * kw), lambda i: (i, 0, 0)),
            pl.BlockSpec((None, 1, kw), lambda i: (i, 0, 0)),
            pl.BlockSpec((kw, kw), lambda i: (0, 0)),
        ],
        out_specs=[pl.BlockSpec((None, b, kw, nm), lambda i: (i, 0, 0, 0))] * 2
        + [pl.BlockSpec((None, b, MEM_HEADS * nm, kw), lambda i: (i, 0, 0, 0))],
        out_shape=[out, out, out_b],
        compiler_params=_params(("parallel",)),
        name="mem_kv",
    )(mem, g, w, kn, bd)


def _mem_attend_kernel(q_ref, k_ref, v_ref, qn_ref, bd_ref, o_ref, *, nl):
    nb = k_ref.shape[0]
    w = q_ref.shape[1]
    tq = SAMPLE_PAD
    masks = [_head_mask(w, h) for h in range(MEM_HEADS)]
    qn_all = _head_rms64(q_ref[...], qn_ref[...], bd_ref[...]) * HEAD64 ** -0.5
    zpad = jnp.zeros((tq - nl, w), F32)
    outs = []
    for b in range(nb):
        qn = jnp.concatenate([qn_all[b * nl:(b + 1) * nl], zpad], axis=0)
        kt = k_ref[b].astype(BF16)
        vt = v_ref[b].astype(BF16)
        qbd = jnp.concatenate([(qn * m).astype(BF16) for m in masks], axis=0)
        s = _dot(qbd, kt)
        p = jnp.exp(s - jnp.max(s, axis=-1, keepdims=True))
        o = _dot_nt(p.astype(BF16), vt) / jnp.sum(p, axis=-1, keepdims=True)
        acc = o[0:tq] * masks[0]
        for h in range(1, MEM_HEADS):
            acc += o[h * tq:(h + 1) * tq] * masks[h]
        outs.append(acc[0:nl])
    o_ref[...] = jnp.concatenate(outs, axis=0).astype(o_ref.dtype)


def _mem_attend_rows(q, kb, vt, qn, bd):
    nm = vt.shape[1]
    qn = _head_rms64(q, qn, bd) * (HEAD64 ** -0.5 * LOG2E)
    st = _dot_nt(kb, qn.astype(BF16))
    vt = vt.astype(BF16)
    outs = []
    for h in range(MEM_HEADS):
        s = st[h * nm:(h + 1) * nm]
        p = jnp.exp2(s - jnp.max(s, axis=0, keepdims=True))
        den = jnp.sum(p, axis=0, keepdims=True)
        outs.append(_dot(vt[h * HEAD64:(h + 1) * HEAD64, :], p.astype(BF16)) / den)
    return jnp.concatenate(outs, axis=0).T


def _mem_prompt_specs(mkb, mvt, layer, col, tq):
    w, nm = mvt.shape[2], mvt.shape[3]
    ins = [
        pl.BlockSpec((None, tq, w), lambda i, t: (i, t, col)),
        pl.BlockSpec((None, None, MEM_HEADS * nm, w), lambda i, t: (layer, i, 0, 0)),
        pl.BlockSpec((None, None, w, nm), lambda i, t: (layer, i, 0, 0)),
        pl.BlockSpec((1, w), lambda i, t: (0, 0)),
        pl.BlockSpec((w, w), lambda i, t: (0, 0)),
    ]
    return ins, pl.BlockSpec((None, tq, w), lambda i, t: (i, t, 0))


def _mem_attend(q, col, mkt, mvt, layer, qn, bd, nb, nl):
    bsz = mkt.shape[1]
    w, nm = mkt.shape[2], mkt.shape[3]
    return pl.pallas_call(
        functools.partial(_mem_attend_kernel, nl=nl),
        grid=(bsz // nb,),
        in_specs=[
            pl.BlockSpec((nb * nl, w), lambda i: (i, col)),
            pl.BlockSpec((None, nb, w, nm), lambda i: (layer, i, 0, 0)),
            pl.BlockSpec((None, nb, w, nm), lambda i: (layer, i, 0, 0)),
            pl.BlockSpec((1, w), lambda i: (0, 0)),
            pl.BlockSpec((w, w), lambda i: (0, 0)),
        ],
        out_specs=pl.BlockSpec((nb * nl, w), lambda i: (i, 0)),
        out_shape=jax.ShapeDtypeStruct((bsz * nl, w), BF16),
        compiler_params=_params(("parallel",)),
        name="mem_attend",
    )(q, mkt, mvt, qn, bd)


def _swa_prompt_kernel(q_ref, kc_ref, vc_ref, kp_ref, vp_ref, qn_ref, kn_ref, sink_ref, bd_ref,
                       cq_ref, mkb_ref, mvt_ref, mqn_ref, mbd_ref,
                       tok_ref, kout_ref, vout_ref, mo_ref):
    n = pl.program_id(1)
    bd = bd_ref[...]
    mo_ref[...] = _mem_attend_rows(cq_ref[...], mkb_ref[...], mvt_ref[...], mqn_ref[...],
                                   mbd_ref[...]).astype(mo_ref.dtype)
    w = kc_ref.shape[-1]
    nwin = kc_ref.shape[0] // WINDOW
    kk = _head_rms64(jnp.concatenate([kp_ref[...], kc_ref[...]], axis=0), kn_ref[...], bd)
    vv = jnp.concatenate([vp_ref[...], vc_ref[...]], axis=0)
    kb = kk.astype(BF16)
    vvt = vv.T.astype(BF16)
    r = lax.broadcasted_iota(jnp.int32, (2 * WINDOW, WINDOW), 0)
    qi = lax.broadcasted_iota(jnp.int32, (2 * WINDOW, WINDOW), 1)
    band = (r > qi) & (r <= qi + WINDOW)
    masks = [_head_mask(w, k).astype(BF16) for k in range(SW_KV_HEADS)]
    for i in range(nwin):
        rows = slice(i * WINDOW, (i + 1) * WINDOW)
        keys = slice(i * WINDOW, (i + 2) * WINDOW)
        valid = band if i else band & ((r >= WINDOW) | (n > 0))
        pieces = []
        for g in range(SW_GROUP):
            qg = _head_rms64(q_ref[rows, g * w:(g + 1) * w], qn_ref[...], bd)
            qg = (qg * (HEAD64 ** -0.5 * LOG2E)).astype(BF16)
            pieces += [qg * m for m in masks]
        st = _dot_nt(kb[keys], jnp.concatenate(pieces, axis=0))
        outs = []
        for h in range(SW_GROUP * SW_KV_HEADS):
            k = h % SW_KV_HEADS
            s = jnp.where(valid, st[:, h * WINDOW:(h + 1) * WINDOW], NEG)
            sink = sink_ref[h] * LOG2E
            m = jnp.maximum(jnp.max(s, axis=0, keepdims=True), sink)
            p = jnp.exp2(s - m)
            den = jnp.sum(p, axis=0, keepdims=True) + jnp.exp2(sink - m)
            outs.append(_dot(vvt[k * HEAD64:(k + 1) * HEAD64, keys], p.astype(BF16)) / den)
        tok_ref[rows, :] = jnp.concatenate(outs, axis=0).T.astype(tok_ref.dtype)

    @pl.when(n == pl.num_programs(1) - 1)
    def _():
        kout_ref[...] = kk[nwin * WINDOW:].T
        vout_ref[...] = vv[nwin * WINDOW:].T


def _swa_prompt(p, qn, kn, sinks, bd, nwin, mkb, mvt, layer, mqn):
    b, l, _ = p.shape
    w = SW_KV_HEADS * HEAD64
    nq = SW_GROUP
    tq = nwin * WINDOW
    cache = jax.ShapeDtypeStruct((b, w, WINDOW), F32)
    prev = lambda c: (lambda i, n: (i, jnp.maximum(n * nwin - 1, 0), c))
    mem_in, mem_out = _mem_prompt_specs(mkb, mvt, layer, nq + 2, tq)
    return pl.pallas_call(
        _swa_prompt_kernel,
        grid=(b, l // tq),
        in_specs=[
            pl.BlockSpec((None, tq, nq * w), lambda i, n: (i, n, 0)),
            pl.BlockSpec((None, tq, w), lambda i, n: (i, n, nq)),
            pl.BlockSpec((None, tq, w), lambda i, n: (i, n, nq + 1)),
            pl.BlockSpec((None, WINDOW, w), prev(nq)),
            pl.BlockSpec((None, WINDOW, w), prev(nq + 1)),
            pl.BlockSpec((1, w), lambda i, n: (0, 0)),
            pl.BlockSpec((1, w), lambda i, n: (0, 0)),
            pl.BlockSpec(memory_space=pltpu.SMEM),
            pl.BlockSpec((w, w), lambda i, n: (0, 0)),
        ] + mem_in,
        out_specs=[
            pl.BlockSpec((None, tq, nq * w), lambda i, n: (i, n, 0)),
            pl.BlockSpec((None, w, WINDOW), lambda i, n: (i, 0, 0)),
            pl.BlockSpec((None, w, WINDOW), lambda i, n: (i, 0, 0)),
            mem_out,
        ],
        out_shape=[jax.ShapeDtypeStruct((b, l, nq * w), BF16), cache, cache,
                   jax.ShapeDtypeStruct((b, l, mvt.shape[2]), BF16)],
        compiler_params=_params(("parallel", "arbitrary")),
        name="swa_prompt",
    )(p, p, p, p, p, qn, kn, sinks, bd, p, mkb, mvt, mqn, bd)


def _swa_sample_kernel(q_ref, kn_ref_in, vn_ref_in, ck_ref, cv_ref, qn_ref, kn_ref, sink_ref,
                       bd_ref, *rest, nl, nprev):
    if nprev:
        pk_ref, pv_ref, tok_ref, kstack_ref, vstack_ref = rest
        kstack_ref[0:nprev] = pk_ref[...]
        vstack_ref[0:nprev] = pv_ref[...]
    else:
        tok_ref, kstack_ref, vstack_ref = rest
    kout_ref = kstack_ref.at[nprev]
    vout_ref = vstack_ref.at[nprev]
    nb = ck_ref.shape[0]
    w = ck_ref.shape[1]
    rows = SAMPLE_PAD
    nkeys = 2 * WINDOW
    bd = bd_ref[...]
    nh = SW_GROUP * SW_KV_HEADS
    masks = [_head_mask(w, k) for k in range(SW_KV_HEADS)]
    row = lax.broadcasted_iota(jnp.int32, (nh * rows, nkeys), 0)
    r = lax.broadcasted_iota(jnp.int32, (nh * rows, nkeys), 1)
    j = row & (rows - 1)
    valid = (r > j) & (r <= j + WINDOW)
    hrow = lax.broadcasted_iota(jnp.int32, (nh * rows, 1), 0) >> 3
    sink = jnp.zeros((nh * rows, 1), F32)
    for h in range(nh):
        sink = jnp.where(hrow == h, sink_ref[h], sink)
    zpad = jnp.zeros((WINDOW - nl, w), F32)
    zq = jnp.zeros((rows - nl, w), F32)
    is_new = lax.broadcasted_iota(jnp.int32, (1, WINDOW), 1) < nl
    knew_all = _head_rms64(kn_ref_in[...], kn_ref[...], bd)
    vnew_all = vn_ref_in[...]
    q_all = [_head_rms64(q_ref[:, g * w:(g + 1) * w], qn_ref[...], bd) * HEAD64 ** -0.5
             for g in range(SW_GROUP)]
    toks = []
    for b in range(nb):
        mine = slice(b * nl, (b + 1) * nl)
        ck, cv = ck_ref[b], cv_ref[b]
        knt = jnp.concatenate([knew_all[mine], zpad], axis=0).T
        vnt = jnp.concatenate([vnew_all[mine], zpad], axis=0).T
        kt = jnp.concatenate([ck, knt], axis=1)
        vt = jnp.concatenate([cv, vnt], axis=1)
        kout_ref[b] = pltpu.roll(jnp.where(is_new, knt, ck), WINDOW - nl, axis=1)
        vout_ref[b] = pltpu.roll(jnp.where(is_new, vnt, cv), WINDOW - nl, axis=1)
        pieces = []
        for g in range(SW_GROUP):
            qg = jnp.concatenate([q_all[g][mine], zq], axis=0)
            pieces += [(qg * masks[k]).astype(BF16) for k in range(SW_KV_HEADS)]
        s = _dot(jnp.concatenate(pieces, axis=0), kt.astype(BF16))
        s = jnp.where(valid, s, NEG)
        m = jnp.maximum(jnp.max(s, axis=-1, keepdims=True), sink)
        p = jnp.exp(s - m)
        den = jnp.sum(p, axis=-1, keepdims=True) + jnp.exp(sink - m)
        o = _dot_nt(p.astype(BF16), vt.astype(BF16)) / den
        accs = []
        for g in range(SW_GROUP):
            acc = jnp.zeros((rows, w), F32)
            for k in range(SW_KV_HEADS):
                h = g * SW_KV_HEADS + k
                acc += o[h * rows:(h + 1) * rows] * masks[k]
            accs.append(acc[0:nl])
        toks.append(jnp.concatenate(accs, axis=1))
    tok_ref[...] = jnp.concatenate(toks, axis=0).astype(tok_ref.dtype)


def _swa_sample(p, ckt, cvt, layer, qn, kn, sinks, bd, nl, nb, prev):
    b = ckt.shape[1]
    w = ckt.shape[2]
    nq = SW_GROUP
    nprev = 0 if prev is None else prev[0].shape[0]
    cache = jax.ShapeDtypeStruct((nprev + 1, b, w, WINDOW), F32)
    c4 = lambda i: (layer, i, 0, 0)
    stack = lambda n: pl.BlockSpec((n, nb, w, WINDOW), lambda i: (0, i, 0, 0))
    return pl.pallas_call(
        functools.partial(_swa_sample_kernel, nl=nl, nprev=nprev),
        grid=(b // nb,),
        in_specs=[
            pl.BlockSpec((nb * nl, nq * w), lambda i: (i, 0)),
            pl.BlockSpec((nb * nl, w), lambda i: (i, nq)),
            pl.BlockSpec((nb * nl, w), lambda i: (i, nq + 1)),
            pl.BlockSpec((None, nb, w, WINDOW), c4),
            pl.BlockSpec((None, nb, w, WINDOW), c4),
            pl.BlockSpec((1, w), lambda i: (0, 0)),
            pl.BlockSpec((1, w), lambda i: (0, 0)),
            pl.BlockSpec(memory_space=pltpu.SMEM),
            pl.BlockSpec((w, w), lambda i: (0, 0)),
        ] + ([stack(nprev)] * 2 if nprev else []),
        out_specs=[
            pl.BlockSpec((nb * nl, nq * w), lambda i: (i, 0)),
            stack(nprev + 1),
            stack(nprev + 1),
        ],
        out_shape=[jax.ShapeDtypeStruct((b * nl, nq * w), BF16), cache, cache],
        compiler_params=_params(("parallel",)),
        name="swa_sample",
    )(p, p, p, ckt, cvt, qn, kn, sinks, bd, *(() if prev is None else prev))


def _hgrn_consts(rows, group):
    t = np.arange(rows)
    loc, run = t % group, t // group
    same = run[:, None] == run[None, :]
    r = t[None, :]
    mats = [same & (r <= t[:, None]), same & (r > t[:, None])]
    masks = []
    h = group // 2
    while h >= 1:
        par = loc // (2 * h)
        right = (loc % (2 * h)) >= h
        bnd = (run * group + par * 2 * h + h - 1)[:, None]
        q_side = right[:, None] & (r > bnd) & (r <= t[:, None])
        k_side = (~right)[:, None] & (r > t[:, None]) & (r <= bnd)
        mats.append(q_side | k_side)
        masks.append(same & (par[:, None] == par[None, :]) & right[:, None] & (~right)[None, :])
        h //= 2
    masks.append(np.eye(rows, dtype=bool))
    w = np.concatenate(mats, axis=0).astype(np.float32)
    w2 = np.concatenate([w, w], axis=1)
    return jnp.asarray(w2, BF16), jnp.asarray(np.stack(masks).astype(np.float32)), len(masks) - 1


def _hgrn_gates(q_raw, z, lb):
    sp, sn = _sigmoid_pair(z)
    log2f = jnp.log2(lb + (1.0 - lb) * sp)
    kf = (1.0 - lb) * sn
    q = q_raw * _sigmoid(q_raw) * HG_HEAD ** -0.5
    return q, kf, log2f


def _hgrn_decays(log2f, w2_ref):
    hi = log2f.astype(BF16)
    lo = (log2f - hi.astype(F32)).astype(BF16)
    return jnp.exp2(_dot(w2_ref[...], jnp.concatenate([hi, lo], axis=0)))


def _hgrn_intra(q, kf, v, e, m_ref, nlev):
    rows = q.shape[0]
    att = m_ref[nlev] * _dot_nt(q.astype(BF16), kf.astype(BF16))
    for l in range(nlev):
        el = e[(2 + l) * rows:(3 + l) * rows]
        att += m_ref[l] * _dot_nt((q * el).astype(BF16), (kf * el).astype(BF16))
    return _dot(att.astype(BF16), v.astype(BF16))


def _hgrn_out(o, g, gn):
    ms = jnp.mean(o * o, axis=-1, keepdims=True)
    return (o * lax.rsqrt(ms + EPS) * gn * (g * _sigmoid(g))).astype(BF16)


def _hgrn_prompt_kernel(q_ref, f_ref, i_ref, g_ref, lb_ref, gn_ref, w2_ref, m_ref,
                        cq_ref, mkb_ref, mvt_ref, mqn_ref, mbd_ref,
                        tok_ref, st_ref, mo_ref, s_scr, *, nlev):
    t = pl.program_id(1)

    @pl.when(t == 0)
    def _():
        s_scr[...] = jnp.zeros_like(s_scr)

    mo_ref[...] = _mem_attend_rows(cq_ref[...], mkb_ref[...], mvt_ref[...], mqn_ref[...],
                                   mbd_ref[...]).astype(mo_ref.dtype)

    lb = lb_ref[...]
    gn = gn_ref[...]
    rows = HG_ROWS
    nh = q_ref.shape[1] // HG_HEAD

    def chunk(c, st):
        sl = pl.ds(pl.multiple_of(c * rows, rows), rows)
        v = i_ref[sl, :]
        g = g_ref[sl, :]
        q, kf, log2f = _hgrn_gates(q_ref[sl, :], f_ref[sl, :], lb)
        e = _hgrn_decays(log2f, w2_ref)
        qt = (q * e[0:rows]).astype(BF16)
        kh = (kf * e[rows:2 * rows]).astype(BF16)
        vb = v.astype(BF16)
        stb = st.astype(BF16)
        qb = q.astype(BF16)
        kb = kf.astype(BF16)
        ql, kl = [qb], [kb]
        rowi = lax.broadcasted_iota(jnp.int32, (rows, 1), 0)
        for l in range(nlev):
            half = rows >> (l + 1)
            if half >= 8:
                base = jnp.concatenate([(q if (r0 // half) & 1 else kf)[r0:r0 + half]
                                        for r0 in range(0, rows, half)], axis=0)
            else:
                base = jnp.where((rowi & (2 * half - 1)) >= half, q, kf)
            x = (base * e[(2 + l) * rows:(3 + l) * rows]).astype(BF16)
            ql.append(x)
            kl.append(x)
        mk = [m_ref[nlev]] + [m_ref[l] for l in range(nlev)]
        zr = jnp.zeros((rows, HG_HEAD), BF16)
        zs = jnp.zeros((HG_HEAD, HG_HEAD), BF16)

        def bdiag(a, b, z):
            return jnp.concatenate([jnp.concatenate([a, z], axis=1),
                                    jnp.concatenate([z, b], axis=1)], axis=0)

        upd = []
        for p in range(nh // 2):
            h0 = slice(2 * p * HG_HEAD, (2 * p + 1) * HG_HEAD)
            h1 = slice((2 * p + 1) * HG_HEAD, (2 * p + 2) * HG_HEAD)
            pr = slice(2 * p * HG_HEAD, (2 * p + 2) * HG_HEAD)
            att = jnp.zeros((rows, 2 * rows), F32)
            for a, b, m in zip(ql, kl, mk):
                att += m * _dot_nt(a[:, pr], bdiag(b[:, h0], b[:, h1], zr))
            o = _dot(att.astype(BF16), bdiag(vb[:, h0], vb[:, h1], zr))
            o += _dot_nt(qt[:, pr], bdiag(stb[:, h0], stb[:, h1], zs))
            for hs in (h0, h1):
                lo = hs.start - pr.start
                tok_ref[sl, hs] = _hgrn_out(o[:, lo:lo + HG_HEAD], g[:, hs], gn[:, hs])
                upd.append(_dot_tn(vb[:, hs], kh[:, hs]))
        return st * e[rows - 1:rows, :] + jnp.concatenate(upd, axis=1)

    st = lax.fori_loop(0, q_ref.shape[0] // rows, chunk, s_scr[...], unroll=True)
    s_scr[...] = st

    @pl.when(t == pl.num_programs(1) - 1)
    def _():
        for h in range(nh):
            st_ref[h] = st[:, h * HG_HEAD:(h + 1) * HG_HEAD].T


def _hgrn_prompt(p, lb, gn, tb, mkb, mvt, layer, mqn, bd):
    b, l, _ = p.shape
    tw = lb.shape[1]
    nh = tw // HG_HEAD
    mw = mvt.shape[2]
    w2, masks, nlev = _hgrn_consts(HG_ROWS, HG_ROWS)
    masks = jnp.tile(masks, (1, 1, 2))
    sec = lambda s: pl.BlockSpec((None, tb, tw), lambda i, t: (i, t, s))
    mem_in, mem_out = _mem_prompt_specs(mkb, mvt, layer, 4 * tw // mw, tb)
    return pl.pallas_call(
        functools.partial(_hgrn_prompt_kernel, nlev=nlev),
        grid=(b, l // tb),
        in_specs=[
            sec(0), sec(1), sec(2), sec(3),
            pl.BlockSpec((1, tw), lambda i, t: (0, 0)),
            pl.BlockSpec((1, tw), lambda i, t: (0, 0)),
            pl.BlockSpec(w2.shape, lambda i, t: (0, 0)),
            pl.BlockSpec(masks.shape, lambda i, t: (0, 0, 0)),
        ] + mem_in,
        out_specs=[
            pl.BlockSpec((None, tb, tw), lambda i, t: (i, t, 0)),
            pl.BlockSpec((None, nh, HG_HEAD, HG_HEAD), lambda i, t: (i, 0, 0, 0)),
            mem_out,
        ],
        out_shape=[jax.ShapeDtypeStruct((b, l, tw), BF16),
                   jax.ShapeDtypeStruct((b, nh, HG_HEAD, HG_HEAD), F32),
                   jax.ShapeDtypeStruct((b, l, mw), BF16)],
        scratch_shapes=[pltpu.VMEM((HG_HEAD, tw), F32)],
        compiler_params=_params(("parallel", "arbitrary")),
        name="hgrn_prompt",
    )(p, p, p, p, lb, gn, w2, masks, p, mkb, mvt, mqn, bd)


def _hgrn_sample_kernel(q_ref, f_ref, i_ref, g_ref, s0_ref, lb_ref, gn_ref, w2_ref, m_ref,
                        *rest, nlev, nl, nprev):
    if nprev:
        prev_ref, tok_ref, stack_ref = rest
        stack_ref[0:nprev] = prev_ref[...]
    else:
        tok_ref, stack_ref = rest
    st_ref = stack_ref.at[nprev]
    rows = q_ref.shape[0]
    nb = rows // nl
    shift = nl.bit_length() - 1
    rowb = lax.broadcasted_iota(jnp.int32, (rows, 1), 0) >> shift
    colb = lax.broadcasted_iota(jnp.int32, (rows, nb * HG_HEAD), 1) >> 7
    v_all = i_ref[...]
    q_all, kf_all, log2f = _hgrn_gates(q_ref[...], f_ref[...], lb_ref[...])
    e_all = _hgrn_decays(log2f, w2_ref)
    g_all = g_ref[...]
    gn_all = gn_ref[...]
    for hh in range(q_ref.shape[1] // HG_HEAD):
        hs = slice(hh * HG_HEAD, (hh + 1) * HG_HEAD)
        q, kf, v, e = q_all[:, hs], kf_all[:, hs], v_all[:, hs], e_all[:, hs]
        o_intra = _hgrn_intra(q, kf, v, e, m_ref, nlev)
        e_cum = e[0:rows]
        qt = q * e_cum
        kh = kf * e[rows:2 * rows]
        s0cat = jnp.concatenate([s0_ref[b, hh].astype(BF16) for b in range(nb)], axis=1)
        full = _dot(qt.astype(BF16), s0cat)
        o = o_intra
        for b in range(nb):
            o += jnp.where(rowb == b, full[:, b * HG_HEAD:(b + 1) * HG_HEAD], 0.0)
        tok_ref[:, hs] = _hgrn_out(o, g_all[:, hs], gn_all[:, hs])
        xt = jnp.concatenate([kh, e_cum], axis=0).T
        vbd = jnp.where(rowb == colb, jnp.concatenate([v] * nb, axis=1), 0.0).astype(BF16)
        upd = _dot(xt[:, 0:rows].astype(BF16), vbd)
        for b in range(nb):
            last = rows + b * nl + nl - 1
            st_ref[b, hh] = (s0_ref[b, hh] * xt[:, last:last + 1]
                             + upd[:, b * HG_HEAD:(b + 1) * HG_HEAD])


def _hgrn_sample(p, s0, layer, lb, gn, nl, prev):
    rows = HG_ROWS
    nb = rows // nl
    b, nh = s0.shape[1], s0.shape[2]
    hp = 2
    hw = hp * HG_HEAD
    nprev = 0 if prev is None else prev.shape[0]
    w2, masks, nlev = _hgrn_consts(rows, nl)
    sec = lambda s: pl.BlockSpec((rows, hw), lambda i, h: (i, s * (nh // hp) + h))
    stack = lambda n: pl.BlockSpec((n, nb, hp, HG_HEAD, HG_HEAD), lambda i, h: (0, i, h, 0, 0))
    s0_spec = pl.BlockSpec((None, nb, hp, HG_HEAD, HG_HEAD), lambda i, h: (layer, i, h, 0, 0))
    return pl.pallas_call(
        functools.partial(_hgrn_sample_kernel, nlev=nlev, nl=nl, nprev=nprev),
        grid=(b // nb, nh // hp),
        in_specs=[
            sec(0), sec(1), sec(2), sec(3), s0_spec,
            pl.BlockSpec((1, hw), lambda i, h: (0, h)),
            pl.BlockSpec((1, hw), lambda i, h: (0, h)),
            pl.BlockSpec(w2.shape, lambda i, h: (0, 0)),
            pl.BlockSpec(masks.shape, lambda i, h: (0, 0, 0)),
        ] + ([stack(nprev)] if nprev else []),
        out_specs=[pl.BlockSpec((rows, hw), lambda i, h: (i, h)), stack(nprev + 1)],
        out_shape=[jax.ShapeDtypeStruct((b * nl, nh * HG_HEAD), BF16),
                   jax.ShapeDtypeStruct((nprev + 1,) + s0.shape[1:], F32)],
        compiler_params=_params(("parallel", "parallel")),
        name="hgrn_sample",
    )(p, p, p, p, s0, lb, gn, w2, masks, *(() if prev is None else (prev,)))


def _swa_regroup(w, axis):
    axis %= w.ndim
    shape = w.shape[:axis] + (SW_KV_HEADS, SW_GROUP, HEAD64) + w.shape[axis + 1:]
    return jnp.swapaxes(w.reshape(shape), axis, axis + 1).reshape(w.shape)


def kernel(x_prompt, x_sample, cache_mem_k, cache_mem_v, state_hgrn, cache_swa_k, cache_swa_v,
           mem_prompt, norm_mix, norm_ffn, norm_mem, w_in_hgrn, hgrn_lb_raw, hgrn_out_norm,
           w_in_swa, swa_q_norm, swa_k_norm, swa_sinks, w_mem_kv, mem_q_norm, mem_k_norm,
           w_out, w_ffn_in, w_ffn_out):
    depth, d = norm_mix.shape
    bp, lp, _ = x_prompt.shape
    bs, ls, _ = x_sample.shape
    nm = mem_prompt.shape[1]
    mw = MEM_HEADS * HEAD64
    kvw = SW_KV_HEADS * HEAD64
    tokw = d - mw
    tm = DENSE_ROWS

    seg = np.arange(mw) // HEAD64
    bd = jnp.asarray((seg[:, None] == seg[None, :]).astype(np.float32) / HEAD64, BF16)
    tile4 = lambda a: jnp.tile(a.astype(F32), (1, mw // HEAD64))[:, None, :]

    sm = jax.nn.softmax(hgrn_lb_raw.astype(F32), axis=0)
    lbs = jnp.clip(jnp.cumsum(sm, axis=0) - sm[0], 0.0, LB_MAX)

    w_in_swa_b = w_in_swa.astype(BF16)
    w_in_swa_p = jnp.concatenate([_swa_regroup(w_in_swa_b[..., :tokw], -1),
                                  w_in_swa_b[..., tokw:]], axis=-1)
    w_in_hgrn_b = w_in_hgrn.astype(BF16)
    w_out_b = w_out.astype(BF16)
    w_out_b = jnp.stack([w_out_b[i] if i % 2 == 0 else
                         jnp.concatenate([_swa_regroup(w_out_b[i, :tokw], 0), w_out_b[i, tokw:]],
                                         axis=0)
                         for i in range(depth)])
    w1_b = w_ffn_in.astype(BF16)
    w2_b = w_ffn_out.astype(BF16)
    sinks_p = swa_sinks.astype(F32).reshape(-1, SW_KV_HEADS, SW_GROUP).transpose(0, 2, 1)
    sinks_p = sinks_p.reshape(-1, SW_KV_HEADS * SW_GROUP)
    mem_qn = tile4(mem_q_norm)
    swa_qn = tile4(swa_q_norm)
    swa_kn = tile4(swa_k_norm)

    mk_p, mv_p, mk_b = _mem_kv(mem_prompt, norm_mem[:, None, :], w_mem_kv.astype(BF16),
                               tile4(mem_k_norm), bd)

    to_t = lambda a: a.transpose(0, 1, 3, 4, 2).reshape(a.shape[0], a.shape[1], -1, a.shape[2])
    from_t = lambda a, heads: a.reshape(a.shape[:-2] + (heads, HEAD64, a.shape[-1])).transpose(
        *range(a.ndim - 2), a.ndim, a.ndim - 2, a.ndim - 1)
    cmk_t, cmv_t = to_t(cache_mem_k), to_t(cache_mem_v)
    csk_t, csv_t = to_t(cache_swa_k), to_t(cache_swa_v)

    xp = x_prompt.reshape(bp * lp, d)
    xs = x_sample.reshape(bs * ls, d)
    g_mix = norm_mix[:, None, :]
    g_ffn = norm_ffn[:, None, :]
    hg_p, swk_p, swv_p = [], [], []
    hg_s = sw_s = None
    for i in range(depth):
        j = i // 2
        if i % 2 == 0:
            lb = lbs[j][None, :]
            gn = hgrn_out_norm[j][None, :].astype(F32)
            pp = _norm_matmul(xp, g_mix, i, w_in_hgrn_b, j, tm).reshape(bp, lp, -1)
            tok_p, st_p, mo_p = _hgrn_prompt(pp, lb, gn, HGRN_ROWS, mk_b, mv_p, i, mem_qn[i], bd)
            ps = _norm_matmul(xs, g_mix, i, w_in_hgrn_b, j, tm)
            tok_s, hg_s = _hgrn_sample(ps, state_hgrn, j, lb, gn, ls, hg_s)
            hg_p.append(st_p)
            cq_col = 4 * tokw // mw
        else:
            pp = _norm_matmul(xp, g_mix, i, w_in_swa_p, j, tm).reshape(bp, lp, -1)
            tok_p, kb, vb, mo_p = _swa_prompt(pp, swa_qn[j], swa_kn[j], sinks_p[j], bd,
                                              SWA_WINDOWS, mk_b, mv_p, i, mem_qn[i])
            swk_p.append(kb)
            swv_p.append(vb)
            ps = _norm_matmul(xs, g_mix, i, w_in_swa_p, j, tm)
            tok_s, *sw_s = _swa_sample(ps, csk_t, csv_t, j, swa_qn[j], swa_kn[j],
                                       sinks_p[j], bd, ls, SWA_DECODE_SEQS, sw_s)
            cq_col = (tokw + 2 * kvw) // mw
        mo_s = _mem_attend(ps, cq_col, cmk_t, cmv_t, i, mem_qn[i], bd, MEM_DECODE_SEQS, ls)
        xp, xs = _mix_ffn(xp, tok_p.reshape(bp * lp, tokw), mo_p.reshape(bp * lp, mw),
                          xs, tok_s, mo_s, w_out_b, g_ffn, w1_b, w2_b, i, tm)

    return (xp.reshape(bp, lp, d), xs.reshape(bs, ls, d),
            from_t(mk_p, MEM_HEADS), from_t(mv_p, MEM_HEADS),
            jnp.stack(hg_p), hg_s,
            from_t(jnp.stack(swk_p), SW_KV_HEADS), from_t(jnp.stack(swv_p), SW_KV_HEADS),
            from_t(sw_s[0], SW_KV_HEADS), from_t(sw_s[1], SW_KV_HEADS))
```

```python
import functools

import numpy as np
import jax
import jax.numpy as jnp
from jax import lax
from jax.experimental import pallas as pl
from jax.experimental.pallas import tpu as pltpu

F32 = jnp.float32
BF16 = jnp.bfloat16
EPS = 1e-6
NEG = -1e30
LB_MAX = 0.999
LOG2E = 1.4426950408889634

HEAD64 = 64
MEM_HEADS = 4
SW_KV_HEADS = 4
SW_GROUP = 3
WINDOW = 128
HG_HEAD = 128
HG_ROWS = 64
SAMPLE_PAD = 8

V7X_VMEM_BYTES = 64 * 1024 * 1024
VMEM_LIMIT = V7X_VMEM_BYTES - 8 * 1024 * 1024

DENSE_ROWS = 512
HGRN_ROWS = 1024
SWA_WINDOWS = 8
SWA_DECODE_SEQS = 8
MEM_DECODE_SEQS = 16


def _params(sem):
    return pltpu.CompilerParams(dimension_semantics=sem, vmem_limit_bytes=VMEM_LIMIT)


def _dot(a, b):
    return jnp.dot(a, b, preferred_element_type=F32)


def _dot_nt(a, b):
    return lax.dot_general(a, b, (((1,), (1,)), ((), ())), preferred_element_type=F32)


def _dot_tn(a, b):
    return lax.dot_general(a, b, (((0,), (0,)), ((), ())), preferred_element_type=F32)


def _sigmoid(x):
    return 1.0 / (1.0 + jnp.exp(-x))


def _sigmoid_pair(z):
    e = jnp.exp(-jnp.abs(z))
    r = 1.0 / (1.0 + e)
    er = e * r
    pos = z >= 0
    return jnp.where(pos, r, er), jnp.where(pos, er, r)


def _rms_rows(x, g):
    ms = jnp.mean(x * x, axis=-1, keepdims=True)
    return x * lax.rsqrt(ms + EPS) * g


def _head_rms64(x, g, bd):
    ms = _dot((x * x).astype(BF16), bd)
    return x * lax.rsqrt(ms + EPS) * g


def _head_mask(width, h):
    lane = lax.broadcasted_iota(jnp.int32, (1, width), 1)
    return ((lane >> 6) == h).astype(F32)


def _norm_matmul_kernel(xp_ref, xs_ref, g_ref, w_ref, op_ref, os_ref, *, np_steps):
    is_p = pl.program_id(0) < np_steps

    def tile(x_ref, o_ref):
        h = _rms_rows(x_ref[...], g_ref[...]).astype(BF16)
        o_ref[...] = _dot(h, w_ref[...])

    @pl.when(is_p)
    def _():
        tile(xp_ref, op_ref)

    @pl.when(jnp.logical_not(is_p))
    def _():
        tile(xs_ref, os_ref)


def _norm_matmul(xp, xs, g, g_idx, w, w_idx, tm):
    mp, d = xp.shape
    ms = xs.shape[0]
    n = w.shape[2]
    tm = min(tm, mp, ms)
    np_steps, ns_steps = mp // tm, ms // tm
    pidx = lambda i: (jnp.minimum(i, np_steps - 1), 0)
    sidx = lambda i: (jnp.maximum(i - np_steps, 0), 0)
    return pl.pallas_call(
        functools.partial(_norm_matmul_kernel, np_steps=np_steps),
        grid=(np_steps + ns_steps,),
        in_specs=[
            pl.BlockSpec((tm, d), pidx),
            pl.BlockSpec((tm, d), sidx),
            pl.BlockSpec((None, 1, d), lambda i: (g_idx, 0, 0)),
            pl.BlockSpec((None, d, n), lambda i: (w_idx, 0, 0), pipeline_mode=pl.Buffered(1)),
        ],
        out_specs=[pl.BlockSpec((tm, n), pidx), pl.BlockSpec((tm, n), sidx)],
        out_shape=[jax.ShapeDtypeStruct((mp, n), F32), jax.ShapeDtypeStruct((ms, n), F32)],
        compiler_params=_params(("arbitrary",)),
        name="norm_matmul",
    )(xp, xs, g, w)


def _mix_ffn_kernel(xp_ref, tokp_ref, mop_ref, xs_ref, toks_ref, mos_ref, wo_ref, g_ref, w1_ref,
                    w2_ref, op_ref, os_ref, act_ref, *, ff, chunk, np_steps):
    is_p = pl.program_id(0) < np_steps

    def tile(x_ref, tok_ref, mo_ref, o_ref):
        tw = tok_ref.shape[1]
        x1 = (x_ref[...] + _dot(tok_ref[...], wo_ref[0:tw, :])
              + _dot(mo_ref[...], wo_ref[tw:, :]))
        o_ref[...] = x1
        h = _rms_rows(x1, g_ref[...]).astype(BF16)
        for c0 in range(0, ff, chunk):
            g = _dot(h, w1_ref[:, c0:c0 + chunk])
            u = _dot(h, w1_ref[:, ff + c0:ff + c0 + chunk])
            act_ref[:, c0:c0 + chunk] = (g * _sigmoid(g) * u).astype(BF16)
        o_ref[...] += _dot(act_ref[...], w2_ref[...])

    @pl.when(is_p)
    def _():
        tile(xp_ref, tokp_ref, mop_ref, op_ref)

    @pl.when(jnp.logical_not(is_p))
    def _():
        tile(xs_ref, toks_ref, mos_ref, os_ref)


def _mix_ffn(xp, tokp, mop, xs, toks, mos, wo, g, w1, w2, layer, tm):
    mp, d = xp.shape
    ms = xs.shape[0]
    ff = w2.shape[1]
    tm = min(tm, mp, ms)
    np_steps, ns_steps = mp // tm, ms // tm
    tw, mw = tokp.shape[1], mop.shape[1]
    const = lambda i: (layer, 0, 0)
    pidx = lambda i: (jnp.minimum(i, np_steps - 1), 0)
    sidx = lambda i: (jnp.maximum(i - np_steps, 0), 0)
    rows = lambda width, idx: pl.BlockSpec((tm, width), idx)
    return pl.pallas_call(
        functools.partial(_mix_ffn_kernel, ff=ff, chunk=256, np_steps=np_steps),
        grid=(np_steps + ns_steps,),
        in_specs=[
            rows(d, pidx), rows(tw, pidx), rows(mw, pidx),
            rows(d, sidx), rows(tw, sidx), rows(mw, sidx),
            pl.BlockSpec((None, tw + mw, d), const, pipeline_mode=pl.Buffered(1)),
            pl.BlockSpec((None, 1, d), const),
            pl.BlockSpec((None, d, 2 * ff), const, pipeline_mode=pl.Buffered(1)),
            pl.BlockSpec((None, ff, d), const, pipeline_mode=pl.Buffered(1)),
        ],
        out_specs=[rows(d, pidx), rows(d, sidx)],
        out_shape=[jax.ShapeDtypeStruct((mp, d), F32), jax.ShapeDtypeStruct((ms, d), F32)],
        scratch_shapes=[pltpu.VMEM((tm, ff), BF16)],
        compiler_params=_params(("arbitrary",)),
        name="mix_ffn",
    )(xp, tokp, mop, xs, toks, mos, wo, g, w1, w2)


def _mem_kv_kernel(mem_ref, g_ref, w_ref, kn_ref, bd_ref, k_ref, v_ref, kb_ref):
    nb, nm, d = mem_ref.shape
    kw = k_ref.shape[1]
    h = _rms_rows(mem_ref[...].reshape(nb * nm, d), g_ref[...]).astype(BF16)
    kv = _dot(h, w_ref[...])
    k_all = _head_rms64(kv[:, :kw], kn_ref[...], bd_ref[...])
    for b in range(nb):
        k = k_all[b * nm:(b + 1) * nm]
        k_ref[b] = k.T
        v_ref[b] = kv[b * nm:(b + 1) * nm, kw:].T
        kb_ref[b] = jnp.concatenate([(k * _head_mask(kw, hd)).astype(BF16)
                                     for hd in range(MEM_HEADS)], axis=0)


def _mem_kv(mem, g, w, kn, bd):
    depth = w.shape[0]
    b, nm, d = mem.shape
    kw = w.shape[2] // 2
    out = jax.ShapeDtypeStruct((depth, b, kw, nm), F32)
    out_b = jax.ShapeDtypeStruct((depth, b, MEM_HEADS * nm, kw), BF16)
    return pl.pallas_call(
        _mem_kv_kernel,
        grid=(depth,),
        in_specs=[
            pl.BlockSpec((b, nm, d), lambda i: (0, 0, 0)),
            pl.BlockSpec((None, 1, d), lambda i: (i, 0, 0)),
            pl.BlockSpec((None, d, 2 * kw), lambda i: (i, 0, 0)),
            pl.BlockSpec((None, 1, kw), lambda i: (i, 0, 0)),
            pl.BlockSpec((kw, kw), lambda i: (0, 0)),
        ],
        out_specs=[pl.BlockSpec((None, b, kw, nm), lambda i: (i, 0, 0, 0))] * 2
        + [pl.BlockSpec((None, b, MEM_HEADS * nm, kw), lambda i: (i, 0, 0, 0))],
        out_shape=[out, out, out_b],
        compiler_params=_params(("parallel",)),
        name="mem_kv",
    )(mem, g, w, kn, bd)


def _mem_attend_kernel(q_ref, k_ref, v_ref, qn_ref, bd_ref, o_ref, *, nl):
    nb = k_ref.shape[0]
    w = q_ref.shape[1]
    tq = SAMPLE_PAD
    masks = [_head_mask(w, h) for h in range(MEM_HEADS)]
    qn_all = _head_rms64(q_ref[...], qn_ref[...], bd_ref[...]) * HEAD64 ** -0.5
    zpad = jnp.zeros((tq - nl, w), F32)
    outs = []
    for b in range(nb):
        qn = jnp.concatenate([qn_all[b * nl:(b + 1) * nl], zpad], axis=0)
        kt = k_ref[b].astype(BF16)
        vt = v_ref[b].astype(BF16)
        qbd = jnp.concatenate([(qn * m).astype(BF16) for m in masks], axis=0)
        s = _dot(qbd, kt)
        p = jnp.exp(s - jnp.max(s, axis=-1, keepdims=True))
        o = _dot_nt(p.astype(BF16), vt) / jnp.sum(p, axis=-1, keepdims=True)
        acc = o[0:tq] * masks[0]
        for h in range(1, MEM_HEADS):
            acc += o[h * tq:(h + 1) * tq] * masks[h]
        outs.append(acc[0:nl])
    o_ref[...] = jnp.concatenate(outs, axis=0).astype(o_ref.dtype)


def _mem_attend_rows(q, kb, vt, qn, bd):
    nm = vt.shape[1]
    qn = _head_rms64(q, qn, bd) * (HEAD64 ** -0.5 * LOG2E)
    st = _dot_nt(kb, qn.astype(BF16))
    vt = vt.astype(BF16)
    outs = []
    for h in range(MEM_HEADS):
        s = st[h * nm:(h + 1) * nm]
        p = jnp.exp2(s - jnp.max(s, axis=0, keepdims=True))
        den = jnp.sum(p, axis=0, keepdims=True)
        outs.append(_dot(vt[h * HEAD64:(h + 1) * HEAD64, :], p.astype(BF16)) / den)
    return jnp.concatenate(outs, axis=0).T


def _mem_prompt_specs(mkb, mvt, layer, col, tq):
    w, nm = mvt.shape[2], mvt.shape[3]
    ins = [
        pl.BlockSpec((None, tq, w), lambda i, t: (i, t, col)),
        pl.BlockSpec((None, None, MEM_HEADS * nm, w), lambda i, t: (layer, i, 0, 0)),
        pl.BlockSpec((None, None, w, nm), lambda i, t: (layer, i, 0, 0)),
        pl.BlockSpec((1, w), lambda i, t: (0, 0)),
        pl.BlockSpec((w, w), lambda i, t: (0, 0)),
    ]
    return ins, pl.BlockSpec((None, tq, w), lambda i, t: (i, t, 0))


def _mem_attend(q, col, mkt, mvt, layer, qn, bd, nb, nl):
    bsz = mkt.shape[1]
    w, nm = mkt.shape[2], mkt.shape[3]
    return pl.pallas_call(
        functools.partial(_mem_attend_kernel, nl=nl),
        grid=(bsz // nb,),
        in_specs=[
            pl.BlockSpec((nb * nl, w), lambda i: (i, col)),
            pl.BlockSpec((None, nb, w, nm), lambda i: (layer, i, 0, 0)),
            pl.BlockSpec((None, nb, w, nm), lambda i: (layer, i, 0, 0)),
            pl.BlockSpec((1, w), lambda i: (0, 0)),
            pl.BlockSpec((w, w), lambda i: (0, 0)),
        ],
        out_specs=pl.BlockSpec((nb * nl, w), lambda i: (i, 0)),
        out_shape=jax.ShapeDtypeStruct((bsz * nl, w), BF16),
        compiler_params=_params(("parallel",)),
        name="mem_attend",
    )(q, mkt, mvt, qn, bd)


def _swa_prompt_kernel(q_ref, kc_ref, vc_ref, kp_ref, vp_ref, qn_ref, kn_ref, sink_ref, bd_ref,
                       cq_ref, mkb_ref, mvt_ref, mqn_ref, mbd_ref,
                       tok_ref, kout_ref, vout_ref, mo_ref):
    n = pl.program_id(1)
    bd = bd_ref[...]
    mo_ref[...] = _mem_attend_rows(cq_ref[...], mkb_ref[...], mvt_ref[...], mqn_ref[...],
                                   mbd_ref[...]).astype(mo_ref.dtype)
    w = kc_ref.shape[-1]
    nwin = kc_ref.shape[0] // WINDOW
    kk = _head_rms64(jnp.concatenate([kp_ref[...], kc_ref[...]], axis=0), kn_ref[...], bd)
    vv = jnp.concatenate([vp_ref[...], vc_ref[...]], axis=0)
    kb = kk.astype(BF16)
    vvt = vv.T.astype(BF16)
    r = lax.broadcasted_iota(jnp.int32, (2 * WINDOW, WINDOW), 0)
    qi = lax.broadcasted_iota(jnp.int32, (2 * WINDOW, WINDOW), 1)
    band = (r > qi) & (r <= qi + WINDOW)
    masks = [_head_mask(w, k).astype(BF16) for k in range(SW_KV_HEADS)]
    for i in range(nwin):
        rows = slice(i * WINDOW, (i + 1) * WINDOW)
        keys = slice(i * WINDOW, (i + 2) * WINDOW)
        valid = band if i else band & ((r >= WINDOW) | (n > 0))
        pieces = []
        for g in range(SW_GROUP):
            qg = _head_rms64(q_ref[rows, g * w:(g + 1) * w], qn_ref[...], bd)
            qg = (qg * (HEAD64 ** -0.5 * LOG2E)).astype(BF16)
            pieces += [qg * m for m in masks]
        st = _dot_nt(kb[keys], jnp.concatenate(pieces, axis=0))
        outs = []
        for h in range(SW_GROUP * SW_KV_HEADS):
            k = h % SW_KV_HEADS
            s = jnp.where(valid, st[:, h * WINDOW:(h + 1) * WINDOW], NEG)
            sink = sink_ref[h] * LOG2E
            m = jnp.maximum(jnp.max(s, axis=0, keepdims=True), sink)
            p = jnp.exp2(s - m)
            den = jnp.sum(p, axis=0, keepdims=True) + jnp.exp2(sink - m)
            outs.append(_dot(vvt[k * HEAD64:(k + 1) * HEAD64, keys], p.astype(BF16)) / den)
        tok_ref[rows, :] = jnp.concatenate(outs, axis=0).T.astype(tok_ref.dtype)

    @pl.when(n == pl.num_programs(1) - 1)
    def _():
        kout_ref[...] = kk[nwin * WINDOW:].T
        vout_ref[...] = vv[nwin * WINDOW:].T


def _swa_prompt(p, qn, kn, sinks, bd, nwin, mkb, mvt, layer, mqn):
    b, l, _ = p.shape
    w = SW_KV_HEADS * HEAD64
    nq = SW_GROUP
    tq = nwin * WINDOW
    cache = jax.ShapeDtypeStruct((b, w, WINDOW), F32)
    prev = lambda c: (lambda i, n: (i, jnp.maximum(n * nwin - 1, 0), c))
    mem_in, mem_out = _mem_prompt_specs(mkb, mvt, layer, nq + 2, tq)
    return pl.pallas_call(
        _swa_prompt_kernel,
        grid=(b, l // tq),
        in_specs=[
            pl.BlockSpec((None, tq, nq * w), lambda i, n: (i, n, 0)),
            pl.BlockSpec((None, tq, w), lambda i, n: (i, n, nq)),
            pl.BlockSpec((None, tq, w), lambda i, n: (i, n, nq + 1)),
            pl.BlockSpec((None, WINDOW, w), prev(nq)),
            pl.BlockSpec((None, WINDOW, w), prev(nq + 1)),
            pl.BlockSpec((1, w), lambda i, n: (0, 0)),
            pl.BlockSpec((1, w), lambda i, n: (0, 0)),
            pl.BlockSpec(memory_space=pltpu.SMEM),
            pl.BlockSpec((w, w), lambda i, n: (0, 0)),
        ] + mem_in,
        out_specs=[
            pl.BlockSpec((None, tq, nq * w), lambda i, n: (i, n, 0)),
            pl.BlockSpec((None, w, WINDOW), lambda i, n: (i, 0, 0)),
            pl.BlockSpec((None, w, WINDOW), lambda i, n: (i, 0, 0)),
            mem_out,
        ],
        out_shape=[jax.ShapeDtypeStruct((b, l, nq * w), BF16), cache, cache,
                   jax.ShapeDtypeStruct((b, l, mvt.shape[2]), BF16)],
        compiler_params=_params(("parallel", "arbitrary")),
        name="swa_prompt",
    )(p, p, p, p, p, qn, kn, sinks, bd, p, mkb, mvt, mqn, bd)


def _swa_sample_kernel(q_ref, kn_ref_in, vn_ref_in, ck_ref, cv_ref, qn_ref, kn_ref, sink_ref,
                       bd_ref, *rest, nl, nprev):
    if nprev:
        pk_ref, pv_ref, tok_ref, kstack_ref, vstack_ref = rest
        kstack_ref[0:nprev] = pk_ref[...]
        vstack_ref[0:nprev] = pv_ref[...]
    else:
        tok_ref, kstack_ref, vstack_ref = rest
    kout_ref = kstack_ref.at[nprev]
    vout_ref = vstack_ref.at[nprev]
    nb = ck_ref.shape[0]
    w = ck_ref.shape[1]
    rows = SAMPLE_PAD
    nkeys = 2 * WINDOW
    bd = bd_ref[...]
    nh = SW_GROUP * SW_KV_HEADS
    masks = [_head_mask(w, k) for k in range(SW_KV_HEADS)]
    row = lax.broadcasted_iota(jnp.int32, (nh * rows, nkeys), 0)
    r = lax.broadcasted_iota(jnp.int32, (nh * rows, nkeys), 1)
    j = row & (rows - 1)
    valid = (r > j) & (r <= j + WINDOW)
    hrow = lax.broadcasted_iota(jnp.int32, (nh * rows, 1), 0) >> 3
    sink = jnp.zeros((nh * rows, 1), F32)
    for h in range(nh):
        sink = jnp.where(hrow == h, sink_ref[h], sink)
    zpad = jnp.zeros((WINDOW - nl, w), F32)
    zq = jnp.zeros((rows - nl, w), F32)
    is_new = lax.broadcasted_iota(jnp.int32, (1, WINDOW), 1) < nl
    knew_all = _head_rms64(kn_ref_in[...], kn_ref[...], bd)
    vnew_all = vn_ref_in[...]
    q_all = [_head_rms64(q_ref[:, g * w:(g + 1) * w], qn_ref[...], bd) * HEAD64 ** -0.5
             for g in range(SW_GROUP)]
    toks = []
    for b in range(nb):
        mine = slice(b * nl, (b + 1) * nl)
        ck, cv = ck_ref[b], cv_ref[b]
        knt = jnp.concatenate([knew_all[mine], zpad], axis=0).T
        vnt = jnp.concatenate([vnew_all[mine], zpad], axis=0).T
        kt = jnp.concatenate([ck, knt], axis=1)
        vt = jnp.concatenate([cv, vnt], axis=1)
        kout_ref[b] = pltpu.roll(jnp.where(is_new, knt, ck), WINDOW - nl, axis=1)
        vout_ref[b] = pltpu.roll(jnp.where(is_new, vnt, cv), WINDOW - nl, axis=1)
        pieces = []
        for g in range(SW_GROUP):
            qg = jnp.concatenate([q_all[g][mine], zq], axis=0)
            pieces += [(qg * masks[k]).astype(BF16) for k in range(SW_KV_HEADS)]
        s = _dot(jnp.concatenate(pieces, axis=0), kt.astype(BF16))
        s = jnp.where(valid, s, NEG)
        m = jnp.maximum(jnp.max(s, axis=-1, keepdims=True), sink)
        p = jnp.exp(s - m)
        den = jnp.sum(p, axis=-1, keepdims=True) + jnp.exp(sink - m)
        o = _dot_nt(p.astype(BF16), vt.astype(BF16)) / den
        accs = []
        for g in range(SW_GROUP):
            acc = jnp.zeros((rows, w), F32)
            for k in range(SW_KV_HEADS):
                h = g * SW_KV_HEADS + k
                acc += o[h * rows:(h + 1) * rows] * masks[k]
            accs.append(acc[0:nl])
        toks.append(jnp.concatenate(accs, axis=1))
    tok_ref[...] = jnp.concatenate(toks, axis=0).astype(tok_ref.dtype)


def _swa_sample(p, ckt, cvt, layer, qn, kn, sinks, bd, nl, nb, prev):
    b = ckt.shape[1]
    w = ckt.shape[2]
    nq = SW_GROUP
    nprev = 0 if prev is None else prev[0].shape[0]
    cache = jax.ShapeDtypeStruct((nprev + 1, b, w, WINDOW), F32)
    c4 = lambda i: (layer, i, 0, 0)
    stack = lambda n: pl.BlockSpec((n, nb, w, WINDOW), lambda i: (0, i, 0, 0))
    return pl.pallas_call(
        functools.partial(_swa_sample_kernel, nl=nl, nprev=nprev),
        grid=(b // nb,),
        in_specs=[
            pl.BlockSpec((nb * nl, nq * w), lambda i: (i, 0)),
            pl.BlockSpec((nb * nl, w), lambda i: (i, nq)),
            pl.BlockSpec((nb * nl, w), lambda i: (i, nq + 1)),
            pl.BlockSpec((None, nb, w, WINDOW), c4),
            pl.BlockSpec((None, nb, w, WINDOW), c4),
            pl.BlockSpec((1, w), lambda i: (0, 0)),
            pl.BlockSpec((1, w), lambda i: (0, 0)),
            pl.BlockSpec(memory_space=pltpu.SMEM),
            pl.BlockSpec((w, w), lambda i: (0, 0)),
        ] + ([stack(nprev)] * 2 if nprev else []),
        out_specs=[
            pl.BlockSpec((nb * nl, nq * w), lambda i: (i, 0)),
            stack(nprev + 1),
            stack(nprev + 1),
        ],
        out_shape=[jax.ShapeDtypeStruct((b * nl, nq * w), BF16), cache, cache],
        compiler_params=_params(("parallel",)),
        name="swa_sample",
    )(p, p, p, ckt, cvt, qn, kn, sinks, bd, *(() if prev is None else prev))


def _hgrn_consts(rows, group):
    t = np.arange(rows)
    loc, run = t % group, t // group
    same = run[:, None] == run[None, :]
    r = t[None, :]
    mats = [same & (r <= t[:, None]), same & (r > t[:, None])]
    masks = []
    h = group // 2
    while h >= 1:
        par = loc // (2 * h)
        right = (loc % (2 * h)) >= h
        bnd = (run * group + par * 2 * h + h - 1)[:, None]
        q_side = right[:, None] & (r > bnd) & (r <= t[:, None])
        k_side = (~right)[:, None] & (r > t[:, None]) & (r <= bnd)
        mats.append(q_side | k_side)
        masks.append(same & (par[:, None] == par[None, :]) & right[:, None] & (~right)[None, :])
        h //= 2
    masks.append(np.eye(rows, dtype=bool))
    w = np.concatenate(mats, axis=0).astype(np.float32)
    w2 = np.concatenate([w, w], axis=1)
    return jnp.asarray(w2, BF16), jnp.asarray(np.stack(masks).astype(np.float32)), len(masks) - 1


def _hgrn_gates(q_raw, z, lb):
    sp, sn = _sigmoid_pair(z)
    log2f = jnp.log2(lb + (1.0 - lb) * sp)
    kf = (1.0 - lb) * sn
    q = q_raw * _sigmoid(q_raw) * HG_HEAD ** -0.5
    return q, kf, log2f


def _hgrn_decays(log2f, w2_ref):
    hi = log2f.astype(BF16)
    lo = (log2f - hi.astype(F32)).astype(BF16)
    return jnp.exp2(_dot(w2_ref[...], jnp.concatenate([hi, lo], axis=0)))


def _hgrn_intra(q, kf, v, e, m_ref, nlev):
    rows = q.shape[0]
    att = m_ref[nlev] * _dot_nt(q.astype(BF16), kf.astype(BF16))
    for l in range(nlev):
        el = e[(2 + l) * rows:(3 + l) * rows]
        att += m_ref[l] * _dot_nt((q * el).astype(BF16), (kf * el).astype(BF16))
    return _dot(att.astype(BF16), v.astype(BF16))


def _hgrn_out(o, g, gn):
    ms = jnp.mean(o * o, axis=-1, keepdims=True)
    return (o * lax.rsqrt(ms + EPS) * gn * (g * _sigmoid(g))).astype(BF16)


def _hgrn_prompt_kernel(q_ref, f_ref, i_ref, g_ref, lb_ref, gn_ref, w2_ref, m_ref,
                        cq_ref, mkb_ref, mvt_ref, mqn_ref, mbd_ref,
                        tok_ref, st_ref, mo_ref, s_scr, *, nlev):
    t = pl.program_id(1)

    @pl.when(t == 0)
    def _():
        s_scr[...] = jnp.zeros_like(s_scr)

    mo_ref[...] = _mem_attend_rows(cq_ref[...], mkb_ref[...], mvt_ref[...], mqn_ref[...],
                                   mbd_ref[...]).astype(mo_ref.dtype)

    lb = lb_ref[...]
    gn = gn_ref[...]
    rows = HG_ROWS
    nh = q_ref.shape[1] // HG_HEAD

    def chunk(c, st):
        sl = pl.ds(pl.multiple_of(c * rows, rows), rows)
        v = i_ref[sl, :]
        g = g_ref[sl, :]
        q, kf, log2f = _hgrn_gates(q_ref[sl, :], f_ref[sl, :], lb)
        e = _hgrn_decays(log2f, w2_ref)
        qt = (q * e[0:rows]).astype(BF16)
        kh = (kf * e[rows:2 * rows]).astype(BF16)
        vb = v.astype(BF16)
        stb = st.astype(BF16)
        qb = q.astype(BF16)
        kb = kf.astype(BF16)
        ql, kl = [qb], [kb]
        rowi = lax.broadcasted_iota(jnp.int32, (rows, 1), 0)
        for l in range(nlev):
            half = rows >> (l + 1)
            if half >= 8:
                base = jnp.concatenate([(q if (r0 // half) & 1 else kf)[r0:r0 + half]
                                        for r0 in range(0, rows, half)], axis=0)
            else:
                base = jnp.where((rowi & (2 * half - 1)) >= half, q, kf)
            x = (base * e[(2 + l) * rows:(3 + l) * rows]).astype(BF16)
            ql.append(x)
            kl.append(x)
        mk = [m_ref[nlev]] + [m_ref[l] for l in range(nlev)]
        zr = jnp.zeros((rows, HG_HEAD), BF16)
        zs = jnp.zeros((HG_HEAD, HG_HEAD), BF16)

        def bdiag(a, b, z):
            return jnp.concatenate([jnp.concatenate([a, z], axis=1),
                                    jnp.concatenate([z, b], axis=1)], axis=0)

        upd = []
        for p in range(nh // 2):
            h0 = slice(2 * p * HG_HEAD, (2 * p + 1) * HG_HEAD)
            h1 = slice((2 * p + 1) * HG_HEAD, (2 * p + 2) * HG_HEAD)
            pr = slice(2 * p * HG_HEAD, (2 * p + 2) * HG_HEAD)
            att = jnp.zeros((rows, 2 * rows), F32)
            for a, b, m in zip(ql, kl, mk):
                att += m * _dot_nt(a[:, pr], bdiag(b[:, h0], b[:, h1], zr))
            o = _dot(att.astype(BF16), bdiag(vb[:, h0], vb[:, h1], zr))
            o += _dot_nt(qt[:, pr], bdiag(stb[:, h0], stb[:, h1], zs))
            for hs in (h0, h1):
                lo = hs.start - pr.start
                tok_ref[sl, hs] = _hgrn_out(o[:, lo:lo + HG_HEAD], g[:, hs], gn[:, hs])
                upd.append(_dot_tn(vb[:, hs], kh[:, hs]))
        return st * e[rows - 1:rows, :] + jnp.concatenate(upd, axis=1)

    st = lax.fori_loop(0, q_ref.shape[0] // rows, chunk, s_scr[...], unroll=True)
    s_scr[...] = st

    @pl.when(t == pl.num_programs(1) - 1)
    def _():
        for h in range(nh):
            st_ref[h] = st[:, h * HG_HEAD:(h + 1) * HG_HEAD].T


def _hgrn_prompt(p, lb, gn, tb, mkb, mvt, layer, mqn, bd):
    b, l, _ = p.shape
    tw = lb.shape[1]
    nh = tw // HG_HEAD
    mw = mvt.shape[2]
    w2, masks, nlev = _hgrn_consts(HG_ROWS, HG_ROWS)
    masks = jnp.tile(masks, (1, 1, 2))
    sec = lambda s: pl.BlockSpec((None, tb, tw), lambda i, t: (i, t, s))
    mem_in, mem_out = _mem_prompt_specs(mkb, mvt, layer, 4 * tw // mw, tb)
    return pl.pallas_call(
        functools.partial(_hgrn_prompt_kernel, nlev=nlev),
        grid=(b, l // tb),
        in_specs=[
            sec(0), sec(1), sec(2), sec(3),
            pl.BlockSpec((1, tw), lambda i, t: (0, 0)),
            pl.BlockSpec((1, tw), lambda i, t: (0, 0)),
            pl.BlockSpec(w2.shape, lambda i, t: (0, 0)),
            pl.BlockSpec(masks.shape, lambda i, t: (0, 0, 0)),
        ] + mem_in,
        out_specs=[
            pl.BlockSpec((None, tb, tw), lambda i, t: (i, t, 0)),
            pl.BlockSpec((None, nh, HG_HEAD, HG_HEAD), lambda i, t: (i, 0, 0, 0)),
            mem_out,
        ],
        out_shape=[jax.ShapeDtypeStruct((b, l, tw), BF16),
                   jax.ShapeDtypeStruct((b, nh, HG_HEAD, HG_HEAD), F32),
                   jax.ShapeDtypeStruct((b, l, mw), BF16)],
        scratch_shapes=[pltpu.VMEM((HG_HEAD, tw), F32)],
        compiler_params=_params(("parallel", "arbitrary")),
        name="hgrn_prompt",
    )(p, p, p, p, lb, gn, w2, masks, p, mkb, mvt, mqn, bd)


def _hgrn_sample_kernel(q_ref, f_ref, i_ref, g_ref, s0_ref, lb_ref, gn_ref, w2_ref, m_ref,
                        *rest, nlev, nl, nprev):
    if nprev:
        prev_ref, tok_ref, stack_ref = rest
        stack_ref[0:nprev] = prev_ref[...]
    else:
        tok_ref, stack_ref = rest
    st_ref = stack_ref.at[nprev]
    rows = q_ref.shape[0]
    nb = rows // nl
    shift = nl.bit_length() - 1
    rowb = lax.broadcasted_iota(jnp.int32, (rows, 1), 0) >> shift
    colb = lax.broadcasted_iota(jnp.int32, (rows, nb * HG_HEAD), 1) >> 7
    v_all = i_ref[...]
    q_all, kf_all, log2f = _hgrn_gates(q_ref[...], f_ref[...], lb_ref[...])
    e_all = _hgrn_decays(log2f, w2_ref)
    g_all = g_ref[...]
    gn_all = gn_ref[...]
    for hh in range(q_ref.shape[1] // HG_HEAD):
        hs = slice(hh * HG_HEAD, (hh + 1) * HG_HEAD)
        q, kf, v, e = q_all[:, hs], kf_all[:, hs], v_all[:, hs], e_all[:, hs]
        o_intra = _hgrn_intra(q, kf, v, e, m_ref, nlev)
        e_cum = e[0:rows]
        qt = q * e_cum
        kh = kf * e[rows:2 * rows]
        s0cat = jnp.concatenate([s0_ref[b, hh].astype(BF16) for b in range(nb)], axis=1)
        full = _dot(qt.astype(BF16), s0cat)
        o = o_intra
        for b in range(nb):
            o += jnp.where(rowb == b, full[:, b * HG_HEAD:(b + 1) * HG_HEAD], 0.0)
        tok_ref[:, hs] = _hgrn_out(o, g_all[:, hs], gn_all[:, hs])
        xt = jnp.concatenate([kh, e_cum], axis=0).T
        vbd = jnp.where(rowb == colb, jnp.concatenate([v] * nb, axis=1), 0.0).astype(BF16)
        upd = _dot(xt[:, 0:rows].astype(BF16), vbd)
        for b in range(nb):
            last = rows + b * nl + nl - 1
            st_ref[b, hh] = (s0_ref[b, hh] * xt[:, last:last + 1]
                             + upd[:, b * HG_HEAD:(b + 1) * HG_HEAD])


def _hgrn_sample(p, s0, layer, lb, gn, nl, prev):
    rows = HG_ROWS
    nb = rows // nl
    b, nh = s0.shape[1], s0.shape[2]
    hp = 2
    hw = hp * HG_HEAD
    nprev = 0 if prev is None else prev.shape[0]
    w2, masks, nlev = _hgrn_consts(rows, nl)
    sec = lambda s: pl.BlockSpec((rows, hw), lambda i, h: (i, s * (nh // hp) + h))
    stack = lambda n: pl.BlockSpec((n, nb, hp, HG_HEAD, HG_HEAD), lambda i, h: (0, i, h, 0, 0))
    s0_spec = pl.BlockSpec((None, nb, hp, HG_HEAD, HG_HEAD), lambda i, h: (layer, i, h, 0, 0))
    return pl.pallas_call(
        functools.partial(_hgrn_sample_kernel, nlev=nlev, nl=nl, nprev=nprev),
        grid=(b // nb, nh // hp),
        in_specs=[
            sec(0), sec(1), sec(2), sec(3), s0_spec,
            pl.BlockSpec((1, hw), lambda i, h: (0, h)),
            pl.BlockSpec((1, hw), lambda i, h: (0, h)),
            pl.BlockSpec(w2.shape, lambda i, h: (0, 0)),
            pl.BlockSpec(masks.shape, lambda i, h: (0, 0, 0)),
        ] + ([stack(nprev)] if nprev else []),
        out_specs=[pl.BlockSpec((rows, hw), lambda i, h: (i, h)), stack(nprev + 1)],
        out_shape=[jax.ShapeDtypeStruct((b * nl, nh * HG_HEAD), BF16),
                   jax.ShapeDtypeStruct((nprev + 1,) + s0.shape[1:], F32)],
        compiler_params=_params(("parallel", "parallel")),
        name="hgrn_sample",
    )(p, p, p, p, s0, lb, gn, w2, masks, *(() if prev is None else (prev,)))


def _swa_regroup(w, axis):
    axis %= w.ndim
    shape = w.shape[:axis] + (SW_KV_HEADS, SW_GROUP, HEAD64) + w.shape[axis + 1:]
    return jnp.swapaxes(w.reshape(shape), axis, axis + 1).reshape(w.shape)


def kernel(x_prompt, x_sample, cache_mem_k, cache_mem_v, state_hgrn, cache_swa_k, cache_swa_v,
           mem_prompt, norm_mix, norm_ffn, norm_mem, w_in_hgrn, hgrn_lb_raw, hgrn_out_norm,
           w_in_swa, swa_q_norm, swa_k_norm, swa_sinks, w_mem_kv, mem_q_norm, mem_k_norm,
           w_out, w_ffn_in, w_ffn_out):
    depth, d = norm_mix.shape
    bp, lp, _ = x_prompt.shape
    bs, ls, _ = x_sample.shape
    nm = mem_prompt.shape[1]
    mw = MEM_HEADS * HEAD64
    kvw = SW_KV_HEADS * HEAD64
    tokw = d - mw
    tm = DENSE_ROWS

    seg = np.arange(mw) // HEAD64
    bd = jnp.asarray((seg[:, None] == seg[None, :]).astype(np.float32) / HEAD64, BF16)
    tile4 = lambda a: jnp.tile(a.astype(F32), (1, mw // HEAD64))[:, None, :]

    sm = jax.nn.softmax(hgrn_lb_raw.astype(F32), axis=0)
    lbs = jnp.clip(jnp.cumsum(sm, axis=0) - sm[0], 0.0, LB_MAX)

    w_in_swa_b = w_in_swa.astype(BF16)
    w_in_swa_p = jnp.concatenate([_swa_regroup(w_in_swa_b[..., :tokw], -1),
                                  w_in_swa_b[..., tokw:]], axis=-1)
    w_in_hgrn_b = w_in_hgrn.astype(BF16)
    w_out_b = w_out.astype(BF16)
    w_out_b = jnp.stack([w_out_b[i] if i % 2 == 0 else
                         jnp.concatenate([_swa_regroup(w_out_b[i, :tokw], 0), w_out_b[i, tokw:]],
                                         axis=0)
                         for i in range(depth)])
    w1_b = w_ffn_in.astype(BF16)
    w2_b = w_ffn_out.astype(BF16)
    sinks_p = swa_sinks.astype(F32).reshape(-1, SW_KV_HEADS, SW_GROUP).transpose(0, 2, 1)
    sinks_p = sinks_p.reshape(-1, SW_KV_HEADS * SW_GROUP)
    mem_qn = tile4(mem_q_norm)
    swa_qn = tile4(swa_q_norm)
    swa_kn = tile4(swa_k_norm)

    mk_p, mv_p, mk_b = _mem_kv(mem_prompt, norm_mem[:, None, :], w_mem_kv.astype(BF16),
                               tile4(mem_k_norm), bd)

    to_t = lambda a: a.transpose(0, 1, 3, 4, 2).reshape(a.shape[0], a.shape[1], -1, a.shape[2])
    from_t = lambda a, heads: a.reshape(a.shape[:-2] + (heads, HEAD64, a.shape[-1])).transpose(
        *range(a.ndim - 2), a.ndim, a.ndim - 2, a.ndim - 1)
    cmk_t, cmv_t = to_t(cache_mem_k), to_t(cache_mem_v)
    csk_t, csv_t = to_t(cache_swa_k), to_t(cache_swa_v)

    xp = x_prompt.reshape(bp * lp, d)
    xs = x_sample.reshape(bs * ls, d)
    g_mix = norm_mix[:, None, :]
    g_ffn = norm_ffn[:, None, :]
    hg_p, swk_p, swv_p = [], [], []
    hg_s = sw_s = None
    for i in range(depth):
        j = i // 2
        if i % 2 == 0:
            lb = lbs[j][None, :]
            gn = hgrn_out_norm[j][None, :].astype(F32)
            pp, ps = _norm_matmul(xp, xs, g_mix, i, w_in_hgrn_b, j, tm)
            pp = pp.reshape(bp, lp, -1)
            tok_p, st_p, mo_p = _hgrn_prompt(pp, lb, gn, HGRN_ROWS, mk_b, mv_p, i, mem_qn[i], bd)
            tok_s, hg_s = _hgrn_sample(ps, state_hgrn, j, lb, gn, ls, hg_s)
            hg_p.append(st_p)
            cq_col = 4 * tokw // mw
        else:
            pp, ps = _norm_matmul(xp, xs, g_mix, i, w_in_swa_p, j, tm)
            pp = pp.reshape(bp, lp, -1)
            tok_p, kb, vb, mo_p = _swa_prompt(pp, swa_qn[j], swa_kn[j], sinks_p[j], bd,
                                              SWA_WINDOWS, mk_b, mv_p, i, mem_qn[i])
            swk_p.append(kb)
            swv_p.append(vb)
            tok_s, *sw_s = _swa_sample(ps, csk_t, csv_t, j, swa_qn[j], swa_kn[j],
                                       sinks_p[j], bd, ls, SWA_DECODE_SEQS, sw_s)
            cq_col = (tokw + 2 * kvw) // mw
        mo_s = _mem_attend(ps, cq_col, cmk_t, cmv_t, i, mem_qn[i], bd, MEM_DECODE_SEQS, ls)
        xp, xs = _mix_ffn(xp, tok_p.reshape(bp * lp, tokw), mo_p.reshape(bp * lp, mw),
                          xs, tok_s, mo_s, w_out_b, g_ffn, w1_b, w2_b, i, tm)

    return (xp.reshape(bp, lp, d), xs.reshape(bs, ls, d),
            from_t(mk_p, MEM_HEADS), from_t(mv_p, MEM_HEADS),
            jnp.stack(hg_p), hg_s,
            from_t(jnp.stack(swk_p), SW_KV_HEADS), from_t(jnp.stack(swv_p), SW_KV_HEADS),
            from_t(sw_s[0], SW_KV_HEADS), from_t(sw_s[1], SW_KV_HEADS))
```
